```python
import math
import jax, jax.numpy as jnp
from jax import lax
import numpy as np

D_MODEL = 4096
BATCH = 4
SEQ = 4096
DEPTH = 1

NORM_EPS = 1e-6
GLA_HEADS = 16
GLA_DK = 64
GLA_DV = 128
GLA_KEY_W = GLA_HEADS * GLA_DK
GLA_VAL_W = GLA_HEADS * GLA_DV
GLA_GATE_RANK = 16
GLA_TAU = 16.0
GLA_CHUNK = 64
NSA_HEADS = 16
NSA_KV_HEADS = 4
NSA_GROUP = NSA_HEADS // NSA_KV_HEADS
NSA_DIM = 128
NSA_Q_W = NSA_HEADS * NSA_DIM
NSA_KV_W = NSA_KV_HEADS * NSA_DIM
CMP_BLOCK = 32
CMP_STRIDE = 16
PHI_HIDDEN = 128
SEL_BLOCK = 64
SEL_TOPK = 16
SEL_LOCAL = 2
SEL_QUERY_BLOCK = 32
WINDOW = 512
WIN_QUERY_BLOCK = 128
REL_BUCKETS = 32
REL_MAX_DIST = 128
D_FF = 4 * D_MODEL

NEG_INF = -1e30
FORCE_SCORE = 1e4

IN_SPLITS = (GLA_KEY_W, GLA_KEY_W, GLA_VAL_W, GLA_GATE_RANK, GLA_VAL_W,
             NSA_Q_W, NSA_KV_W, NSA_KV_W, NSA_KV_W, NSA_KV_W, NSA_KV_W, NSA_KV_W, 3 * NSA_HEADS,
             D_MODEL, D_MODEL)
D_IN_PROJ = sum(IN_SPLITS)
SPLIT_POINTS = [int(v) for v in np.cumsum(IN_SPLITS)[:-1]]

kernel_name = 'hybrid_gla_nsa_gated_block'


def _rmsnorm(x, g):
    xf = x.astype(jnp.float32)
    y = xf * lax.rsqrt(jnp.mean(xf * xf, axis=-1, keepdims=True) + NORM_EPS)
    return (y * g.astype(jnp.float32)).astype(x.dtype)


def _rel_bucket(dist):
    n = jnp.maximum(dist, 0)
    max_exact = REL_BUCKETS // 2
    nf = jnp.maximum(n, max_exact).astype(jnp.float32)
    large = max_exact + (jnp.log(nf / max_exact) / math.log(REL_MAX_DIST / max_exact)
                         * (REL_BUCKETS - max_exact)).astype(jnp.int32)
    large = jnp.minimum(large, REL_BUCKETS - 1)
    return jnp.where(n < max_exact, n, large)


def _masked_softmax(logits, valid):
    p = jax.nn.softmax(jnp.where(valid, logits, NEG_INF), axis=-1)
    return jnp.where(valid, p, 0.0)


def _cmp_to_sel_matrix(n_c, n_s):
    m_mat = np.zeros((n_c, n_s), np.float32)
    j = np.arange(n_s)
    for m in range(SEL_BLOCK // CMP_STRIDE):
        for n in range(CMP_BLOCK // CMP_STRIDE):
            c = (SEL_BLOCK // CMP_STRIDE) * j + m - n
            ok = (c >= 0) & (c < n_c)
            np.add.at(m_mat, (c[ok], j[ok]), 1.0)
    return jnp.asarray(m_mat)


def _gla_mixer(q, k, v, a_lr, r, w_alpha2, b_alpha, norm_g):
    B, S, _ = q.shape
    N, C, H = S // GLA_CHUNK, GLA_CHUNK, GLA_HEADS

    def chunks(t, d):
        return t.reshape(B, N, C, H, d).transpose(0, 3, 1, 2, 4)

    log_a = jax.nn.log_sigmoid((a_lr @ w_alpha2 + b_alpha).astype(jnp.float32)) / GLA_TAU
    bcum = jnp.cumsum(chunks(log_a, GLA_DK), axis=3)
    qc = chunks(q, GLA_DK).astype(jnp.float32) * (GLA_DK ** -0.5)
    kc = chunks(k, GLA_DK).astype(jnp.float32)
    vc = chunks(v, GLA_DV).astype(jnp.float32)
    qe = qc * jnp.exp(bcum)
    ke = kc * jnp.exp(-bcum)
    kd = kc * jnp.exp(bcum[:, :, :, -1:, :] - bcum)
    causal = jnp.asarray(np.tril(np.ones((C, C), dtype=bool)))
    attn = jnp.where(causal, jnp.einsum('bhncd,bhnsd->bhncs', qe, ke), 0.0)
    o_intra = jnp.einsum('bhncs,bhnse->bhnce', attn, vc)
    kv = jnp.einsum('bhnsd,bhnse->bhnde', kd, vc)
    decay = jnp.exp(bcum[:, :, :, -1, :])

    def step(state, inp):
        dec, kv_n = inp
        return dec[..., None] * state + kv_n, state

    init = jnp.zeros((B, H, GLA_DK, GLA_DV), jnp.float32)
    _, prev = lax.scan(step, init, (jnp.moveaxis(decay, 2, 0), jnp.moveaxis(kv, 2, 0)))
    o_inter = jnp.einsum('bhncd,nbhde->bhnce', qe, prev)
    o = (o_intra + o_inter).transpose(0, 2, 3, 1, 4).reshape(B, S, H, GLA_DV)
    o = _rmsnorm(o, norm_g)
    return (o.reshape(B, S, GLA_VAL_W) * jax.nn.silu(r.astype(jnp.float32))).astype(q.dtype)


def _nsa_mixer(q, k_cmp, v_cmp, k_sel, v_sel, k_win, v_win, gate_logits,
               pos_k, pos_v, phi_k_w1, phi_k_w2, phi_v_w1, phi_v_w2, rel_table):
    B, S, _ = q.shape
    G, R, d = NSA_KV_HEADS, NSA_GROUP, NSA_DIM
    scale = d ** -0.5
    q5 = q.reshape(B, S, G, R, d)
    k_cmp, v_cmp, k_sel, v_sel, k_win, v_win = [t.reshape(B, S, G, d) for t in
                                                 (k_cmp, v_cmp, k_sel, v_sel, k_win, v_win)]
    table_gr = rel_table.astype(jnp.float32).reshape(REL_BUCKETS, G, R)
    pos_t = jnp.arange(S, dtype=jnp.int32)

    n_c = (S - CMP_BLOCK) // CMP_STRIDE + 1
    blk_idx = np.arange(n_c)[:, None] * CMP_STRIDE + np.arange(CMP_BLOCK)[None, :]

    def compress(t, pos, w1, w2):
        blocks = t[:, blk_idx] + pos[None, None, :, None, :]
        flat = blocks.transpose(0, 1, 3, 2, 4).reshape(B, n_c, G, CMP_BLOCK * d)
        return jax.nn.relu(flat @ w1) @ w2

    kc = compress(k_cmp, pos_k, phi_k_w1, phi_k_w2)
    vc = compress(v_cmp, pos_v, phi_v_w1, phi_v_w2)
    cmp_end = jnp.arange(n_c, dtype=jnp.int32) * CMP_STRIDE + (CMP_BLOCK - 1)
    dist_c = pos_t[:, None] - cmp_end[None, :]
    bias_c = table_gr[_rel_bucket(dist_c)].transpose(2, 3, 0, 1)
    logits_c = jnp.einsum('bsgrd,bcgd->bgrsc', q5, kc).astype(jnp.float32) * scale + bias_c
    p_c = _masked_softmax(logits_c, dist_c >= 0)
    o_cmp = jnp.einsum('bgrsc,bcgd->bsgrd', p_c.astype(vc.dtype), vc)

    n_s = S // SEL_BLOCK
    top_n = min(SEL_TOPK, n_s)
    imp = jnp.einsum('bgrsc,cj->bgsj', p_c, _cmp_to_sel_matrix(n_c, n_s))
    blk = jnp.arange(n_s, dtype=jnp.int32)
    cur = pos_t // SEL_BLOCK
    forced = (blk[None, :] == 0) | ((blk[None, :] <= cur[:, None]) &
                                     (blk[None, :] > cur[:, None] - SEL_LOCAL))
    reachable = blk[None, :] * SEL_BLOCK <= pos_t[:, None]
    score = jnp.where(forced, FORCE_SCORE, jnp.where(reachable, imp, -FORCE_SCORE))
    _, sel_idx = lax.top_k(score, top_n)
    ks_blk = k_sel.reshape(B, n_s, SEL_BLOCK, G, d).transpose(0, 3, 1, 2, 4)
    vs_blk = v_sel.reshape(B, n_s, SEL_BLOCK, G, d).transpose(0, 3, 1, 2, 4)
    b_ix = jnp.arange(B)[:, None, None, None]
    g_ix = jnp.arange(G)[None, :, None, None]
    n_key = top_n * SEL_BLOCK
    QS = SEL_QUERY_BLOCK

    def sel_chunk(i):
        q0 = i * QS
        qb = lax.dynamic_slice_in_dim(q5, q0, QS, axis=1)
        idx = lax.dynamic_slice_in_dim(sel_idx, q0, QS, axis=2)
        kg = ks_blk[b_ix, g_ix, idx].reshape(B, G, QS, n_key, d)
        vg = vs_blk[b_ix, g_ix, idx].reshape(B, G, QS, n_key, d)
        kpos = (idx[..., None] * SEL_BLOCK + jnp.arange(SEL_BLOCK, dtype=jnp.int32)).reshape(B, G, QS, n_key)
        dist = (q0 + jnp.arange(QS, dtype=jnp.int32))[None, None, :, None] - kpos
        bias = table_gr[_rel_bucket(dist), g_ix].transpose(0, 1, 4, 2, 3)
        logits = jnp.einsum('bqgrd,bgqkd->bgrqk', qb, kg).astype(jnp.float32) * scale + bias
        p = _masked_softmax(logits, (dist >= 0)[:, :, None])
        return jnp.einsum('bgrqk,bgqkd->bqgrd', p.astype(vg.dtype), vg)

    o_sel = lax.map(sel_chunk, jnp.arange(S // QS, dtype=jnp.int32))
    o_sel = jnp.moveaxis(o_sel, 0, 1).reshape(B, S, G, R, d)

    QB = WIN_QUERY_BLOCK
    span = QB + WINDOW
    pad = ((0, 0), (WINDOW, 0), (0, 0), (0, 0))
    kw_pad = jnp.pad(k_win, pad)
    vw_pad = jnp.pad(v_win, pad)

    def win_block(i):
        s0 = i * QB
        qb = lax.dynamic_slice_in_dim(q5, s0, QB, axis=1)
        kb = lax.dynamic_slice_in_dim(kw_pad, s0, span, axis=1)
        vb = lax.dynamic_slice_in_dim(vw_pad, s0, span, axis=1)
        tq = s0 + jnp.arange(QB, dtype=jnp.int32)
        kpos = s0 - WINDOW + jnp.arange(span, dtype=jnp.int32)
        dist = tq[:, None] - kpos[None, :]
        valid = (dist >= 0) & (dist < WINDOW) & (kpos[None, :] >= 0)
        bias = table_gr[_rel_bucket(dist)].transpose(2, 3, 0, 1)
        logits = jnp.einsum('bqgrd,bkgd->bgrqk', qb, kb).astype(jnp.float32) * scale + bias
        p = _masked_softmax(logits, valid)
        return jnp.einsum('bgrqk,bkgd->bqgrd', p.astype(vb.dtype), vb)

    o_win = lax.map(win_block, jnp.arange(S // QB, dtype=jnp.int32))
    o_win = jnp.moveaxis(o_win, 0, 1).reshape(B, S, G, R, d)

    gates = jax.nn.sigmoid(gate_logits.astype(jnp.float32)).reshape(B, S, G, R, 3)
    o = gates[..., 0:1] * o_cmp + gates[..., 1:2] * o_sel + gates[..., 2:3] * o_win
    return o.reshape(B, S, NSA_Q_W).astype(q.dtype)


def setup_inputs(seed: int = 0) -> dict:
    key = jax.random.key(seed)
    ks = jax.random.split(key, 22)
    f32 = jnp.float32
    L = DEPTH

    def nrm(k, shape, scale):
        return jax.random.normal(k, shape, f32) * scale

    def gain(k, shape):
        return 1.0 + nrm(k, shape, 0.02)

    return {
        'x': nrm(ks[0], (BATCH, SEQ, D_MODEL), 1.0),
        'g_mix_norm': gain(ks[1], (L, D_MODEL)),
        'w_in': nrm(ks[2], (L, D_MODEL, D_IN_PROJ), D_MODEL ** -0.5),
        'w_alpha2': nrm(ks[3], (L, GLA_GATE_RANK, GLA_KEY_W), GLA_GATE_RANK ** -0.5),
        'b_alpha': nrm(ks[4], (L, GLA_KEY_W), 0.1),
        'gla_norm_g': gain(ks[5], (L, GLA_HEADS, GLA_DV)),
        'cmp_pos_k': nrm(ks[6], (L, CMP_BLOCK, NSA_DIM), 0.02),
        'cmp_pos_v': nrm(ks[7], (L, CMP_BLOCK, NSA_DIM), 0.02),
        'phi_k_w1': nrm(ks[8], (L, CMP_BLOCK * NSA_DIM, PHI_HIDDEN), (CMP_BLOCK * NSA_DIM) ** -0.5),
        'phi_k_w2': nrm(ks[9], (L, PHI_HIDDEN, NSA_DIM), PHI_HIDDEN ** -0.5),
        'phi_v_w1': nrm(ks[10], (L, CMP_BLOCK * NSA_DIM, PHI_HIDDEN), (CMP_BLOCK * NSA_DIM) ** -0.5),
        'phi_v_w2': nrm(ks[11], (L, PHI_HIDDEN, NSA_DIM), PHI_HIDDEN ** -0.5),
        'rel_bias_table': nrm(ks[12], (REL_BUCKETS, NSA_HEADS), 0.5),
        'w_gla_proj': nrm(ks[13], (L, GLA_VAL_W, D_MODEL), GLA_VAL_W ** -0.5),
        'w_nsa_proj': nrm(ks[14], (L, NSA_Q_W, D_MODEL), NSA_Q_W ** -0.5),
        'w_out': nrm(ks[15], (L, D_MODEL, D_MODEL), D_MODEL ** -0.5),
        'g_mlp_norm': gain(ks[16], (L, D_MODEL)),
        'w_up': nrm(ks[17], (L, D_MODEL, D_FF), D_MODEL ** -0.5),
        'w_down': nrm(ks[18], (L, D_FF, D_MODEL), D_FF ** -0.5),
        'g_final_norm': gain(ks[19], (D_MODEL,)),
    }


def reference(x, g_mix_norm, w_in, w_alpha2, b_alpha, gla_norm_g, cmp_pos_k, cmp_pos_v,
              phi_k_w1, phi_k_w2, phi_v_w1, phi_v_w2, rel_bias_table, w_gla_proj, w_nsa_proj,
              w_out, g_mlp_norm, w_up, w_down, g_final_norm):
    for l in range(DEPTH):
        h = _rmsnorm(x, g_mix_norm[l])
        u = h @ w_in[l]
        (gq, gk, gv, ga, gr, nq, kc, vc, ksl, vsl, kw, vw, ng, m_gla, m_nsa) = jnp.split(u, SPLIT_POINTS, axis=-1)
        o_gla = _gla_mixer(gq, gk, gv, ga, gr, w_alpha2[l], b_alpha[l], gla_norm_g[l])
        o_nsa = _nsa_mixer(nq, kc, vc, ksl, vsl, kw, vw, ng, cmp_pos_k[l], cmp_pos_v[l],
                           phi_k_w1[l], phi_k_w2[l], phi_v_w1[l], phi_v_w2[l], rel_bias_table)
        mix = (jax.nn.sigmoid(m_gla) * (o_gla @ w_gla_proj[l])
               + jax.nn.sigmoid(m_nsa) * (o_nsa @ w_nsa_proj[l]))
        x = x + mix @ w_out[l]
        h = _rmsnorm(x, g_mlp_norm[l])
        x = x + jnp.square(jax.nn.relu(h @ w_up[l])) @ w_down[l]
    return _rmsnorm(x, g_final_norm)
```

```python
import functools
import math

import numpy as np
import jax
import jax.numpy as jnp
from jax import lax
from jax.experimental import pallas as pl
from jax.experimental.pallas import tpu as pltpu

F32 = jnp.float32
BF16 = jnp.bfloat16
HIGHEST = lax.Precision.HIGHEST

NORM_EPS = 1e-6
GLA_HEADS = 16
GLA_DK = 64
GLA_DV = 128
GLA_KEY_W = GLA_HEADS * GLA_DK
GLA_VAL_W = GLA_HEADS * GLA_DV
GLA_GATE_RANK = 16
GLA_TAU = 16.0
GLA_CHUNK = 64
NSA_HEADS = 16
NSA_KV_HEADS = 4
NSA_GROUP = NSA_HEADS // NSA_KV_HEADS
NSA_DIM = 128
NSA_Q_W = NSA_HEADS * NSA_DIM
NSA_KV_W = NSA_KV_HEADS * NSA_DIM
CMP_BLOCK = 32
CMP_STRIDE = 16
SEL_BLOCK = 64
SEL_TOPK = 16
SEL_LOCAL = 2
WINDOW = 512
REL_BUCKETS = 32
REL_MAX_DIST = 128
NEG_INF = -1e30
FORCE_SCORE = 1e4

LANES = 128
VMEM_LIMIT = 56 * 1024 * 1024
ATT_TILE = 256
MM_TILE = 1024

U_GQ = 0
U_GK = U_GQ + GLA_KEY_W
U_GV = U_GK + GLA_KEY_W
U_GR = U_GV + GLA_VAL_W
U_NQ = U_GR + GLA_VAL_W
U_MG = U_NQ + NSA_Q_W


def _params(sem):
    return pltpu.CompilerParams(dimension_semantics=sem, vmem_limit_bytes=VMEM_LIMIT)


def _rmsnorm_kernel(x_ref, g_ref, o_ref):
    x = x_ref[...].astype(F32)
    ms = jnp.mean(x * x, axis=-1, keepdims=True)
    o_ref[...] = (x * lax.rsqrt(ms + NORM_EPS) * g_ref[...]).astype(o_ref.dtype)


def _rmsnorm(x2, g, out_dtype, tm=256):
    t, d = x2.shape
    return pl.pallas_call(
        _rmsnorm_kernel,
        grid=(t // tm,),
        in_specs=[pl.BlockSpec((tm, d), lambda i: (i, 0)),
                  pl.BlockSpec((1, d), lambda i: (0, 0))],
        out_specs=pl.BlockSpec((tm, d), lambda i: (i, 0)),
        out_shape=jax.ShapeDtypeStruct((t, d), out_dtype),
        compiler_params=_params(("parallel",)),
        name="rmsnorm",
    )(x2, g.reshape(1, d).astype(F32))


def _mm_kernel(a_ref, b_ref, o_ref, *, act):
    acc = jnp.dot(a_ref[...], b_ref[...], preferred_element_type=F32)
    if act == "relu2":
        r = jnp.maximum(acc, 0.0)
        acc = r * r
    o_ref[...] = acc.astype(o_ref.dtype)


def _matmul(a, b, out_dtype, tm, tn, act=None, name="matmul"):
    m, k = a.shape
    _, n = b.shape
    return pl.pallas_call(
        functools.partial(_mm_kernel, act=act),
        grid=(m // tm, n // tn),
        in_specs=[pl.BlockSpec((tm, k), lambda i, j: (i, 0)),
                  pl.BlockSpec((k, tn), lambda i, j: (0, j))],
        out_specs=pl.BlockSpec((tm, tn), lambda i, j: (i, j)),
        out_shape=jax.ShapeDtypeStruct((m, n), out_dtype),
        compiler_params=_params(("parallel", "arbitrary")),
        name=name,
    )(a, b)


def _mix_kernel(og_ref, wg_ref, on_ref, wn_ref, mg_ref, mn_ref, o_ref):
    yg = jnp.dot(og_ref[...], wg_ref[...], preferred_element_type=F32)
    yn = jnp.dot(on_ref[...], wn_ref[...], preferred_element_type=F32)
    o = (jax.nn.sigmoid(mg_ref[...].astype(F32)) * yg
         + jax.nn.sigmoid(mn_ref[...].astype(F32)) * yn)
    o_ref[...] = o.astype(o_ref.dtype)


def _mix(o_gla, w_g, o_nsa, w_n, u, d, tm=MM_TILE, tn=MM_TILE):
    t = o_gla.shape[0]
    mg_blk = U_MG // tn
    mn_blk = (U_MG + d) // tn
    return pl.pallas_call(
        _mix_kernel,
        grid=(t // tm, d // tn),
        in_specs=[pl.BlockSpec((tm, o_gla.shape[1]), lambda i, j: (i, 0)),
                  pl.BlockSpec((w_g.shape[0], tn), lambda i, j: (0, j)),
                  pl.BlockSpec((tm, o_nsa.shape[1]), lambda i, j: (i, 0)),
                  pl.BlockSpec((w_n.shape[0], tn), lambda i, j: (0, j)),
                  pl.BlockSpec((tm, tn), lambda i, j: (i, mg_blk + j)),
                  pl.BlockSpec((tm, tn), lambda i, j: (i, mn_blk + j))],
        out_specs=pl.BlockSpec((tm, tn), lambda i, j: (i, j)),
        out_shape=jax.ShapeDtypeStruct((t, d), BF16),
        compiler_params=_params(("parallel", "arbitrary")),
        name="mix",
    )(o_gla, w_g, o_nsa, w_n, u, u)


def _mm_res_kernel(a_ref, b_ref, r_ref, o_ref, acc_ref):
    kk = pl.program_id(2)

    @pl.when(kk == 0)
    def _():
        acc_ref[...] = r_ref[...]

    acc_ref[...] += jnp.dot(a_ref[...], b_ref[...], preferred_element_type=F32)

    @pl.when(kk == pl.num_programs(2) - 1)
    def _():
        o_ref[...] = acc_ref[...]


def _matmul_residual(a, b, res, tm, tn, tk, name):
    m, k = a.shape
    _, n = b.shape
    return pl.pallas_call(
        _mm_res_kernel,
        grid=(m // tm, n // tn, k // tk),
        in_specs=[pl.BlockSpec((tm, tk), lambda i, j, kk: (i, kk)),
                  pl.BlockSpec((tk, tn), lambda i, j, kk: (kk, j)),
                  pl.BlockSpec((tm, tn), lambda i, j, kk: (i, j))],
        out_specs=pl.BlockSpec((tm, tn), lambda i, j, kk: (i, j)),
        out_shape=jax.ShapeDtypeStruct((m, n), F32),
        scratch_shapes=[pltpu.VMEM((tm, tn), F32)],
        compiler_params=_params(("parallel", "arbitrary", "arbitrary")),
        name=name,
    )(a, b, res)


def _gla_kernel(q_ref, k_ref, v_ref, r_ref, a_ref, wa_ref, ba_ref, g_ref, o_ref,
                la_ref, st_ref, *, seq):
    c = GLA_CHUNK
    dv = GLA_DV
    z = jnp.dot(a_ref[...], wa_ref[...], preferred_element_type=F32,
                precision=HIGHEST) + ba_ref[...]
    log_sig = jnp.minimum(z, 0.0) - jnp.log1p(jnp.exp(-jnp.abs(z)))
    la_ref[...] = log_sig * (1.0 / GLA_TAU)
    st_ref[...] = jnp.zeros_like(st_ref)

    row = lax.broadcasted_iota(jnp.int32, (c, c), 0)
    col = lax.broadcasted_iota(jnp.int32, (c, c), 1)
    causal = row >= col
    tri = causal.astype(F32)
    lane = lax.broadcasted_iota(jnp.int32, (1, 2 * GLA_DK), 1)
    head_mask = [(lane < GLA_DK).astype(F32), (lane >= GLA_DK).astype(F32)]
    gvec = g_ref[...]
    nt = (((1,), (1,)), ((), ()))
    tn = (((0,), (0,)), ((), ()))

    def body(i, carry):
        r0 = pl.multiple_of(i * c, c)
        la = la_ref[pl.ds(r0, c), :]
        bc = jnp.dot(tri, la, preferred_element_type=F32, precision=HIGHEST)
        last = bc[c - 1:c, :]
        q = q_ref[pl.ds(r0, c), :].astype(F32) * (GLA_DK ** -0.5)
        k = k_ref[pl.ds(r0, c), :].astype(F32)
        qe = q * jnp.exp(bc)
        ke = (k * jnp.exp(-bc)).astype(BF16)
        kd = (k * jnp.exp(last - bc)).astype(BF16)
        v = v_ref[pl.ds(r0, c), :]
        st = st_ref[...]
        stb = st.astype(BF16)
        outs = []
        for h in range(2):
            qh = (qe * head_mask[h]).astype(BF16)
            att = lax.dot_general(qh, ke, nt, preferred_element_type=F32)
            att = jnp.where(causal, att, 0.0).astype(BF16)
            oh = jnp.dot(att, v[:, h * dv:(h + 1) * dv], preferred_element_type=F32)
            oh = oh + lax.dot_general(qh, stb[h * dv:(h + 1) * dv, :], nt,
                                      preferred_element_type=F32)
            ms = jnp.mean(oh * oh, axis=-1, keepdims=True)
            y = oh * lax.rsqrt(ms + NORM_EPS) * gvec[:, h * dv:(h + 1) * dv]
            rr = r_ref[pl.ds(r0, c), h * dv:(h + 1) * dv].astype(F32)
            outs.append(y * (rr * jax.nn.sigmoid(rr)))
        o_ref[pl.ds(r0, c), :] = jnp.concatenate(outs, axis=1).astype(o_ref.dtype)
        kv = lax.dot_general(v, kd, tn, preferred_element_type=F32)
        st_ref[...] = st * jnp.exp(last) + kv
        return carry

    lax.fori_loop(0, seq // c, body, 0)


def _gla(u, small, w_alpha_pad, b_alpha, norm_g, batch, seq):
    t = batch * seq
    pairs = GLA_HEADS // 2
    kw = 2 * GLA_DK
    vw = 2 * GLA_DV
    return pl.pallas_call(
        functools.partial(_gla_kernel, seq=seq),
        grid=(batch, pairs),
        in_specs=[pl.BlockSpec((seq, kw), lambda b, j: (b, U_GQ // kw + j)),
                  pl.BlockSpec((seq, kw), lambda b, j: (b, U_GK // kw + j)),
                  pl.BlockSpec((seq, vw), lambda b, j: (b, U_GV // vw + j)),
                  pl.BlockSpec((seq, vw), lambda b, j: (b, U_GR // vw + j)),
                  pl.BlockSpec((seq, LANES), lambda b, j: (b, 0)),
                  pl.BlockSpec((LANES, kw), lambda b, j: (0, j)),
                  pl.BlockSpec((1, kw), lambda b, j: (0, j)),
                  pl.BlockSpec((1, vw), lambda b, j: (0, j))],
        out_specs=pl.BlockSpec((seq, vw), lambda b, j: (b, j)),
        out_shape=jax.ShapeDtypeStruct((t, GLA_VAL_W), BF16),
        scratch_shapes=[pltpu.VMEM((seq, kw), F32), pltpu.VMEM((vw, kw), F32)],
        compiler_params=_params(("parallel", "arbitrary")),
        name="gla",
    )(u, u, u, u, small, w_alpha_pad, b_alpha, norm_g)


def _compress_kernel(*refs):
    n = CMP_STRIDE
    xk = refs[0:n]
    xv = refs[n:2 * n]
    pk_ref, pv_ref, w1k_ref, w2k_ref, w1v_ref, w2v_ref, ok_ref, ov_ref = refs[2 * n:]
    d = NSA_DIM

    def one(xs, pos_ref, w1_ref, w2_ref, o_ref):
        nb = xs[0].shape[0]
        top = jnp.zeros((nb, w1_ref.shape[1]), F32)
        bot = jnp.zeros((nb, w1_ref.shape[1]), F32)
        for j in range(n):
            x = xs[j][...].astype(F32)
            xa = (x + pos_ref[j:j + 1, :]).astype(BF16)
            xb = (x + pos_ref[n + j:n + j + 1, :]).astype(BF16)
            top = top + jnp.dot(xa, w1_ref[j * d:(j + 1) * d, :],
                                preferred_element_type=F32)
            bot = bot + jnp.dot(xb, w1_ref[(n + j) * d:(n + j + 1) * d, :],
                                preferred_element_type=F32)
        hid = top + pltpu.roll(bot, nb - 1, axis=0)
        hid = jnp.maximum(hid, 0.0).astype(BF16)
        o_ref[0, 0] = jnp.dot(hid, w2_ref[...],
                              preferred_element_type=F32).astype(o_ref.dtype)

    one(xk, pk_ref, w1k_ref, w2k_ref, ok_ref)
    one(xv, pv_ref, w1v_ref, w2v_ref, ov_ref)


def _compress(u, pos_k, pos_v, w1k, w2k, w1v, w2v, batch, seq, u_kc):
    t, nu = u.shape
    n = CMP_STRIDE
    nb = seq // n
    u16 = u.reshape(t // n, n * nu)
    cb = nu // NSA_DIM
    kc_blk = u_kc // NSA_DIM
    vc_blk = kc_blk + NSA_KV_HEADS

    def xspec(j, base):
        return pl.BlockSpec((nb, NSA_DIM), lambda b, g: (b, j * cb + base + g))

    full = lambda a: pl.BlockSpec(a.shape, lambda b, g: (0,) * a.ndim)
    out = jax.ShapeDtypeStruct((batch, NSA_KV_HEADS, nb, NSA_DIM), BF16)
    ospec = pl.BlockSpec((1, 1, nb, NSA_DIM), lambda b, g: (b, g, 0, 0))
    return pl.pallas_call(
        _compress_kernel,
        grid=(batch, NSA_KV_HEADS),
        in_specs=([xspec(j, kc_blk) for j in range(n)]
                  + [xspec(j, vc_blk) for j in range(n)]
                  + [full(pos_k), full(pos_v), full(w1k), full(w2k), full(w1v), full(w2v)]),
        out_specs=[ospec, ospec],
        out_shape=[out, out],
        compiler_params=_params(("parallel", "arbitrary")),
        name="nsa_compress",
    )(*([u16] * (2 * n)), pos_k, pos_v, w1k, w2k, w1v, w2v)


def _cmp_select_kernel(q_ref, kc_ref, vc_ref, pc_ref, mt_ref, o_ref, sel_ref, *,
                       tq, nb, n_s, top_n):
    i = pl.program_id(2)
    d = NSA_DIM
    scale = d ** -0.5
    kc = kc_ref[0, 0]
    vc = vc_ref[0, 0]
    nt = (((1,), (1,)), ((), ()))
    shift = i * (tq // CMP_STRIDE) + nb
    psum = jnp.zeros((tq, nb), F32)
    for r in range(NSA_GROUP):
        q = q_ref[:, r * d:(r + 1) * d]
        s = lax.dot_general(q, kc, nt, preferred_element_type=F32) * scale
        bias = pltpu.roll(pc_ref[r], shift, axis=1)[:, :nb]
        s = s + bias
        m = jnp.max(s, axis=-1, keepdims=True)
        e = jnp.exp(s - m)
        p = e / jnp.sum(e, axis=-1, keepdims=True)
        p = jnp.where(bias > 0.5 * NEG_INF, p, 0.0)
        o_ref[:, r * d:(r + 1) * d] = jnp.dot(
            p.astype(BF16), vc, preferred_element_type=F32).astype(o_ref.dtype)
        psum = psum + p
    imp = lax.dot_general(mt_ref[...], psum, nt, preferred_element_type=F32,
                          precision=HIGHEST)
    blk = lax.broadcasted_iota(jnp.int32, (n_s, tq), 0)
    pos = i * tq + lax.broadcasted_iota(jnp.int32, (n_s, tq), 1)
    cur = lax.shift_right_logical(pos, int(math.log2(SEL_BLOCK)))
    forced = jnp.where(blk == 0, 1, jnp.where(blk <= cur, jnp.where(blk > cur - SEL_LOCAL, 1, 0), 0))
    score = jnp.where(forced == 1, FORCE_SCORE, jnp.where(blk <= cur, imp, -FORCE_SCORE))
    rank = jnp.zeros((n_s, tq), jnp.int32)
    for j in range(n_s):
        rj = score[j:j + 1, :]
        tie = jnp.where(rj == score, jnp.where(blk > j, 1, 0), 0)
        rank = rank + jnp.where(rj > score, 1, tie)
    selb = jnp.where(rank < top_n, 0.0, NEG_INF)
    if n_s < LANES:
        selb = jnp.concatenate([selb, jnp.zeros((LANES - n_s, tq), F32)], axis=0)
    sel_ref[0, 0] = selb.T.astype(sel_ref.dtype)


def _cmp_select(u, kc, vc, pc, mt, batch, seq, tq=ATT_TILE):
    t = batch * seq
    nb = seq // CMP_STRIDE
    n_s = seq // SEL_BLOCK
    top_n = min(SEL_TOPK, n_s)
    nq = seq // tq
    qw = NSA_GROUP * NSA_DIM
    kv_spec = pl.BlockSpec((1, 1, nb, NSA_DIM), lambda b, g, i: (b, g, 0, 0))
    return pl.pallas_call(
        functools.partial(_cmp_select_kernel, tq=tq, nb=nb, n_s=n_s, top_n=top_n),
        grid=(batch, NSA_KV_HEADS, nq),
        in_specs=[pl.BlockSpec((tq, qw), lambda b, g, i: (b * nq + i, U_NQ // qw + g)),
                  kv_spec, kv_spec,
                  pl.BlockSpec((NSA_GROUP, tq, 2 * nb), lambda b, g, i: (g, 0, 0)),
                  pl.BlockSpec((n_s, nb), lambda b, g, i: (0, 0))],
        out_specs=[pl.BlockSpec((tq, qw), lambda b, g, i: (b * nq + i, g)),
                   pl.BlockSpec((1, 1, tq, LANES), lambda b, g, i: (b, g, i, 0))],
        out_shape=[jax.ShapeDtypeStruct((t, NSA_Q_W), BF16),
                   jax.ShapeDtypeStruct((batch, NSA_KV_HEADS, seq, LANES), BF16)],
        compiler_params=_params(("parallel", "parallel", "arbitrary")),
        name="nsa_cmp_select",
    )(u, kc, vc, pc, mt)


def _flash_kernel(*refs, mode, tq, tk):
    if mode == "sel":
        (q_ref, k_ref, v_ref, sel_ref, p0_ref, p1_ref, o_ref,
         qs_ref, ks_ref, m_ref, l_ref, acc_ref) = refs
    else:
        (q_ref, k_ref, v_ref, p0_ref, p1_ref, p2_ref, o_ref,
         qs_ref, m_ref, l_ref, acc_ref) = refs
    i = pl.program_id(2)
    d = NSA_DIM
    scale = d ** -0.5
    rows = NSA_GROUP * tq
    nt = (((1,), (1,)), ((), ()))

    if mode == "sel":
        @pl.when(i == 0)
        def _():
            seq = k_ref.shape[0]
            ks_ref[:, :d] = k_ref[...]
            krow = lax.broadcasted_iota(jnp.int32, (seq, LANES), 0)
            klane = lax.broadcasted_iota(jnp.int32, (seq, LANES), 1)
            kblk = lax.shift_right_logical(krow, int(math.log2(SEL_BLOCK)))
            ks_ref[:, d:] = jnp.where(kblk == klane, 1.0, 0.0).astype(ks_ref.dtype)

    for r in range(NSA_GROUP):
        qs_ref[r * tq:(r + 1) * tq, :d] = q_ref[:, r * d:(r + 1) * d]
        if mode == "sel":
            qs_ref[r * tq:(r + 1) * tq, d:] = sel_ref[0, 0]
    m_ref[...] = jnp.full_like(m_ref, NEG_INF)
    l_ref[...] = jnp.zeros_like(l_ref)
    acc_ref[...] = jnp.zeros_like(acc_ref)

    def tile_step(j, bias):
        k0 = pl.multiple_of(j * tk, tk)
        if mode == "sel":
            kt = ks_ref[pl.ds(k0, tk), :]
        else:
            kt = k_ref[pl.ds(k0, tk), :]
        vt = v_ref[pl.ds(k0, tk), :]
        s = lax.dot_general(qs_ref[...], kt, nt, preferred_element_type=F32) * scale
        if bias is not None:
            s = s + bias
        m_prev = m_ref[...]
        m_new = jnp.maximum(m_prev, jnp.max(s, axis=-1, keepdims=True))
        alpha = jnp.exp(m_prev - m_new)
        p = jnp.exp(s - jnp.concatenate([m_new] * (tk // LANES), axis=1))
        l_ref[...] = alpha * l_ref[...] + jnp.sum(p, axis=-1, keepdims=True)
        acc_ref[...] = alpha * acc_ref[...] + jnp.dot(
            p.astype(BF16), vt, preferred_element_type=F32)
        m_ref[...] = m_new

    if mode == "sel":
        def far(j, carry):
            tile_step(j, None)
            return carry
        lax.fori_loop(0, i - 1, far, 0)
    else:
        @pl.when(i >= 2)
        def _():
            tile_step(i - 2, p2_ref[0])

    @pl.when(i >= 1)
    def _():
        tile_step(i - 1, p1_ref[0])

    tile_step(i, p0_ref[0])

    inv = 1.0 / l_ref[...]
    out = acc_ref[...] * inv
    for r in range(NSA_GROUP):
        o_ref[:, r * d:(r + 1) * d] = out[r * tq:(r + 1) * tq, :].astype(o_ref.dtype)


def _flash(u, k_blk, v_blk, biases, batch, seq, mode, sel=None, tq=ATT_TILE, tk=ATT_TILE):
    t = batch * seq
    nq = seq // tq
    qw = NSA_GROUP * NSA_DIM
    rows = NSA_GROUP * tq
    d = NSA_DIM
    q_spec = pl.BlockSpec((tq, qw), lambda b, g, i: (b * nq + i, U_NQ // qw + g))
    k_spec = pl.BlockSpec((seq, d), lambda b, g, i: (b, k_blk + g))
    v_spec = pl.BlockSpec((seq, d), lambda b, g, i: (b, v_blk + g))
    b_spec = pl.BlockSpec((1, rows, tk), lambda b, g, i: (g, 0, 0))
    stats = [pltpu.VMEM((rows, LANES), F32)] * 3
    if mode == "sel":
        in_specs = [q_spec, k_spec, v_spec,
                    pl.BlockSpec((1, 1, tq, LANES), lambda b, g, i: (b, g, i, 0)),
                    b_spec, b_spec]
        args = (u, u, u, sel) + tuple(biases)
        scratch = [pltpu.VMEM((rows, 2 * d), BF16), pltpu.VMEM((seq, 2 * d), BF16)] + stats
    else:
        in_specs = [q_spec, k_spec, v_spec, b_spec, b_spec, b_spec]
        args = (u, u, u) + tuple(biases)
        scratch = [pltpu.VMEM((rows, d), BF16)] + stats
    return pl.pallas_call(
        functools.partial(_flash_kernel, mode=mode, tq=tq, tk=tk),
        grid=(batch, NSA_KV_HEADS, nq),
        in_specs=in_specs,
        out_specs=pl.BlockSpec((tq, qw), lambda b, g, i: (b * nq + i, g)),
        out_shape=jax.ShapeDtypeStruct((t, NSA_Q_W), BF16),
        scratch_shapes=scratch,
        compiler_params=_params(("parallel", "parallel", "arbitrary")),
        name="nsa_flash_" + mode,
    )(*args)


def _combine_kernel(oc_ref, os_ref, ow_ref, ng_ref, o_ref):
    gates = jax.nn.sigmoid(ng_ref[...])
    d = NSA_DIM
    for h in range(NSA_HEADS):
        c0 = GLA_GATE_RANK + 3 * h
        sl = slice(h * d, (h + 1) * d)
        o = (gates[:, c0:c0 + 1] * oc_ref[:, sl].astype(F32)
             + gates[:, c0 + 1:c0 + 2] * os_ref[:, sl].astype(F32)
             + gates[:, c0 + 2:c0 + 3] * ow_ref[:, sl].astype(F32))
        o_ref[:, sl] = o.astype(o_ref.dtype)


def _combine(o_cmp, o_sel, o_win, small, tm=512):
    t, w = o_cmp.shape
    spec = pl.BlockSpec((tm, w), lambda i: (i, 0))
    return pl.pallas_call(
        _combine_kernel,
        grid=(t // tm,),
        in_specs=[spec, spec, spec, pl.BlockSpec((tm, LANES), lambda i: (i, 0))],
        out_specs=spec,
        out_shape=jax.ShapeDtypeStruct((t, w), BF16),
        compiler_params=_params(("parallel",)),
        name="nsa_combine",
    )(o_cmp, o_sel, o_win, small)


def _rel_bucket(dist):
    n = jnp.maximum(dist, 0)
    max_exact = REL_BUCKETS // 2
    nf = jnp.maximum(n, max_exact).astype(F32)
    large = max_exact + (jnp.log(nf / max_exact) / math.log(REL_MAX_DIST / max_exact)
                         * (REL_BUCKETS - max_exact)).astype(jnp.int32)
    large = jnp.minimum(large, REL_BUCKETS - 1)
    return jnp.where(n < max_exact, n, large)


def _bias_tables(rel_table, seq, tq, tk):
    table = rel_table.astype(F32)
    far = table[REL_BUCKETS - 1]
    a = np.arange(tq)[:, None]
    b = np.arange(tk)[None, :]

    def tile(dist_np, valid_np):
        dist = jnp.asarray(np.maximum(dist_np, 0), jnp.int32)
        vals = table[_rel_bucket(dist)] - far
        vals = jnp.where(jnp.asarray(valid_np)[..., None], vals, NEG_INF)
        vals = vals.transpose(2, 0, 1)
        return vals.reshape(NSA_KV_HEADS, NSA_GROUP * tq, tk)

    d0 = a - b
    p0 = tile(d0, d0 >= 0)
    d1 = tk + a - b
    p1 = tile(d1, np.ones_like(d1, bool))
    d2 = 2 * tk + a - b
    p2 = tile(d2, d2 < WINDOW)
    nb = seq // CMP_STRIDE
    cp = np.arange(2 * nb)[None, :] - nb
    dc = a - CMP_STRIDE * cp - (CMP_BLOCK - 1)
    distc = jnp.asarray(np.maximum(dc, 0), jnp.int32)
    pc = table[_rel_bucket(distc)]
    pc = jnp.where(jnp.asarray(dc >= 0)[..., None], pc, NEG_INF).transpose(2, 0, 1)
    return p0, p1, p2, pc


def _cmp_to_sel_matrix_t(n_c_pad, n_s):
    n_c = n_c_pad - 1
    m_mat = np.zeros((n_s, n_c_pad), np.float32)
    j = np.arange(n_s)
    for m in range(SEL_BLOCK // CMP_STRIDE):
        for n in range(CMP_BLOCK // CMP_STRIDE):
            c = (SEL_BLOCK // CMP_STRIDE) * j + m - n
            ok = (c >= 0) & (c < n_c)
            np.add.at(m_mat, (j[ok], c[ok]), 1.0)
    return jnp.asarray(m_mat)


def _nsa(u, small, pos_k, pos_v, w1k, w2k, w1v, w2v, rel_table, batch, seq, u_kc):
    kc_blk = u_kc // NSA_DIM
    ksl_blk = kc_blk + 2 * NSA_KV_HEADS
    vsl_blk = kc_blk + 3 * NSA_KV_HEADS
    kw_blk = kc_blk + 4 * NSA_KV_HEADS
    vw_blk = kc_blk + 5 * NSA_KV_HEADS
    p0, p1, p2, pc = _bias_tables(rel_table, seq, ATT_TILE, ATT_TILE)
    mt = _cmp_to_sel_matrix_t(seq // CMP_STRIDE, seq // SEL_BLOCK)
    kc, vc = _compress(u, pos_k.astype(F32), pos_v.astype(F32),
                       w1k.astype(BF16), w2k.astype(BF16),
                       w1v.astype(BF16), w2v.astype(BF16), batch, seq, u_kc)
    o_cmp, sel = _cmp_select(u, kc, vc, pc, mt, batch, seq)
    o_sel = _flash(u, ksl_blk, vsl_blk, (p0, p1), batch, seq, "sel", sel=sel)
    o_win = _flash(u, kw_blk, vw_blk, (p0, p1, p2), batch, seq, "win")
    return _combine(o_cmp, o_sel, o_win, small)


def kernel(x, g_mix_norm, w_in, w_alpha2, b_alpha, gla_norm_g, cmp_pos_k, cmp_pos_v,
           phi_k_w1, phi_k_w2, phi_v_w1, phi_v_w2, rel_bias_table, w_gla_proj,
           w_nsa_proj, w_out, g_mlp_norm, w_up, w_down, g_final_norm):
    batch, seq, d = x.shape
    t = batch * seq
    depth = w_in.shape[0]
    u_kc = U_MG + 2 * d
    s_ga = 2 * GLA_KEY_W + GLA_VAL_W
    s_gr = s_ga + GLA_GATE_RANK
    s_kc = s_gr + GLA_VAL_W + NSA_Q_W
    s_ng = s_kc + 6 * NSA_KV_W
    s_mg = s_ng + 3 * NSA_HEADS
    xf = x.reshape(t, d)
    for l in range(depth):
        w = w_in[l]
        w_main = jnp.concatenate(
            [w[:, :s_ga], w[:, s_gr:s_kc], w[:, s_mg:], w[:, s_kc:s_ng]], axis=1).astype(BF16)
        pad = LANES - GLA_GATE_RANK - 3 * NSA_HEADS
        w_small = jnp.concatenate(
            [w[:, s_ga:s_gr], w[:, s_ng:s_mg], jnp.zeros((d, pad), w.dtype)], axis=1).astype(BF16)
        wa_pad = jnp.concatenate(
            [w_alpha2[l], jnp.zeros((LANES - GLA_GATE_RANK, GLA_KEY_W), w_alpha2.dtype)],
            axis=0).astype(F32)

        h = _rmsnorm(xf, g_mix_norm[l], BF16)
        u = _matmul(h, w_main, BF16, MM_TILE, MM_TILE, name="in_proj")
        small = _matmul(h, w_small, F32, MM_TILE, LANES, name="in_proj_small")
        o_gla = _gla(u, small, wa_pad, b_alpha[l].reshape(1, -1).astype(F32),
                     gla_norm_g[l].reshape(1, -1).astype(F32), batch, seq)
        o_nsa = _nsa(u, small, cmp_pos_k[l], cmp_pos_v[l], phi_k_w1[l], phi_k_w2[l],
                     phi_v_w1[l], phi_v_w2[l], rel_bias_table, batch, seq, u_kc)
        mix = _mix(o_gla, w_gla_proj[l].astype(BF16), o_nsa, w_nsa_proj[l].astype(BF16), u, d)
        xf = _matmul_residual(mix, w_out[l].astype(BF16), xf, MM_TILE, MM_TILE, 2048, "out_proj")
        h2 = _rmsnorm(xf, g_mlp_norm[l], BF16)
        act = _matmul(h2, w_up[l].astype(BF16), BF16, MM_TILE, MM_TILE, act="relu2", name="mlp_up")
        xf = _matmul_residual(act, w_down[l].astype(BF16), xf, MM_TILE, MM_TILE, 2048, "mlp_down")
    out = _rmsnorm(xf, g_final_norm, F32)
    return out.reshape(batch, seq, d)
```

```python
import functools
import math

import numpy as np
import jax
import jax.numpy as jnp
from jax import lax
from jax.experimental import pallas as pl
from jax.experimental.pallas import tpu as pltpu

F32 = jnp.float32
BF16 = jnp.bfloat16
HIGHEST = lax.Precision.HIGHEST

NORM_EPS = 1e-6
GLA_HEADS = 16
GLA_DK = 64
GLA_DV = 128
GLA_KEY_W = GLA_HEADS * GLA_DK
GLA_VAL_W = GLA_HEADS * GLA_DV
GLA_GATE_RANK = 16
GLA_TAU = 16.0
GLA_CHUNK = 64
NSA_HEADS = 16
NSA_KV_HEADS = 4
NSA_GROUP = NSA_HEADS // NSA_KV_HEADS
NSA_DIM = 128
NSA_Q_W = NSA_HEADS * NSA_DIM
NSA_KV_W = NSA_KV_HEADS * NSA_DIM
CMP_BLOCK = 32
CMP_STRIDE = 16
SEL_BLOCK = 64
SEL_TOPK = 16
SEL_LOCAL = 2
WINDOW = 512
REL_BUCKETS = 32
REL_MAX_DIST = 128
NEG_INF = -1e30
FORCE_SCORE = 1e4
LOG2E = math.log2(math.e)

LANES = 128
VMEM_LIMIT = 56 * 1024 * 1024
ATT_TILE = 256
MM_TILE = 1024
GLA_UNROLL = 8

U_GQ = 0
U_GK = U_GQ + GLA_KEY_W
U_GV = U_GK + GLA_KEY_W
U_GR = U_GV + GLA_VAL_W
U_NQ = U_GR + GLA_VAL_W
U_MG = U_NQ + NSA_Q_W

NT = (((1,), (1,)), ((), ()))
TN = (((0,), (0,)), ((), ()))


def _params(sem):
    return pltpu.CompilerParams(dimension_semantics=sem, vmem_limit_bytes=VMEM_LIMIT)


def _rmsnorm_kernel(x_ref, g_ref, o_ref):
    x = x_ref[...].astype(F32)
    ms = jnp.mean(x * x, axis=-1, keepdims=True)
    o_ref[...] = (x * lax.rsqrt(ms + NORM_EPS) * g_ref[...]).astype(o_ref.dtype)


def _rmsnorm(x2, g, out_dtype, tm=256):
    t, d = x2.shape
    return pl.pallas_call(
        _rmsnorm_kernel,
        grid=(t // tm,),
        in_specs=[pl.BlockSpec((tm, d), lambda i: (i, 0)),
                  pl.BlockSpec((1, d), lambda i: (0, 0))],
        out_specs=pl.BlockSpec((tm, d), lambda i: (i, 0)),
        out_shape=jax.ShapeDtypeStruct((t, d), out_dtype),
        compiler_params=_params(("parallel",)),
        name="rmsnorm",
    )(x2, g.reshape(1, d).astype(F32))


def _mm_kernel(a_ref, b_ref, o_ref, *, act):
    acc = jnp.dot(a_ref[...], b_ref[...], preferred_element_type=F32)
    if act == "relu2":
        r = jnp.maximum(acc, 0.0)
        acc = r * r
    o_ref[...] = acc.astype(o_ref.dtype)


def _matmul(a, b, out_dtype, tm, tn, act=None, name="matmul"):
    m, k = a.shape
    _, n = b.shape
    return pl.pallas_call(
        functools.partial(_mm_kernel, act=act),
        grid=(m // tm, n // tn),
        in_specs=[pl.BlockSpec((tm, k), lambda i, j: (i, 0)),
                  pl.BlockSpec((k, tn), lambda i, j: (0, j))],
        out_specs=pl.BlockSpec((tm, tn), lambda i, j: (i, j)),
        out_shape=jax.ShapeDtypeStruct((m, n), out_dtype),
        compiler_params=_params(("parallel", "arbitrary")),
        name=name,
    )(a, b)


def _mix_kernel(og_ref, wg_ref, on_ref, wn_ref, mg_ref, mn_ref, o_ref):
    yg = jnp.dot(og_ref[...], wg_ref[...], preferred_element_type=F32)
    yn = jnp.dot(on_ref[...], wn_ref[...], preferred_element_type=F32)
    o = (jax.nn.sigmoid(mg_ref[...].astype(F32)) * yg
         + jax.nn.sigmoid(mn_ref[...].astype(F32)) * yn)
    o_ref[...] = o.astype(o_ref.dtype)


def _mix(o_gla, w_g, o_nsa, w_n, u, d, tm=MM_TILE, tn=MM_TILE):
    t = o_gla.shape[0]
    mg_blk = U_MG // tn
    mn_blk = (U_MG + d) // tn
    return pl.pallas_call(
        _mix_kernel,
        grid=(t // tm, d // tn),
        in_specs=[pl.BlockSpec((tm, o_gla.shape[1]), lambda i, j: (i, 0)),
                  pl.BlockSpec((w_g.shape[0], tn), lambda i, j: (0, j)),
                  pl.BlockSpec((tm, o_nsa.shape[1]), lambda i, j: (i, 0)),
                  pl.BlockSpec((w_n.shape[0], tn), lambda i, j: (0, j)),
                  pl.BlockSpec((tm, tn), lambda i, j: (i, mg_blk + j)),
                  pl.BlockSpec((tm, tn), lambda i, j: (i, mn_blk + j))],
        out_specs=pl.BlockSpec((tm, tn), lambda i, j: (i, j)),
        out_shape=jax.ShapeDtypeStruct((t, d), BF16),
        compiler_params=_params(("parallel", "arbitrary")),
        name="mix",
    )(o_gla, w_g, o_nsa, w_n, u, u)


def _mm_res_kernel(a_ref, b_ref, r_ref, o_ref, acc_ref):
    kk = pl.program_id(2)

    @pl.when(kk == 0)
    def _():
        acc_ref[...] = r_ref[...]

    acc_ref[...] += jnp.dot(a_ref[...], b_ref[...], preferred_element_type=F32)

    @pl.when(kk == pl.num_programs(2) - 1)
    def _():
        o_ref[...] = acc_ref[...]


def _matmul_residual(a, b, res, tm, tn, tk, name):
    m, k = a.shape
    _, n = b.shape
    return pl.pallas_call(
        _mm_res_kernel,
        grid=(m // tm, n // tn, k // tk),
        in_specs=[pl.BlockSpec((tm, tk), lambda i, j, kk: (i, kk)),
                  pl.BlockSpec((tk, tn), lambda i, j, kk: (kk, j)),
                  pl.BlockSpec((tm, tn), lambda i, j, kk: (i, j))],
        out_specs=pl.BlockSpec((tm, tn), lambda i, j, kk: (i, j)),
        out_shape=jax.ShapeDtypeStruct((m, n), F32),
        scratch_shapes=[pltpu.VMEM((tm, tn), F32)],
        compiler_params=_params(("parallel", "arbitrary", "arbitrary")),
        name=name,
    )(a, b, res)


def _gla_kernel(q_ref, k_ref, v_ref, r_ref, a_ref, wa_ref, ba_ref, g_ref, o_ref,
                la_ref, st_ref, *, seq):
    c = GLA_CHUNK
    dv = GLA_DV
    z = jnp.dot(a_ref[...], wa_ref[...], preferred_element_type=F32,
                precision=HIGHEST) + ba_ref[...]
    log_sig = jnp.minimum(z, 0.0) - jnp.log1p(jnp.exp(-jnp.abs(z)))
    la_ref[...] = log_sig * (1.0 / GLA_TAU)
    st_ref[...] = jnp.zeros_like(st_ref)

    row = lax.broadcasted_iota(jnp.int32, (c, c), 0)
    col = lax.broadcasted_iota(jnp.int32, (c, c), 1)
    causal = row >= col
    tri = causal.astype(F32)
    lane = lax.broadcasted_iota(jnp.int32, (1, 2 * GLA_DK), 1)
    head_mask = [(lane < GLA_DK).astype(F32), (lane >= GLA_DK).astype(F32)]
    gvec = g_ref[...]

    def body(i, carry):
        r0 = pl.multiple_of(i * c, c)
        la = la_ref[pl.ds(r0, c), :]
        bc = jnp.dot(tri, la, preferred_element_type=F32, precision=HIGHEST)
        last = bc[c - 1:c, :]
        q = q_ref[pl.ds(r0, c), :].astype(F32) * (GLA_DK ** -0.5)
        k = k_ref[pl.ds(r0, c), :].astype(F32)
        qe = q * jnp.exp(bc)
        ke = (k * jnp.exp(-bc)).astype(BF16)
        kd = (k * jnp.exp(last - bc)).astype(BF16)
        v = v_ref[pl.ds(r0, c), :]
        st = st_ref[...]
        stb = st.astype(BF16)
        outs = []
        for h in range(2):
            qh = (qe * head_mask[h]).astype(BF16)
            att = lax.dot_general(qh, ke, NT, preferred_element_type=F32)
            att = jnp.where(causal, att, 0.0).astype(BF16)
            oh = jnp.dot(att, v[:, h * dv:(h + 1) * dv], preferred_element_type=F32)
            oh = oh + lax.dot_general(qh, stb[h * dv:(h + 1) * dv, :], NT,
                                      preferred_element_type=F32)
            ms = jnp.mean(oh * oh, axis=-1, keepdims=True)
            y = oh * lax.rsqrt(ms + NORM_EPS) * gvec[:, h * dv:(h + 1) * dv]
            rr = r_ref[pl.ds(r0, c), h * dv:(h + 1) * dv].astype(F32)
            outs.append(y * (rr * jax.nn.sigmoid(rr)))
        o_ref[pl.ds(r0, c), :] = jnp.concatenate(outs, axis=1).astype(o_ref.dtype)
        kv = lax.dot_general(v, kd, TN, preferred_element_type=F32)
        st_ref[...] = st * jnp.exp(last) + kv
        return carry

    lax.fori_loop(0, seq // c, body, 0, unroll=GLA_UNROLL)


def _gla(u, small, w_alpha_pad, b_alpha, norm_g, batch, seq):
    t = batch * seq
    pairs = GLA_HEADS // 2
    kw = 2 * GLA_DK
    vw = 2 * GLA_DV
    return pl.pallas_call(
        functools.partial(_gla_kernel, seq=seq),
        grid=(batch, pairs),
        in_specs=[pl.BlockSpec((seq, kw), lambda b, j: (b, U_GQ // kw + j)),
                  pl.BlockSpec((seq, kw), lambda b, j: (b, U_GK // kw + j)),
                  pl.BlockSpec((seq, vw), lambda b, j: (b, U_GV // vw + j)),
                  pl.BlockSpec((seq, vw), lambda b, j: (b, U_GR // vw + j)),
                  pl.BlockSpec((seq, LANES), lambda b, j: (b, 0)),
                  pl.BlockSpec((LANES, kw), lambda b, j: (0, j)),
                  pl.BlockSpec((1, kw), lambda b, j: (0, j)),
                  pl.BlockSpec((1, vw), lambda b, j: (0, j))],
        out_specs=pl.BlockSpec((seq, vw), lambda b, j: (b, j)),
        out_shape=jax.ShapeDtypeStruct((t, GLA_VAL_W), BF16),
        scratch_shapes=[pltpu.VMEM((seq, kw), F32), pltpu.VMEM((vw, kw), F32)],
        compiler_params=_params(("parallel", "arbitrary")),
        name="gla",
    )(u, u, u, u, small, w_alpha_pad, b_alpha, norm_g)


def _cmp_cols_per_tile(tq):
    return tq // CMP_STRIDE


def _cmp_front_pad(seq, tq):
    return seq // CMP_STRIDE - _cmp_cols_per_tile(tq)


def _compress_kernel(*refs, front):
    n = CMP_STRIDE
    xk = refs[0:n]
    xv = refs[n:2 * n]
    pk_ref, pv_ref, w1k_ref, w2k_ref, w1v_ref, w2v_ref, ok_ref, ov_ref = refs[2 * n:]
    d = NSA_DIM

    def one(xs, pos_ref, w1_ref, w2_ref, o_ref):
        nb = xs[0].shape[0]
        top = jnp.zeros((nb, w1_ref.shape[1]), F32)
        bot = jnp.zeros((nb, w1_ref.shape[1]), F32)
        for j in range(n):
            x = xs[j][...].astype(F32)
            xa = (x + pos_ref[j:j + 1, :]).astype(BF16)
            xb = (x + pos_ref[n + j:n + j + 1, :]).astype(BF16)
            top = top + jnp.dot(xa, w1_ref[j * d:(j + 1) * d, :],
                                preferred_element_type=F32)
            bot = bot + jnp.dot(xb, w1_ref[(n + j) * d:(n + j + 1) * d, :],
                                preferred_element_type=F32)
        hid = top + pltpu.roll(bot, nb - 1, axis=0)
        hid = jnp.maximum(hid, 0.0).astype(BF16)
        total = o_ref.shape[2]
        o_ref[0, 0, 0:front] = jnp.zeros((front, d), o_ref.dtype)
        o_ref[0, 0, front:front + nb] = jnp.dot(
            hid, w2_ref[...], preferred_element_type=F32).astype(o_ref.dtype)
        o_ref[0, 0, front + nb:total] = jnp.zeros((total - front - nb, d), o_ref.dtype)

    one(xk, pk_ref, w1k_ref, w2k_ref, ok_ref)
    one(xv, pv_ref, w1v_ref, w2v_ref, ov_ref)


def _compress(u, pos_k, pos_v, w1k, w2k, w1v, w2v, batch, seq, u_kc, tq):
    t = u.shape[0]
    n = CMP_STRIDE
    nb = seq // n
    front = _cmp_front_pad(seq, tq)
    slab = u[:, u_kc:u_kc + 2 * NSA_KV_W].reshape(t // n, n * 2 * NSA_KV_W)
    cb = 2 * NSA_KV_W // NSA_DIM

    def xspec(j, base):
        return pl.BlockSpec((nb, NSA_DIM), lambda b, g: (b, j * cb + base + g))

    full = lambda a: pl.BlockSpec(a.shape, lambda b, g: (0,) * a.ndim)
    out = jax.ShapeDtypeStruct((batch, NSA_KV_HEADS, 2 * nb, NSA_DIM), BF16)
    ospec = pl.BlockSpec((1, 1, 2 * nb, NSA_DIM), lambda b, g: (b, g, 0, 0))
    return pl.pallas_call(
        functools.partial(_compress_kernel, front=front),
        grid=(batch, NSA_KV_HEADS),
        in_specs=([xspec(j, 0) for j in range(n)]
                  + [xspec(j, NSA_KV_HEADS) for j in range(n)]
                  + [full(pos_k), full(pos_v), full(w1k), full(w2k), full(w1v), full(w2v)]),
        out_specs=[ospec, ospec],
        out_shape=[out, out],
        compiler_params=_params(("parallel", "arbitrary")),
        name="nsa_compress",
    )(*([slab] * (2 * n)), pos_k, pos_v, w1k, w2k, w1v, w2v)


def _cmp_select_kernel(q_ref, kc_ref, vc_ref, pc_ref, ms_ref, o_ref, sel_ref, *,
                       tq, nb, n_s, top_n, front):
    i = pl.program_id(2)
    d = NSA_DIM
    w0 = pl.multiple_of(i * _cmp_cols_per_tile(tq), _cmp_cols_per_tile(tq))
    kc = kc_ref[0, 0, pl.ds(w0, nb), :]
    vc = vc_ref[0, 0, pl.ds(w0, nb), :]
    col = lax.broadcasted_iota(jnp.int32, (tq, nb), 1)
    before_start = jnp.where(col < front - w0, NEG_INF, 0.0)
    psum = jnp.zeros((tq, nb), F32)
    for r in range(NSA_GROUP):
        q = q_ref[:, r * d:(r + 1) * d]
        s = lax.dot_general(q, kc, NT, preferred_element_type=F32) + before_start
        s = jnp.concatenate([s[:, :nb - LANES], s[:, nb - LANES:] + pc_ref[r]], axis=1)
        m = jnp.max(s, axis=-1, keepdims=True)
        e = jnp.exp2(s - m)
        p = e / jnp.sum(e, axis=-1, keepdims=True)
        p = jnp.where(s > 0.5 * NEG_INF, p, 0.0)
        o_ref[:, r * d:(r + 1) * d] = jnp.dot(
            p.astype(BF16), vc, preferred_element_type=F32).astype(o_ref.dtype)
        psum = psum + p
    imp = lax.dot_general(ms_ref[pl.ds(w0, nb), :], psum, (((0,), (1,)), ((), ())),
                          preferred_element_type=F32, precision=HIGHEST)
    blk = lax.broadcasted_iota(jnp.int32, (n_s, tq), 0)
    pos = i * tq + lax.broadcasted_iota(jnp.int32, (n_s, tq), 1)
    cur = lax.shift_right_logical(pos, int(math.log2(SEL_BLOCK)))
    forced = jnp.where(blk == 0, 1, jnp.where(blk <= cur, jnp.where(blk > cur - SEL_LOCAL, 1, 0), 0))
    score = jnp.where(forced == 1, FORCE_SCORE, jnp.where(blk <= cur, imp, -FORCE_SCORE))
    rank = jnp.zeros((n_s, tq), jnp.int32)
    for j in range(n_s):
        rj = score[j:j + 1, :]
        tie = jnp.where(rj == score, jnp.where(blk > j, 1, 0), 0)
        rank = rank + jnp.where(rj > score, 1, tie)
    selb = jnp.where(rank < top_n, 0.0, NEG_INF)
    if n_s < LANES:
        selb = jnp.concatenate([selb, jnp.zeros((LANES - n_s, tq), F32)], axis=0)
    sel_ref[0, 0] = selb.T.astype(sel_ref.dtype)


def _cmp_select(u, kc, vc, pc, ms, batch, seq, tq):
    t = batch * seq
    nb = seq // CMP_STRIDE
    n_s = seq // SEL_BLOCK
    top_n = min(SEL_TOPK, n_s)
    nq = seq // tq
    qw = NSA_GROUP * NSA_DIM
    kv_spec = pl.BlockSpec((1, 1, 2 * nb, NSA_DIM), lambda b, g, i: (b, g, 0, 0))
    return pl.pallas_call(
        functools.partial(_cmp_select_kernel, tq=tq, nb=nb, n_s=n_s, top_n=top_n,
                          front=_cmp_front_pad(seq, tq)),
        grid=(batch, NSA_KV_HEADS, nq),
        in_specs=[pl.BlockSpec((tq, qw), lambda b, g, i: (b * nq + i, U_NQ // qw + g)),
                  kv_spec, kv_spec,
                  pl.BlockSpec((NSA_GROUP, tq, LANES), lambda b, g, i: (g, 0, 0)),
                  pl.BlockSpec((2 * nb, n_s), lambda b, g, i: (0, 0))],
        out_specs=[pl.BlockSpec((tq, qw), lambda b, g, i: (b * nq + i, g)),
                   pl.BlockSpec((1, 1, tq, LANES), lambda b, g, i: (b, g, i, 0))],
        out_shape=[jax.ShapeDtypeStruct((t, NSA_Q_W), BF16),
                   jax.ShapeDtypeStruct((batch, NSA_KV_HEADS, seq, LANES), BF16)],
        compiler_params=_params(("parallel", "parallel", "arbitrary")),
        name="nsa_cmp_select",
    )(u, kc, vc, pc, ms)


def _flash_kernel(*refs, mode, tq, tk):
    if mode == "sel":
        (q_ref, k_ref, v_ref, sel_ref, p0_ref, p1_ref, o_ref,
         qs_ref, ks_ref, m_ref, l_ref, acc_ref) = refs
    else:
        (q_ref, k_ref, v_ref, p0_ref, p1_ref, o_ref,
         qs_ref, m_ref, l_ref, acc_ref) = refs
    i = pl.program_id(2)
    d = NSA_DIM
    rows = NSA_GROUP * tq

    if mode == "sel":
        @pl.when(i == 0)
        def _():
            seq = k_ref.shape[0]
            ks_ref[:, :d] = k_ref[...]
            krow = lax.broadcasted_iota(jnp.int32, (seq, LANES), 0)
            klane = lax.broadcasted_iota(jnp.int32, (seq, LANES), 1)
            kblk = lax.shift_right_logical(krow, int(math.log2(SEL_BLOCK)))
            ks_ref[:, d:] = jnp.where(kblk == klane, 1.0, 0.0).astype(ks_ref.dtype)

    for r in range(NSA_GROUP):
        qs_ref[r * tq:(r + 1) * tq, :d] = q_ref[:, r * d:(r + 1) * d]
        if mode == "sel":
            qs_ref[r * tq:(r + 1) * tq, d:] = sel_ref[0, 0]
    m_ref[...] = jnp.full_like(m_ref, NEG_INF)
    l_ref[...] = jnp.zeros_like(l_ref)
    acc_ref[...] = jnp.zeros_like(acc_ref)

    def tile_step(k0, width, bias):
        if mode == "sel":
            kt = ks_ref[pl.ds(k0, width), :]
        else:
            kt = k_ref[pl.ds(k0, width), :]
        vt = v_ref[pl.ds(k0, width), :]
        s = lax.dot_general(qs_ref[...], kt, NT, preferred_element_type=F32)
        if bias is not None:
            s = s + bias
        m_prev = m_ref[...]
        m_new = jnp.maximum(m_prev, jnp.max(s, axis=-1, keepdims=True))
        alpha = jnp.exp2(m_prev - m_new)
        p = jnp.exp2(s - jnp.concatenate([m_new] * (width // LANES), axis=1))
        l_ref[...] = alpha * l_ref[...] + jnp.sum(p, axis=-1, keepdims=True)
        acc_ref[...] = alpha * acc_ref[...] + jnp.dot(
            p.astype(BF16), vt, preferred_element_type=F32)
        m_ref[...] = m_new

    if mode == "sel":
        n_far = jnp.maximum(i - 1, 0)

        def far(j, carry):
            tile_step(pl.multiple_of(j * (2 * tk), 2 * tk), 2 * tk, None)
            return carry
        lax.fori_loop(0, n_far // 2, far, 0)

        @pl.when(n_far % 2 == 1)
        def _():
            tile_step(pl.multiple_of((n_far - 1) * tk, tk), tk, None)
    else:
        @pl.when(i >= 2)
        def _():
            a = lax.broadcasted_iota(jnp.int32, (rows, tk), 0) & (tq - 1)
            b = lax.broadcasted_iota(jnp.int32, (rows, tk), 1)
            in_window = 2 * tk + a - b < WINDOW
            tile_step(pl.multiple_of((i - 2) * tk, tk), tk,
                      jnp.where(in_window, 0.0, NEG_INF))

    @pl.when(i >= 1)
    def _():
        tile_step(pl.multiple_of((i - 1) * tk, tk), tk, p1_ref[0])

    tile_step(pl.multiple_of(i * tk, tk), tk, p0_ref[0])

    inv = 1.0 / l_ref[...]
    out = acc_ref[...] * inv
    for r in range(NSA_GROUP):
        o_ref[:, r * d:(r + 1) * d] = out[r * tq:(r + 1) * tq, :].astype(o_ref.dtype)


def _flash(u, k_blk, v_blk, biases, batch, seq, mode, sel=None, tq=ATT_TILE, tk=ATT_TILE):
    assert tq == tk and 2 * tk >= REL_MAX_DIST and 3 * tk > WINDOW >= 2 * tk
    t = batch * seq
    nq = seq // tq
    qw = NSA_GROUP * NSA_DIM
    rows = NSA_GROUP * tq
    d = NSA_DIM
    q_spec = pl.BlockSpec((tq, qw), lambda b, g, i: (b * nq + i, U_NQ // qw + g))
    k_spec = pl.BlockSpec((seq, d), lambda b, g, i: (b, k_blk + g))
    v_spec = pl.BlockSpec((seq, d), lambda b, g, i: (b, v_blk + g))
    b_spec = pl.BlockSpec((1, rows, tk), lambda b, g, i: (g, 0, 0))
    stats = [pltpu.VMEM((rows, LANES), F32)] * 3
    if mode == "sel":
        in_specs = [q_spec, k_spec, v_spec,
                    pl.BlockSpec((1, 1, tq, LANES), lambda b, g, i: (b, g, i, 0)),
                    b_spec, b_spec]
        args = (u, u, u, sel) + tuple(biases)
        scratch = [pltpu.VMEM((rows, 2 * d), BF16), pltpu.VMEM((seq, 2 * d), BF16)] + stats
    else:
        in_specs = [q_spec, k_spec, v_spec, b_spec, b_spec]
        args = (u, u, u) + tuple(biases)
        scratch = [pltpu.VMEM((rows, d), BF16)] + stats
    return pl.pallas_call(
        functools.partial(_flash_kernel, mode=mode, tq=tq, tk=tk),
        grid=(batch, NSA_KV_HEADS, nq),
        in_specs=in_specs,
        out_specs=pl.BlockSpec((tq, qw), lambda b, g, i: (b * nq + i, g)),
        out_shape=jax.ShapeDtypeStruct((t, NSA_Q_W), BF16),
        scratch_shapes=scratch,
        compiler_params=_params(("parallel", "parallel", "arbitrary")),
        name="nsa_flash_" + mode,
    )(*args)


def _combine_kernel(oc_ref, os_ref, ow_ref, ng_ref, o_ref):
    gates = jax.nn.sigmoid(ng_ref[...])
    d = NSA_DIM
    for h in range(NSA_HEADS):
        c0 = GLA_GATE_RANK + 3 * h
        sl = slice(h * d, (h + 1) * d)
        o = (gates[:, c0:c0 + 1] * oc_ref[:, sl].astype(F32)
             + gates[:, c0 + 1:c0 + 2] * os_ref[:, sl].astype(F32)
             + gates[:, c0 + 2:c0 + 3] * ow_ref[:, sl].astype(F32))
        o_ref[:, sl] = o.astype(o_ref.dtype)


def _combine(o_cmp, o_sel, o_win, small, tm=512):
    t, w = o_cmp.shape
    spec = pl.BlockSpec((tm, w), lambda i: (i, 0))
    return pl.pallas_call(
        _combine_kernel,
        grid=(t // tm,),
        in_specs=[spec, spec, spec, pl.BlockSpec((tm, LANES), lambda i: (i, 0))],
        out_specs=spec,
        out_shape=jax.ShapeDtypeStruct((t, w), BF16),
        compiler_params=_params(("parallel",)),
        name="nsa_combine",
    )(o_cmp, o_sel, o_win, small)


def _rel_bucket(dist):
    n = jnp.maximum(dist, 0)
    max_exact = REL_BUCKETS // 2
    nf = jnp.maximum(n, max_exact).astype(F32)
    large = max_exact + (jnp.log(nf / max_exact) / math.log(REL_MAX_DIST / max_exact)
                         * (REL_BUCKETS - max_exact)).astype(jnp.int32)
    large = jnp.minimum(large, REL_BUCKETS - 1)
    return jnp.where(n < max_exact, n, large)


def _bias_by_distance(table, dist):
    onehot = (_rel_bucket(jnp.asarray(dist, jnp.int32))[..., None]
              == jnp.arange(REL_BUCKETS, dtype=jnp.int32)).astype(F32)
    return jnp.dot(onehot, table, precision=HIGHEST)


def _bias_tables(rel_table, seq, tq, tk):
    table = rel_table.astype(F32) * LOG2E
    heads = table.shape[1]
    far = table[REL_BUCKETS - 1]
    period = 3 * tk
    vec = _bias_by_distance(table, np.arange(2 * tk)) - far
    vec = jnp.concatenate([vec, jnp.full((tk, heads), NEG_INF, F32)], axis=0).T
    skew = jnp.tile(vec, (1, tk))[:, :tk * (period - 1)].reshape(heads, tk, period - 1)
    tiles = skew[:, :, :2 * tq].transpose(0, 2, 1)
    p0 = tiles[:, :tq].reshape(NSA_KV_HEADS, NSA_GROUP * tq, tk)
    p1 = tiles[:, tq:].reshape(NSA_KV_HEADS, NSA_GROUP * tq, tk)
    front = _cmp_front_pad(seq, tq)
    nb = seq // CMP_STRIDE
    a = np.arange(tq)[:, None]
    rel_blk = np.arange(nb - LANES, nb)[None, :] - front
    dc = a - CMP_STRIDE * rel_blk - (CMP_BLOCK - 1)
    assert (a - CMP_STRIDE * (nb - LANES - 1 - front) - (CMP_BLOCK - 1)).min() >= REL_MAX_DIST
    pc = _bias_by_distance(table, np.maximum(dc, 0)) - far
    pc = jnp.where(jnp.asarray(dc >= 0)[..., None], pc, NEG_INF).transpose(2, 0, 1)
    return p0, p1, pc


def _cmp_to_sel_matrix(seq, tq):
    nb = seq // CMP_STRIDE
    n_c = nb - 1
    n_s = seq // SEL_BLOCK
    front = _cmp_front_pad(seq, tq)
    m_mat = np.zeros((2 * nb, n_s), np.float32)
    j = np.arange(n_s)
    for m in range(SEL_BLOCK // CMP_STRIDE):
        for n in range(CMP_BLOCK // CMP_STRIDE):
            c = (SEL_BLOCK // CMP_STRIDE) * j + m - n
            ok = (c >= 0) & (c < n_c)
            np.add.at(m_mat, (front + c[ok], j[ok]), 1.0)
    return jnp.asarray(m_mat)


def _nsa(u, small, pos_k, pos_v, w1k, w2k, w1v, w2v, rel_table, batch, seq, u_kc):
    tq = ATT_TILE
    kc_blk = u_kc // NSA_DIM
    ksl_blk = kc_blk + 2 * NSA_KV_HEADS
    vsl_blk = kc_blk + 3 * NSA_KV_HEADS
    kw_blk = kc_blk + 4 * NSA_KV_HEADS
    vw_blk = kc_blk + 5 * NSA_KV_HEADS
    p0, p1, pc = _bias_tables(rel_table, seq, tq, tq)
    ms = _cmp_to_sel_matrix(seq, tq)
    kc, vc = _compress(u, pos_k.astype(F32), pos_v.astype(F32),
                       w1k.astype(BF16), w2k.astype(BF16),
                       w1v.astype(BF16), w2v.astype(BF16), batch, seq, u_kc, tq)
    o_cmp, sel = _cmp_select(u, kc, vc, pc, ms, batch, seq, tq)
    o_sel = _flash(u, ksl_blk, vsl_blk, (p0, p1), batch, seq, "sel", sel=sel)
    o_win = _flash(u, kw_blk, vw_blk, (p0, p1), batch, seq, "win")
    return _combine(o_cmp, o_sel, o_win, small)


def kernel(x, g_mix_norm, w_in, w_alpha2, b_alpha, gla_norm_g, cmp_pos_k, cmp_pos_v,
           phi_k_w1, phi_k_w2, phi_v_w1, phi_v_w2, rel_bias_table, w_gla_proj,
           w_nsa_proj, w_out, g_mlp_norm, w_up, w_down, g_final_norm):
    batch, seq, d = x.shape
    t = batch * seq
    depth = w_in.shape[0]
    u_kc = U_MG + 2 * d
    s_ga = 2 * GLA_KEY_W + GLA_VAL_W
    s_gr = s_ga + GLA_GATE_RANK
    s_nq = s_gr + GLA_VAL_W
    s_kc = s_nq + NSA_Q_W
    s_ng = s_kc + 6 * NSA_KV_W
    s_mg = s_ng + 3 * NSA_HEADS
    q_scale = NSA_DIM ** -0.5 * LOG2E
    xf = x.reshape(t, d)
    for l in range(depth):
        w = w_in[l]
        w_main = jnp.concatenate(
            [w[:, :s_ga], w[:, s_gr:s_nq], w[:, s_nq:s_kc] * q_scale, w[:, s_mg:],
             w[:, s_kc:s_ng]], axis=1).astype(BF16)
        pad = LANES - GLA_GATE_RANK - 3 * NSA_HEADS
        w_small = jnp.concatenate(
            [w[:, s_ga:s_gr], w[:, s_ng:s_mg], jnp.zeros((d, pad), w.dtype)], axis=1).astype(BF16)
        wa_pad = jnp.concatenate(
            [w_alpha2[l], jnp.zeros((LANES - GLA_GATE_RANK, GLA_KEY_W), w_alpha2.dtype)],
            axis=0).astype(F32)

        h = _rmsnorm(xf, g_mix_norm[l], BF16)
        u = _matmul(h, w_main, BF16, MM_TILE, MM_TILE, name="in_proj")
        small = _matmul(h, w_small, F32, MM_TILE, LANES, name="in_proj_small")
        o_gla = _gla(u, small, wa_pad, b_alpha[l].reshape(1, -1).astype(F32),
                     gla_norm_g[l].reshape(1, -1).astype(F32), batch, seq)
        o_nsa = _nsa(u, small, cmp_pos_k[l], cmp_pos_v[l], phi_k_w1[l], phi_k_w2[l],
                     phi_v_w1[l], phi_v_w2[l], rel_bias_table, batch, seq, u_kc)
        mix = _mix(o_gla, w_gla_proj[l].astype(BF16), o_nsa, w_nsa_proj[l].astype(BF16), u, d)
        xf = _matmul_residual(mix, w_out[l].astype(BF16), xf, MM_TILE, MM_TILE, 2048, "out_proj")
        h2 = _rmsnorm(xf, g_mlp_norm[l], BF16)
        act = _matmul(h2, w_up[l].astype(BF16), BF16, MM_TILE, MM_TILE, act="relu2", name="mlp_up")
        xf = _matmul_residual(act, w_down[l].astype(BF16), xf, MM_TILE, MM_TILE, 2048, "mlp_down")
    out = _rmsnorm(xf, g_final_norm, F32)
    return out.reshape(batch, seq, d)
```

```python
import functools
import math

import numpy as np
import jax
import jax.numpy as jnp
from jax import lax
from jax.experimental import pallas as pl
from jax.experimental.pallas import tpu as pltpu

F32 = jnp.float32
BF16 = jnp.bfloat16
HIGHEST = lax.Precision.HIGHEST

NORM_EPS = 1e-6
GLA_HEADS = 16
GLA_DK = 64
GLA_DV = 128
GLA_KEY_W = GLA_HEADS * GLA_DK
GLA_VAL_W = GLA_HEADS * GLA_DV
GLA_GATE_RANK = 16
GLA_TAU = 16.0
GLA_CHUNK = 64
NSA_HEADS = 16
NSA_KV_HEADS = 4
NSA_GROUP = NSA_HEADS // NSA_KV_HEADS
NSA_DIM = 128
NSA_Q_W = NSA_HEADS * NSA_DIM
NSA_KV_W = NSA_KV_HEADS * NSA_DIM
CMP_BLOCK = 32
CMP_STRIDE = 16
SEL_BLOCK = 64
SEL_TOPK = 16
SEL_LOCAL = 2
WINDOW = 512
REL_BUCKETS = 32
REL_MAX_DIST = 128
NEG_INF = -1e30
FORCE_SCORE = 1e4
LOG2E = math.log2(math.e)

LANES = 128
VMEM_LIMIT = 56 * 1024 * 1024
VMEM_LIMIT_KTILED = 62 * 1024 * 1024
ATT_TILE = 256
MM_TILE = 1024
GLA_UNROLL = 8

U_GQ = 0
U_GK = U_GQ + GLA_KEY_W
U_GV = U_GK + GLA_KEY_W
U_GR = U_GV + GLA_VAL_W
U_NQ = U_GR + GLA_VAL_W
U_MG = U_NQ + NSA_Q_W

NT = (((1,), (1,)), ((), ()))
TN = (((0,), (0,)), ((), ()))


def _params(sem, vmem_limit=VMEM_LIMIT):
    return pltpu.CompilerParams(dimension_semantics=sem, vmem_limit_bytes=vmem_limit)


def _rmsnorm_kernel(x_ref, g_ref, o_ref):
    x = x_ref[...].astype(F32)
    ms = jnp.mean(x * x, axis=-1, keepdims=True)
    o_ref[...] = (x * lax.rsqrt(ms + NORM_EPS) * g_ref[...]).astype(o_ref.dtype)


def _rmsnorm(x2, g, out_dtype, tm=256):
    t, d = x2.shape
    return pl.pallas_call(
        _rmsnorm_kernel,
        grid=(t // tm,),
        in_specs=[pl.BlockSpec((tm, d), lambda i: (i, 0)),
                  pl.BlockSpec((1, d), lambda i: (0, 0))],
        out_specs=pl.BlockSpec((tm, d), lambda i: (i, 0)),
        out_shape=jax.ShapeDtypeStruct((t, d), out_dtype),
        compiler_params=_params(("parallel",)),
        name="rmsnorm",
    )(x2, g.reshape(1, d).astype(F32))


def _mm_kernel(a_ref, b_ref, o_ref, *, act):
    acc = jnp.dot(a_ref[...], b_ref[...], preferred_element_type=F32)
    if act == "relu2":
        r = jnp.maximum(acc, 0.0)
        acc = r * r
    o_ref[...] = acc.astype(o_ref.dtype)


def _matmul(a, b, out_dtype, tm, tn, act=None, name="matmul"):
    m, k = a.shape
    _, n = b.shape
    return pl.pallas_call(
        functools.partial(_mm_kernel, act=act),
        grid=(m // tm, n // tn),
        in_specs=[pl.BlockSpec((tm, k), lambda i, j: (i, 0)),
                  pl.BlockSpec((k, tn), lambda i, j: (0, j))],
        out_specs=pl.BlockSpec((tm, tn), lambda i, j: (i, j)),
        out_shape=jax.ShapeDtypeStruct((m, n), out_dtype),
        compiler_params=_params(("parallel", "arbitrary")),
        name=name,
    )(a, b)


def _mix_kernel(og_ref, wg_ref, on_ref, wn_ref, mg_ref, mn_ref, o_ref):
    yg = jnp.dot(og_ref[...], wg_ref[...], preferred_element_type=F32)
    yn = jnp.dot(on_ref[...], wn_ref[...], preferred_element_type=F32)
    o = (jax.nn.sigmoid(mg_ref[...].astype(F32)) * yg
         + jax.nn.sigmoid(mn_ref[...].astype(F32)) * yn)
    o_ref[...] = o.astype(o_ref.dtype)


def _mix(o_gla, w_g, o_nsa, w_n, u, d, tm=MM_TILE, tn=MM_TILE):
    t = o_gla.shape[0]
    mg_blk = U_MG // tn
    mn_blk = (U_MG + d) // tn
    return pl.pallas_call(
        _mix_kernel,
        grid=(t // tm, d // tn),
        in_specs=[pl.BlockSpec((tm, o_gla.shape[1]), lambda i, j: (i, 0)),
                  pl.BlockSpec((w_g.shape[0], tn), lambda i, j: (0, j)),
                  pl.BlockSpec((tm, o_nsa.shape[1]), lambda i, j: (i, 0)),
                  pl.BlockSpec((w_n.shape[0], tn), lambda i, j: (0, j)),
                  pl.BlockSpec((tm, tn), lambda i, j: (i, mg_blk + j)),
                  pl.BlockSpec((tm, tn), lambda i, j: (i, mn_blk + j))],
        out_specs=pl.BlockSpec((tm, tn), lambda i, j: (i, j)),
        out_shape=jax.ShapeDtypeStruct((t, d), BF16),
        compiler_params=_params(("parallel", "arbitrary")),
        name="mix",
    )(o_gla, w_g, o_nsa, w_n, u, u)


def _mm_res_kernel(a_ref, b_ref, r_ref, o_ref, acc_ref):
    kk = pl.program_id(2)

    @pl.when(kk == 0)
    def _():
        acc_ref[...] = r_ref[...]

    acc_ref[...] += jnp.dot(a_ref[...], b_ref[...], preferred_element_type=F32)

    @pl.when(kk == pl.num_programs(2) - 1)
    def _():
        o_ref[...] = acc_ref[...]


def _mm_res_fullk_kernel(a_ref, b_ref, r_ref, o_ref):
    o_ref[...] = r_ref[...] + jnp.dot(a_ref[...], b_ref[...], preferred_element_type=F32)


def _matmul_residual(a, b, res, tm, tn, tk, name):
    m, k = a.shape
    _, n = b.shape
    if tk == k:
        return pl.pallas_call(
            _mm_res_fullk_kernel,
            grid=(m // tm, n // tn),
            in_specs=[pl.BlockSpec((tm, k), lambda i, j: (i, 0)),
                      pl.BlockSpec((k, tn), lambda i, j: (0, j)),
                      pl.BlockSpec((tm, tn), lambda i, j: (i, j))],
            out_specs=pl.BlockSpec((tm, tn), lambda i, j: (i, j)),
            out_shape=jax.ShapeDtypeStruct((m, n), F32),
            compiler_params=_params(("parallel", "arbitrary")),
            name=name,
        )(a, b, res)
    return pl.pallas_call(
        _mm_res_kernel,
        grid=(m // tm, n // tn, k // tk),
        in_specs=[pl.BlockSpec((tm, tk), lambda i, j, kk: (i, kk)),
                  pl.BlockSpec((tk, tn), lambda i, j, kk: (kk, j)),
                  pl.BlockSpec((tm, tn), lambda i, j, kk: (i, j))],
        out_specs=pl.BlockSpec((tm, tn), lambda i, j, kk: (i, j)),
        out_shape=jax.ShapeDtypeStruct((m, n), F32),
        scratch_shapes=[pltpu.VMEM((tm, tn), F32)],
        compiler_params=_params(("parallel", "arbitrary", "arbitrary"), VMEM_LIMIT_KTILED),
        name=name,
    )(a, b, res)


def _gla_kernel(q_ref, k_ref, v_ref, r_ref, a_ref, wa_ref, ba_ref, g_ref, o_ref,
                la_ref, st_ref, *, seq):
    c = GLA_CHUNK
    dv = GLA_DV
    z = jnp.dot(a_ref[...], wa_ref[...], preferred_element_type=F32,
                precision=HIGHEST) + ba_ref[...]
    log_sig = jnp.minimum(z, 0.0) - jnp.log1p(jnp.exp(-jnp.abs(z)))
    la_ref[...] = log_sig * (1.0 / GLA_TAU)
    st_ref[...] = jnp.zeros_like(st_ref)

    row = lax.broadcasted_iota(jnp.int32, (c, c), 0)
    col = lax.broadcasted_iota(jnp.int32, (c, c), 1)
    causal = row >= col
    tri = causal.astype(F32)
    lane = lax.broadcasted_iota(jnp.int32, (1, 2 * GLA_DK), 1)
    head_mask = [(lane < GLA_DK).astype(F32), (lane >= GLA_DK).astype(F32)]
    gvec = g_ref[...]

    def body(i, carry):
        r0 = pl.multiple_of(i * c, c)
        la = la_ref[pl.ds(r0, c), :]
        bc = jnp.dot(tri, la, preferred_element_type=F32, precision=HIGHEST)
        last = bc[c - 1:c, :]
        q = q_ref[pl.ds(r0, c), :].astype(F32) * (GLA_DK ** -0.5)
        k = k_ref[pl.ds(r0, c), :].astype(F32)
        qe = q * jnp.exp(bc)
        ke = (k * jnp.exp(-bc)).astype(BF16)
        kd = (k * jnp.exp(last - bc)).astype(BF16)
        v = v_ref[pl.ds(r0, c), :]
        st = st_ref[...]
        stb = st.astype(BF16)
        outs = []
        for h in range(2):
            qh = (qe * head_mask[h]).astype(BF16)
            att = lax.dot_general(qh, ke, NT, preferred_element_type=F32)
            att = jnp.where(causal, att, 0.0).astype(BF16)
            oh = jnp.dot(att, v[:, h * dv:(h + 1) * dv], preferred_element_type=F32)
            oh = oh + lax.dot_general(qh, stb[h * dv:(h + 1) * dv, :], NT,
                                      preferred_element_type=F32)
            ms = jnp.mean(oh * oh, axis=-1, keepdims=True)
            y = oh * lax.rsqrt(ms + NORM_EPS) * gvec[:, h * dv:(h + 1) * dv]
            rr = r_ref[pl.ds(r0, c), h * dv:(h + 1) * dv].astype(F32)
            outs.append(y * (rr * jax.nn.sigmoid(rr)))
        o_ref[pl.ds(r0, c), :] = jnp.concatenate(outs, axis=1).astype(o_ref.dtype)
        kv = lax.dot_general(v, kd, TN, preferred_element_type=F32)
        st_ref[...] = st * jnp.exp(last) + kv
        return carry

    lax.fori_loop(0, seq // c, body, 0, unroll=GLA_UNROLL)


def _gla(u, small, w_alpha_pad, b_alpha, norm_g, batch, seq):
    t = batch * seq
    pairs = GLA_HEADS // 2
    kw = 2 * GLA_DK
    vw = 2 * GLA_DV
    return pl.pallas_call(
        functools.partial(_gla_kernel, seq=seq),
        grid=(batch, pairs),
        in_specs=[pl.BlockSpec((seq, kw), lambda b, j: (b, U_GQ // kw + j)),
                  pl.BlockSpec((seq, kw), lambda b, j: (b, U_GK // kw + j)),
                  pl.BlockSpec((seq, vw), lambda b, j: (b, U_GV // vw + j)),
                  pl.BlockSpec((seq, vw), lambda b, j: (b, U_GR // vw + j)),
                  pl.BlockSpec((seq, LANES), lambda b, j: (b, 0)),
                  pl.BlockSpec((LANES, kw), lambda b, j: (0, j)),
                  pl.BlockSpec((1, kw), lambda b, j: (0, j)),
                  pl.BlockSpec((1, vw), lambda b, j: (0, j))],
        out_specs=pl.BlockSpec((seq, vw), lambda b, j: (b, j)),
        out_shape=jax.ShapeDtypeStruct((t, GLA_VAL_W), BF16),
        scratch_shapes=[pltpu.VMEM((seq, kw), F32), pltpu.VMEM((vw, kw), F32)],
        compiler_params=_params(("parallel", "arbitrary")),
        name="gla",
    )(u, u, u, u, small, w_alpha_pad, b_alpha, norm_g)


def _cmp_cols_per_tile(tq):
    return tq // CMP_STRIDE


def _cmp_front_pad(seq, tq):
    return seq // CMP_STRIDE - _cmp_cols_per_tile(tq)


def _compress_kernel(*refs, front):
    n = CMP_STRIDE
    xk = refs[0:n]
    xv = refs[n:2 * n]
    pk_ref, pv_ref, w1k_ref, w2k_ref, w1v_ref, w2v_ref, ok_ref, ov_ref = refs[2 * n:]
    d = NSA_DIM

    def one(xs, pos_ref, w1_ref, w2_ref, o_ref):
        nb = xs[0].shape[0]
        top = jnp.zeros((nb, w1_ref.shape[1]), F32)
        bot = jnp.zeros((nb, w1_ref.shape[1]), F32)
        for j in range(n):
            x = xs[j][...].astype(F32)
            xa = (x + pos_ref[j:j + 1, :]).astype(BF16)
            xb = (x + pos_ref[n + j:n + j + 1, :]).astype(BF16)
            top = top + jnp.dot(xa, w1_ref[j * d:(j + 1) * d, :],
                                preferred_element_type=F32)
            bot = bot + jnp.dot(xb, w1_ref[(n + j) * d:(n + j + 1) * d, :],
                                preferred_element_type=F32)
        hid = top + pltpu.roll(bot, nb - 1, axis=0)
        hid = jnp.maximum(hid, 0.0).astype(BF16)
        total = o_ref.shape[2]
        o_ref[0, 0, 0:front] = jnp.zeros((front, d), o_ref.dtype)
        o_ref[0, 0, front:front + nb] = jnp.dot(
            hid, w2_ref[...], preferred_element_type=F32).astype(o_ref.dtype)
        o_ref[0, 0, front + nb:total] = jnp.zeros((total - front - nb, d), o_ref.dtype)

    one(xk, pk_ref, w1k_ref, w2k_ref, ok_ref)
    one(xv, pv_ref, w1v_ref, w2v_ref, ov_ref)


def _compress(u, pos_k, pos_v, w1k, w2k, w1v, w2v, batch, seq, u_kc, tq):
    t = u.shape[0]
    n = CMP_STRIDE
    nb = seq // n
    front = _cmp_front_pad(seq, tq)
    slab = u[:, u_kc:u_kc + 2 * NSA_KV_W].reshape(t // n, n * 2 * NSA_KV_W)
    cb = 2 * NSA_KV_W // NSA_DIM

    def xspec(j, base):
        return pl.BlockSpec((nb, NSA_DIM), lambda b, g: (b, j * cb + base + g))

    full = lambda a: pl.BlockSpec(a.shape, lambda b, g: (0,) * a.ndim)
    out = jax.ShapeDtypeStruct((batch, NSA_KV_HEADS, 2 * nb, NSA_DIM), BF16)
    ospec = pl.BlockSpec((1, 1, 2 * nb, NSA_DIM), lambda b, g: (b, g, 0, 0))
    return pl.pallas_call(
        functools.partial(_compress_kernel, front=front),
        grid=(batch, NSA_KV_HEADS),
        in_specs=([xspec(j, 0) for j in range(n)]
                  + [xspec(j, NSA_KV_HEADS) for j in range(n)]
                  + [full(pos_k), full(pos_v), full(w1k), full(w2k), full(w1v), full(w2v)]),
        out_specs=[ospec, ospec],
        out_shape=[out, out],
        compiler_params=_params(("parallel", "arbitrary")),
        name="nsa_compress",
    )(*([slab] * (2 * n)), pos_k, pos_v, w1k, w2k, w1v, w2v)


def _cmp_select_kernel(q_ref, kc_ref, vc_ref, pc_ref, ms_ref, o_ref, sel_ref, *,
                       tq, nb, n_s, top_n, front):
    i = pl.program_id(2)
    d = NSA_DIM
    w0 = pl.multiple_of(i * _cmp_cols_per_tile(tq), _cmp_cols_per_tile(tq))
    kc = kc_ref[0, 0, pl.ds(w0, nb), :]
    vc = vc_ref[0, 0, pl.ds(w0, nb), :]
    col = lax.broadcasted_iota(jnp.int32, (tq, nb), 1)
    before_start = jnp.where(col < front - w0, NEG_INF, 0.0)
    psum = jnp.zeros((tq, nb), F32)
    for r in range(NSA_GROUP):
        q = q_ref[:, r * d:(r + 1) * d]
        s = lax.dot_general(q, kc, NT, preferred_element_type=F32) + before_start
        s = jnp.concatenate([s[:, :nb - LANES], s[:, nb - LANES:] + pc_ref[r]], axis=1)
        m = jnp.max(s, axis=-1, keepdims=True)
        e = jnp.exp2(s - m)
        p = e / jnp.sum(e, axis=-1, keepdims=True)
        p = jnp.where(s > 0.5 * NEG_INF, p, 0.0)
        o_ref[:, r * d:(r + 1) * d] = jnp.dot(
            p.astype(BF16), vc, preferred_element_type=F32).astype(o_ref.dtype)
        psum = psum + p
    imp = lax.dot_general(ms_ref[pl.ds(w0, nb), :], psum, (((0,), (1,)), ((), ())),
                          preferred_element_type=F32, precision=HIGHEST)
    blk = lax.broadcasted_iota(jnp.int32, (n_s, tq), 0)
    pos = i * tq + lax.broadcasted_iota(jnp.int32, (n_s, tq), 1)
    cur = lax.shift_right_logical(pos, int(math.log2(SEL_BLOCK)))
    forced = jnp.where(blk == 0, 1, jnp.where(blk <= cur, jnp.where(blk > cur - SEL_LOCAL, 1, 0), 0))
    score = jnp.where(forced == 1, FORCE_SCORE, jnp.where(blk <= cur, imp, -FORCE_SCORE))
    rank = jnp.zeros((n_s, tq), jnp.int32)
    for j in range(n_s):
        rj = score[j:j + 1, :]
        tie = jnp.where(rj == score, jnp.where(blk > j, 1, 0), 0)
        rank = rank + jnp.where(rj > score, 1, tie)
    selb = jnp.where(rank < top_n, 0.0, NEG_INF)
    if n_s < LANES:
        selb = jnp.concatenate([selb, jnp.zeros((LANES - n_s, tq), F32)], axis=0)
    sel_ref[0, 0] = selb.T.astype(sel_ref.dtype)


def _cmp_select(u, kc, vc, pc, ms, batch, seq, tq):
    t = batch * seq
    nb = seq // CMP_STRIDE
    n_s = seq // SEL_BLOCK
    top_n = min(SEL_TOPK, n_s)
    nq = seq // tq
    qw = NSA_GROUP * NSA_DIM
    kv_spec = pl.BlockSpec((1, 1, 2 * nb, NSA_DIM), lambda b, g, i: (b, g, 0, 0))
    return pl.pallas_call(
        functools.partial(_cmp_select_kernel, tq=tq, nb=nb, n_s=n_s, top_n=top_n,
                          front=_cmp_front_pad(seq, tq)),
        grid=(batch, NSA_KV_HEADS, nq),
        in_specs=[pl.BlockSpec((tq, qw), lambda b, g, i: (b * nq + i, U_NQ // qw + g)),
                  kv_spec, kv_spec,
                  pl.BlockSpec((NSA_GROUP, tq, LANES), lambda b, g, i: (g, 0, 0)),
                  pl.BlockSpec((2 * nb, n_s), lambda b, g, i: (0, 0))],
        out_specs=[pl.BlockSpec((tq, qw), lambda b, g, i: (b * nq + i, g)),
                   pl.BlockSpec((1, 1, tq, LANES), lambda b, g, i: (b, g, i, 0))],
        out_shape=[jax.ShapeDtypeStruct((t, NSA_Q_W), BF16),
                   jax.ShapeDtypeStruct((batch, NSA_KV_HEADS, seq, LANES), BF16)],
        compiler_params=_params(("parallel", "parallel", "arbitrary")),
        name="nsa_cmp_select",
    )(u, kc, vc, pc, ms)


def _flash_kernel(*refs, mode, tq, tk):
    sel = mode == "sel"
    if sel:
        (q_ref, k_ref, v_ref, sel_ref, pw_ref, o_ref,
         qs_ref, ks_ref, m_ref, l_ref, acc_ref) = refs
    else:
        q_ref, k_ref, v_ref, pw_ref, o_ref = refs
    i = pl.program_id(2)
    d = NSA_DIM

    if sel:
        @pl.when(i == 0)
        def _():
            seq = k_ref.shape[0]
            ks_ref[:, :d] = k_ref[...]
            krow = lax.broadcasted_iota(jnp.int32, (seq, LANES), 0)
            klane = lax.broadcasted_iota(jnp.int32, (seq, LANES), 1)
            kblk = lax.shift_right_logical(krow, int(math.log2(SEL_BLOCK)))
            ks_ref[:, d:] = jnp.where(kblk == klane, 1.0, 0.0).astype(ks_ref.dtype)

        for r in range(NSA_GROUP):
            qs_ref[r * tq:(r + 1) * tq, :d] = q_ref[:, r * d:(r + 1) * d]
            qs_ref[r * tq:(r + 1) * tq, d:] = sel_ref[0, 0]
        m_ref[...] = jnp.full_like(m_ref, NEG_INF)
        l_ref[...] = jnp.zeros_like(l_ref)
        acc_ref[...] = jnp.zeros_like(acc_ref)

    chains = [(0, NSA_GROUP)] if sel else [(r, r + 1) for r in range(NSA_GROUP)]

    def chain_logits(h0, h1, kt):
        q = qs_ref[h0 * tq:h1 * tq, :] if sel else q_ref[:, h0 * d:h1 * d]
        return lax.dot_general(q, kt, NT, preferred_element_type=F32)

    def online_update(rows, s, vt):
        m_prev = m_ref[rows]
        m_new = jnp.maximum(m_prev, jnp.max(s, axis=-1, keepdims=True))
        alpha = jnp.exp2(m_prev - m_new)
        p = jnp.exp2(s - jnp.concatenate([m_new] * (s.shape[1] // LANES), axis=1))
        l_new = alpha * l_ref[rows] + jnp.sum(p, axis=-1, keepdims=True)
        acc_new = alpha * acc_ref[rows] + jnp.dot(p.astype(BF16), vt,
                                                  preferred_element_type=F32)
        return m_new, l_new, acc_new

    def last_tiles(n):
        width = n * tk
        k0 = pl.multiple_of((i + 1 - n) * tk, tk)
        kt = ks_ref[pl.ds(k0, width), :] if sel else k_ref[pl.ds(k0, width), :]
        vt = v_ref[pl.ds(k0, width), :]
        for h0, h1 in chains:
            rows = slice(h0 * tq, h1 * tq)
            s = chain_logits(h0, h1, kt) + pw_ref[0, rows, (3 - n) * tk:]
            if sel:
                _, l, acc = online_update(rows, s, vt)
            else:
                p = jnp.exp2(s - jnp.max(s, axis=-1, keepdims=True))
                l = jnp.sum(p, axis=-1, keepdims=True)
                acc = jnp.dot(p.astype(BF16), vt, preferred_element_type=F32)
            out = acc * (1.0 / l)
            for h in range(h0, h1):
                o_ref[:, h * d:(h + 1) * d] = out[(h - h0) * tq:(h - h0 + 1) * tq, :].astype(
                    o_ref.dtype)

    if sel:
        n_far = jnp.maximum(i - 1, 0)
        odd = n_far % 2

        def far_pair(j):
            k0 = pl.multiple_of(j * (2 * tk), 2 * tk)
            kt = ks_ref[pl.ds(k0, 2 * tk), :]
            vt = v_ref[pl.ds(k0, 2 * tk), :]
            for h0, h1 in chains:
                rows = slice(h0 * tq, h1 * tq)
                m_ref[rows], l_ref[rows], acc_ref[rows] = online_update(
                    rows, chain_logits(h0, h1, kt), vt)

        n_pairs = n_far // 2

        def far(j, carry):
            far_pair(2 * j)
            far_pair(2 * j + 1)
            return carry
        lax.fori_loop(0, n_pairs // 2, far, 0)
        pl.when(n_pairs % 2 == 1)(lambda: far_pair(n_pairs - 1))
        pl.when(i == 0)(lambda: last_tiles(1))
        pl.when(jnp.logical_and(i >= 1, odd == 0))(lambda: last_tiles(2))
        pl.when(odd == 1)(lambda: last_tiles(3))
    else:
        pl.when(i == 0)(lambda: last_tiles(1))
        pl.when(i == 1)(lambda: last_tiles(2))
        pl.when(i >= 2)(lambda: last_tiles(3))


def _flash(u, k_blk, v_blk, pw, batch, seq, mode, sel=None, tq=ATT_TILE, tk=ATT_TILE):
    assert tq == tk and 2 * tk >= REL_MAX_DIST and 3 * tk > WINDOW >= 2 * tk
    t = batch * seq
    nq = seq // tq
    qw = NSA_GROUP * NSA_DIM
    rows = NSA_GROUP * tq
    d = NSA_DIM
    q_spec = pl.BlockSpec((tq, qw), lambda b, g, i: (b * nq + i, U_NQ // qw + g))
    k_spec = pl.BlockSpec((seq, d), lambda b, g, i: (b, k_blk + g))
    v_spec = pl.BlockSpec((seq, d), lambda b, g, i: (b, v_blk + g))
    b_spec = pl.BlockSpec((1, rows, 3 * tk), lambda b, g, i: (g, 0, 0))
    if mode == "sel":
        in_specs = [q_spec, k_spec, v_spec,
                    pl.BlockSpec((1, 1, tq, LANES), lambda b, g, i: (b, g, i, 0)), b_spec]
        args = (u, u, u, sel, pw)
        scratch = ([pltpu.VMEM((rows, 2 * d), BF16), pltpu.VMEM((seq, 2 * d), BF16)]
                   + [pltpu.VMEM((rows, LANES), F32)] * 3)
    else:
        in_specs = [q_spec, k_spec, v_spec, b_spec]
        args = (u, u, u, pw)
        scratch = []
    return pl.pallas_call(
        functools.partial(_flash_kernel, mode=mode, tq=tq, tk=tk),
        grid=(batch, NSA_KV_HEADS, nq),
        in_specs=in_specs,
        out_specs=pl.BlockSpec((tq, qw), lambda b, g, i: (b * nq + i, g)),
        out_shape=jax.ShapeDtypeStruct((t, NSA_Q_W), BF16),
        scratch_shapes=scratch,
        compiler_params=_params(("parallel", "parallel", "arbitrary")),
        name="nsa_flash_" + mode,
    )(*args)


def _combine_kernel(oc_ref, os_ref, ow_ref, ng_ref, o_ref):
    gates = jax.nn.sigmoid(ng_ref[...])
    d = NSA_DIM
    for h in range(NSA_HEADS):
        c0 = GLA_GATE_RANK + 3 * h
        sl = slice(h * d, (h + 1) * d)
        o = (gates[:, c0:c0 + 1] * oc_ref[:, sl].astype(F32)
             + gates[:, c0 + 1:c0 + 2] * os_ref[:, sl].astype(F32)
             + gates[:, c0 + 2:c0 + 3] * ow_ref[:, sl].astype(F32))
        o_ref[:, sl] = o.astype(o_ref.dtype)


def _combine(o_cmp, o_sel, o_win, small, tm=512):
    t, w = o_cmp.shape
    spec = pl.BlockSpec((tm, w), lambda i: (i, 0))
    return pl.pallas_call(
        _combine_kernel,
        grid=(t // tm,),
        in_specs=[spec, spec, spec, pl.BlockSpec((tm, LANES), lambda i: (i, 0))],
        out_specs=spec,
        out_shape=jax.ShapeDtypeStruct((t, w), BF16),
        compiler_params=_params(("parallel",)),
        name="nsa_combine",
    )(o_cmp, o_sel, o_win, small)


def _rel_bucket(dist):
    n = jnp.maximum(dist, 0)
    max_exact = REL_BUCKETS // 2
    nf = jnp.maximum(n, max_exact).astype(F32)
    large = max_exact + (jnp.log(nf / max_exact) / math.log(REL_MAX_DIST / max_exact)
                         * (REL_BUCKETS - max_exact)).astype(jnp.int32)
    large = jnp.minimum(large, REL_BUCKETS - 1)
    return jnp.where(n < max_exact, n, large)


def _bias_by_distance(table, dist):
    onehot = (_rel_bucket(jnp.asarray(dist, jnp.int32))[..., None]
              == jnp.arange(REL_BUCKETS, dtype=jnp.int32)).astype(F32)
    return jnp.dot(onehot, table, precision=HIGHEST)


def _bias_tables(rel_table, seq, tq, tk):
    table = rel_table.astype(F32) * LOG2E
    heads = table.shape[1]
    far = table[REL_BUCKETS - 1]
    period = 3 * tk
    vec = _bias_by_distance(table, np.arange(2 * tk)) - far
    vec = jnp.concatenate([vec, jnp.full((tk, heads), NEG_INF, F32)], axis=0).T
    skew = jnp.tile(vec, (1, tk))[:, :tk * (period - 1)].reshape(heads, tk, period - 1)
    tiles = skew[:, :, :2 * tq].transpose(0, 2, 1)
    p0 = tiles[:, :tq].reshape(NSA_KV_HEADS, NSA_GROUP * tq, tk)
    p1 = tiles[:, tq:].reshape(NSA_KV_HEADS, NSA_GROUP * tq, tk)
    a2 = (np.arange(NSA_GROUP * tq) % tq)[:, None]
    edge = np.where(2 * tk + a2 - np.arange(tk)[None, :] < WINDOW, 0.0, NEG_INF).astype(np.float32)
    edge = jnp.broadcast_to(jnp.asarray(edge), p0.shape)
    pw_sel = jnp.concatenate([jnp.zeros_like(p0), p1, p0], axis=2)
    pw_win = jnp.concatenate([edge, p1, p0], axis=2)
    front = _cmp_front_pad(seq, tq)
    nb = seq // CMP_STRIDE
    a = np.arange(tq)[:, None]
    rel_blk = np.arange(nb - LANES, nb)[None, :] - front
    dc = a - CMP_STRIDE * rel_blk - (CMP_BLOCK - 1)
    assert (a - CMP_STRIDE * (nb - LANES - 1 - front) - (CMP_BLOCK - 1)).min() >= REL_MAX_DIST
    pc = _bias_by_distance(table, np.maximum(dc, 0)) - far
    pc = jnp.where(jnp.asarray(dc >= 0)[..., None], pc, NEG_INF).transpose(2, 0, 1)
    return pw_sel, pw_win, pc


def _cmp_to_sel_matrix(seq, tq):
    nb = seq // CMP_STRIDE
    n_c = nb - 1
    n_s = seq // SEL_BLOCK
    front = _cmp_front_pad(seq, tq)
    m_mat = np.zeros((2 * nb, n_s), np.float32)
    j = np.arange(n_s)
    for m in range(SEL_BLOCK // CMP_STRIDE):
        for n in range(CMP_BLOCK // CMP_STRIDE):
            c = (SEL_BLOCK // CMP_STRIDE) * j + m - n
            ok = (c >= 0) & (c < n_c)
            np.add.at(m_mat, (front + c[ok], j[ok]), 1.0)
    return jnp.asarray(m_mat)


def _nsa(u, small, pos_k, pos_v, w1k, w2k, w1v, w2v, rel_table, batch, seq, u_kc):
    tq = ATT_TILE
    kc_blk = u_kc // NSA_DIM
    ksl_blk = kc_blk + 2 * NSA_KV_HEADS
    vsl_blk = kc_blk + 3 * NSA_KV_HEADS
    kw_blk = kc_blk + 4 * NSA_KV_HEADS
    vw_blk = kc_blk + 5 * NSA_KV_HEADS
    pw_sel, pw_win, pc = _bias_tables(rel_table, seq, tq, tq)
    ms = _cmp_to_sel_matrix(seq, tq)
    kc, vc = _compress(u, pos_k.astype(F32), pos_v.astype(F32),
                       w1k.astype(BF16), w2k.astype(BF16),
                       w1v.astype(BF16), w2v.astype(BF16), batch, seq, u_kc, tq)
    o_cmp, sel = _cmp_select(u, kc, vc, pc, ms, batch, seq, tq)
    o_sel = _flash(u, ksl_blk, vsl_blk, pw_sel, batch, seq, "sel", sel=sel)
    o_win = _flash(u, kw_blk, vw_blk, pw_win, batch, seq, "win")
    return _combine(o_cmp, o_sel, o_win, small)


def kernel(x, g_mix_norm, w_in, w_alpha2, b_alpha, gla_norm_g, cmp_pos_k, cmp_pos_v,
           phi_k_w1, phi_k_w2, phi_v_w1, phi_v_w2, rel_bias_table, w_gla_proj,
           w_nsa_proj, w_out, g_mlp_norm, w_up, w_down, g_final_norm):
    batch, seq, d = x.shape
    t = batch * seq
    depth = w_in.shape[0]
    u_kc = U_MG + 2 * d
    s_ga = 2 * GLA_KEY_W + GLA_VAL_W
    s_gr = s_ga + GLA_GATE_RANK
    s_nq = s_gr + GLA_VAL_W
    s_kc = s_nq + NSA_Q_W
    s_ng = s_kc + 6 * NSA_KV_W
    s_mg = s_ng + 3 * NSA_HEADS
    q_scale = NSA_DIM ** -0.5 * LOG2E
    xf = x.reshape(t, d)
    for l in range(depth):
        w = w_in[l]
        w_main = jnp.concatenate(
            [w[:, :s_ga], w[:, s_gr:s_nq], w[:, s_nq:s_kc] * q_scale, w[:, s_mg:],
             w[:, s_kc:s_ng]], axis=1).astype(BF16)
        pad = LANES - GLA_GATE_RANK - 3 * NSA_HEADS
        w_small = jnp.concatenate(
            [w[:, s_ga:s_gr], w[:, s_ng:s_mg], jnp.zeros((d, pad), w.dtype)], axis=1).astype(BF16)
        wa_pad = jnp.concatenate(
            [w_alpha2[l], jnp.zeros((LANES - GLA_GATE_RANK, GLA_KEY_W), w_alpha2.dtype)],
            axis=0).astype(F32)

        h = _rmsnorm(xf, g_mix_norm[l], BF16)
        u = _matmul(h, w_main, BF16, MM_TILE, MM_TILE, name="in_proj")
        small = _matmul(h, w_small, F32, MM_TILE, LANES, name="in_proj_small")
        o_gla = _gla(u, small, wa_pad, b_alpha[l].reshape(1, -1).astype(F32),
                     gla_norm_g[l].reshape(1, -1).astype(F32), batch, seq)
        o_nsa = _nsa(u, small, cmp_pos_k[l], cmp_pos_v[l], phi_k_w1[l], phi_k_w2[l],
                     phi_v_w1[l], phi_v_w2[l], rel_bias_table, batch, seq, u_kc)
        mix = _mix(o_gla, w_gla_proj[l].astype(BF16), o_nsa, w_nsa_proj[l].astype(BF16), u, d)
        xf = _matmul_residual(mix, w_out[l].astype(BF16), xf, MM_TILE, MM_TILE, d, "out_proj")
        h2 = _rmsnorm(xf, g_mlp_norm[l], BF16)
        act = _matmul(h2, w_up[l].astype(BF16), BF16, MM_TILE, MM_TILE, act="relu2", name="mlp_up")
        xf = _matmul_residual(act, w_down[l].astype(BF16), xf, MM_TILE, MM_TILE, 4096, "mlp_down")
    out = _rmsnorm(xf, g_final_norm, F32)
    return out.reshape(batch, seq, d)
```

```python
import functools
import math

import numpy as np
import jax
import jax.numpy as jnp
from jax import lax
from jax.experimental import pallas as pl
from jax.experimental.pallas import tpu as pltpu

F32 = jnp.float32
BF16 = jnp.bfloat16
HIGHEST = lax.Precision.HIGHEST

NORM_EPS = 1e-6
GLA_HEADS = 16
GLA_DK = 64
GLA_DV = 128
GLA_KEY_W = GLA_HEADS * GLA_DK
GLA_VAL_W = GLA_HEADS * GLA_DV
GLA_GATE_RANK = 16
GLA_TAU = 16.0
GLA_CHUNK = 64
NSA_HEADS = 16
NSA_KV_HEADS = 4
NSA_GROUP = NSA_HEADS // NSA_KV_HEADS
NSA_DIM = 128
NSA_Q_W = NSA_HEADS * NSA_DIM
NSA_KV_W = NSA_KV_HEADS * NSA_DIM
CMP_BLOCK = 32
CMP_STRIDE = 16
SEL_BLOCK = 64
SEL_TOPK = 16
SEL_LOCAL = 2
WINDOW = 512
REL_BUCKETS = 32
REL_MAX_DIST = 128
NEG_INF = -1e30
FORCE_SCORE = 1e4
LOG2E = math.log2(math.e)

LANES = 128
VMEM_LIMIT = 56 * 1024 * 1024
VMEM_LIMIT_KTILED = 62 * 1024 * 1024
ATT_TILE = 256
MM_TILE = 1024
GLA_UNROLL = 8

U_GQ = 0
U_GK = U_GQ + GLA_KEY_W
U_GV = U_GK + GLA_KEY_W
U_GR = U_GV + GLA_VAL_W
U_NQ = U_GR + GLA_VAL_W
U_MG = U_NQ + NSA_Q_W

NT = (((1,), (1,)), ((), ()))
TN = (((0,), (0,)), ((), ()))


def _params(sem, vmem_limit=VMEM_LIMIT):
    return pltpu.CompilerParams(dimension_semantics=sem, vmem_limit_bytes=vmem_limit)


def _rmsnorm_kernel(x_ref, g_ref, o_ref):
    x = x_ref[...].astype(F32)
    ms = jnp.mean(x * x, axis=-1, keepdims=True)
    o_ref[...] = (x * lax.rsqrt(ms + NORM_EPS) * g_ref[...]).astype(o_ref.dtype)


def _rmsnorm(x2, g, out_dtype, tm=256):
    t, d = x2.shape
    return pl.pallas_call(
        _rmsnorm_kernel,
        grid=(t // tm,),
        in_specs=[pl.BlockSpec((tm, d), lambda i: (i, 0)),
                  pl.BlockSpec((1, d), lambda i: (0, 0))],
        out_specs=pl.BlockSpec((tm, d), lambda i: (i, 0)),
        out_shape=jax.ShapeDtypeStruct((t, d), out_dtype),
        compiler_params=_params(("parallel",)),
        name="rmsnorm",
    )(x2, g.reshape(1, d).astype(F32))


W_ROW_ALIGN = 16


def _w_in_prep_kernel(src_ref, w_ref, o_ref, *, q_blocks, q_scale):
    del src_ref
    i = pl.program_id(0)
    is_q = jnp.logical_and(i >= q_blocks[0], i < q_blocks[1])
    scale = jnp.where(is_q, q_scale, 1.0).astype(F32)
    o_ref[...] = (w_ref[0] * scale).astype(o_ref.dtype)


def _w_in_prep(w_in, layer, d, tr=512):
    n = w_in.shape[2]
    s_ga = 2 * GLA_KEY_W + GLA_VAL_W
    s_gr = s_ga + GLA_GATE_RANK
    s_nq = s_gr + GLA_VAL_W
    s_kc = s_nq + NSA_Q_W
    s_ng = s_kc + 6 * NSA_KV_W
    s_mg = s_ng + 3 * NSA_HEADS
    assert n == s_mg + 2 * d
    w_t = jnp.swapaxes(w_in, 1, 2)
    pieces = [(0, s_ga, U_GQ), (s_gr, s_nq, U_GR), (s_nq, s_kc, U_NQ),
              (s_mg, n, U_MG), (s_kc, s_ng, U_MG + 2 * d)]
    src_rows = []
    for s0, s1, d0 in pieces:
        assert s0 % W_ROW_ALIGN == 0 and (s1 - s0) % tr == 0 and d0 == len(src_rows) * tr
        src_rows += [r // W_ROW_ALIGN for r in range(s0, s1, tr)]
    n_main = len(src_rows) * tr
    w_main_t = pl.pallas_call(
        functools.partial(_w_in_prep_kernel,
                          q_blocks=(U_NQ // tr, (U_NQ + NSA_Q_W) // tr),
                          q_scale=NSA_DIM ** -0.5 * LOG2E),
        grid_spec=pltpu.PrefetchScalarGridSpec(
            num_scalar_prefetch=1,
            grid=(n_main // tr,),
            in_specs=[pl.BlockSpec(
                (pl.Element(1), pl.Element(tr), pl.Element(d)),
                lambda i, src: (layer, src[i] * W_ROW_ALIGN, 0))],
            out_specs=pl.BlockSpec((tr, d), lambda i, src: (i, 0))),
        out_shape=jax.ShapeDtypeStruct((n_main, d), BF16),
        compiler_params=_params(("arbitrary",)),
        name="w_in_prep",
    )(jnp.asarray(src_rows, jnp.int32), w_t)
    w_small_t = jnp.concatenate(
        [w_t[layer, s_ga:s_gr], w_t[layer, s_ng:s_mg],
         jnp.zeros((LANES - (s_gr - s_ga) - (s_mg - s_ng), d), w_t.dtype)], axis=0).astype(BF16)
    return w_main_t, w_small_t


def _mm_kernel(a_ref, b_ref, o_ref, *, act, b_transposed):
    if b_transposed:
        acc = lax.dot_general(a_ref[...], b_ref[...], NT, preferred_element_type=F32)
    else:
        acc = jnp.dot(a_ref[...], b_ref[...], preferred_element_type=F32)
    if act == "relu2":
        r = jnp.maximum(acc, 0.0)
        acc = r * r
    o_ref[...] = acc.astype(o_ref.dtype)


def _matmul(a, b, out_dtype, tm, tn, act=None, b_transposed=False, name="matmul"):
    m, k = a.shape
    n = b.shape[0] if b_transposed else b.shape[1]
    b_spec = (pl.BlockSpec((tn, k), lambda i, j: (j, 0)) if b_transposed
              else pl.BlockSpec((k, tn), lambda i, j: (0, j)))
    return pl.pallas_call(
        functools.partial(_mm_kernel, act=act, b_transposed=b_transposed),
        grid=(m // tm, n // tn),
        in_specs=[pl.BlockSpec((tm, k), lambda i, j: (i, 0)), b_spec],
        out_specs=pl.BlockSpec((tm, tn), lambda i, j: (i, j)),
        out_shape=jax.ShapeDtypeStruct((m, n), out_dtype),
        compiler_params=_params(("parallel", "arbitrary")),
        name=name,
    )(a, b)


def _mix_kernel(og_ref, wg_ref, on_ref, wn_ref, mg_ref, mn_ref, o_ref):
    yg = jnp.dot(og_ref[...], wg_ref[...], preferred_element_type=F32)
    yn = jnp.dot(on_ref[...], wn_ref[...], preferred_element_type=F32)
    o = (jax.nn.sigmoid(mg_ref[...].astype(F32)) * yg
         + jax.nn.sigmoid(mn_ref[...].astype(F32)) * yn)
    o_ref[...] = o.astype(o_ref.dtype)


def _mix(o_gla, w_g, o_nsa, w_n, u, d, tm=MM_TILE, tn=MM_TILE):
    t = o_gla.shape[0]
    mg_blk = U_MG // tn
    mn_blk = (U_MG + d) // tn
    return pl.pallas_call(
        _mix_kernel,
        grid=(t // tm, d // tn),
        in_specs=[pl.BlockSpec((tm, o_gla.shape[1]), lambda i, j: (i, 0)),
                  pl.BlockSpec((w_g.shape[0], tn), lambda i, j: (0, j)),
                  pl.BlockSpec((tm, o_nsa.shape[1]), lambda i, j: (i, 0)),
                  pl.BlockSpec((w_n.shape[0], tn), lambda i, j: (0, j)),
                  pl.BlockSpec((tm, tn), lambda i, j: (i, mg_blk + j)),
                  pl.BlockSpec((tm, tn), lambda i, j: (i, mn_blk + j))],
        out_specs=pl.BlockSpec((tm, tn), lambda i, j: (i, j)),
        out_shape=jax.ShapeDtypeStruct((t, d), BF16),
        compiler_params=_params(("parallel", "arbitrary")),
        name="mix",
    )(o_gla, w_g, o_nsa, w_n, u, u)


def _mm_res_kernel(a_ref, b_ref, r_ref, o_ref, acc_ref):
    kk = pl.program_id(2)

    @pl.when(kk == 0)
    def _():
        acc_ref[...] = r_ref[...]

    acc_ref[...] += jnp.dot(a_ref[...], b_ref[...], preferred_element_type=F32)

    @pl.when(kk == pl.num_programs(2) - 1)
    def _():
        o_ref[...] = acc_ref[...]


def _mm_res_fullk_kernel(a_ref, b_ref, r_ref, o_ref):
    o_ref[...] = r_ref[...] + jnp.dot(a_ref[...], b_ref[...], preferred_element_type=F32)


def _matmul_residual(a, b, res, tm, tn, tk, name):
    m, k = a.shape
    _, n = b.shape
    if tk == k:
        return pl.pallas_call(
            _mm_res_fullk_kernel,
            grid=(m // tm, n // tn),
            in_specs=[pl.BlockSpec((tm, k), lambda i, j: (i, 0)),
                      pl.BlockSpec((k, tn), lambda i, j: (0, j)),
                      pl.BlockSpec((tm, tn), lambda i, j: (i, j))],
            out_specs=pl.BlockSpec((tm, tn), lambda i, j: (i, j)),
            out_shape=jax.ShapeDtypeStruct((m, n), F32),
            compiler_params=_params(("parallel", "arbitrary")),
            name=name,
        )(a, b, res)
    return pl.pallas_call(
        _mm_res_kernel,
        grid=(m // tm, n // tn, k // tk),
        in_specs=[pl.BlockSpec((tm, tk), lambda i, j, kk: (i, kk)),
                  pl.BlockSpec((tk, tn), lambda i, j, kk: (kk, j)),
                  pl.BlockSpec((tm, tn), lambda i, j, kk: (i, j))],
        out_specs=pl.BlockSpec((tm, tn), lambda i, j, kk: (i, j)),
        out_shape=jax.ShapeDtypeStruct((m, n), F32),
        scratch_shapes=[pltpu.VMEM((tm, tn), F32)],
        compiler_params=_params(("parallel", "arbitrary", "arbitrary"), VMEM_LIMIT_KTILED),
        name=name,
    )(a, b, res)


def _gla_kernel(q_ref, k_ref, v_ref, r_ref, a_ref, wa_ref, ba_ref, g_ref, o_ref,
                la_ref, st_ref, *, seq):
    c = GLA_CHUNK
    dv = GLA_DV
    z = jnp.dot(a_ref[...], wa_ref[...], preferred_element_type=F32,
                precision=HIGHEST) + ba_ref[...]
    log_sig = jnp.minimum(z, 0.0) - jnp.log1p(jnp.exp(-jnp.abs(z)))
    la_ref[...] = log_sig * (1.0 / GLA_TAU)
    st_ref[...] = jnp.zeros_like(st_ref)

    row = lax.broadcasted_iota(jnp.int32, (c, c), 0)
    col = lax.broadcasted_iota(jnp.int32, (c, c), 1)
    causal = row >= col
    tri = causal.astype(F32)
    lane = lax.broadcasted_iota(jnp.int32, (1, 2 * GLA_DK), 1)
    head_mask = [(lane < GLA_DK).astype(F32), (lane >= GLA_DK).astype(F32)]
    gvec = g_ref[...]

    def body(i, carry):
        r0 = pl.multiple_of(i * c, c)
        la = la_ref[pl.ds(r0, c), :]
        bc = jnp.dot(tri, la, preferred_element_type=F32, precision=HIGHEST)
        last = bc[c - 1:c, :]
        q = q_ref[pl.ds(r0, c), :].astype(F32) * (GLA_DK ** -0.5)
        k = k_ref[pl.ds(r0, c), :].astype(F32)
        qe = q * jnp.exp(bc)
        ke = (k * jnp.exp(-bc)).astype(BF16)
        kd = (k * jnp.exp(last - bc)).astype(BF16)
        v = v_ref[pl.ds(r0, c), :]
        st = st_ref[...]
        stb = st.astype(BF16)
        outs = []
        for h in range(2):
            qh = (qe * head_mask[h]).astype(BF16)
            att = lax.dot_general(qh, ke, NT, preferred_element_type=F32)
            att = jnp.where(causal, att, 0.0).astype(BF16)
            oh = jnp.dot(att, v[:, h * dv:(h + 1) * dv], preferred_element_type=F32)
            oh = oh + lax.dot_general(qh, stb[h * dv:(h + 1) * dv, :], NT,
                                      preferred_element_type=F32)
            ms = jnp.mean(oh * oh, axis=-1, keepdims=True)
            y = oh * lax.rsqrt(ms + NORM_EPS) * gvec[:, h * dv:(h + 1) * dv]
            rr = r_ref[pl.ds(r0, c), h * dv:(h + 1) * dv].astype(F32)
            outs.append(y * (rr * jax.nn.sigmoid(rr)))
        o_ref[pl.ds(r0, c), :] = jnp.concatenate(outs, axis=1).astype(o_ref.dtype)
        kv = lax.dot_general(v, kd, TN, preferred_element_type=F32)
        st_ref[...] = st * jnp.exp(last) + kv
        return carry

    lax.fori_loop(0, seq // c, body, 0, unroll=GLA_UNROLL)


def _gla(u, small, w_alpha_pad, b_alpha, norm_g, batch, seq):
    t = batch * seq
    pairs = GLA_HEADS // 2
    kw = 2 * GLA_DK
    vw = 2 * GLA_DV
    return pl.pallas_call(
        functools.partial(_gla_kernel, seq=seq),
        grid=(batch, pairs),
        in_specs=[pl.BlockSpec((seq, kw), lambda b, j: (b, U_GQ // kw + j)),
                  pl.BlockSpec((seq, kw), lambda b, j: (b, U_GK // kw + j)),
                  pl.BlockSpec((seq, vw), lambda b, j: (b, U_GV // vw + j)),
                  pl.BlockSpec((seq, vw), lambda b, j: (b, U_GR // vw + j)),
                  pl.BlockSpec((seq, LANES), lambda b, j: (b, 0)),
                  pl.BlockSpec((LANES, kw), lambda b, j: (0, j)),
                  pl.BlockSpec((1, kw), lambda b, j: (0, j)),
                  pl.BlockSpec((1, vw), lambda b, j: (0, j))],
        out_specs=pl.BlockSpec((seq, vw), lambda b, j: (b, j)),
        out_shape=jax.ShapeDtypeStruct((t, GLA_VAL_W), BF16),
        scratch_shapes=[pltpu.VMEM((seq, kw), F32), pltpu.VMEM((vw, kw), F32)],
        compiler_params=_params(("parallel", "arbitrary")),
        name="gla",
    )(u, u, u, u, small, w_alpha_pad, b_alpha, norm_g)


def _cmp_cols_per_tile(tq):
    return tq // CMP_STRIDE


def _cmp_front_pad(seq, tq):
    return seq // CMP_STRIDE - _cmp_cols_per_tile(tq)


def _compress_kernel(*refs, front):
    n = CMP_STRIDE
    xk = refs[0:n]
    xv = refs[n:2 * n]
    pk_ref, pv_ref, w1k_ref, w2k_ref, w1v_ref, w2v_ref, ok_ref, ov_ref = refs[2 * n:]
    d = NSA_DIM

    def one(xs, pos_ref, w1_ref, w2_ref, o_ref):
        nb = xs[0].shape[0]
        top = jnp.zeros((nb, w1_ref.shape[1]), F32)
        bot = jnp.zeros((nb, w1_ref.shape[1]), F32)
        for j in range(n):
            x = xs[j][...].astype(F32)
            xa = (x + pos_ref[j:j + 1, :]).astype(BF16)
            xb = (x + pos_ref[n + j:n + j + 1, :]).astype(BF16)
            top = top + jnp.dot(xa, w1_ref[j * d:(j + 1) * d, :],
                                preferred_element_type=F32)
            bot = bot + jnp.dot(xb, w1_ref[(n + j) * d:(n + j + 1) * d, :],
                                preferred_element_type=F32)
        hid = top + pltpu.roll(bot, nb - 1, axis=0)
        hid = jnp.maximum(hid, 0.0).astype(BF16)
        total = o_ref.shape[2]
        o_ref[0, 0, 0:front] = jnp.zeros((front, d), o_ref.dtype)
        o_ref[0, 0, front:front + nb] = jnp.dot(
            hid, w2_ref[...], preferred_element_type=F32).astype(o_ref.dtype)
        o_ref[0, 0, front + nb:total] = jnp.zeros((total - front - nb, d), o_ref.dtype)

    one(xk, pk_ref, w1k_ref, w2k_ref, ok_ref)
    one(xv, pv_ref, w1v_ref, w2v_ref, ov_ref)


def _compress(u, pos_k, pos_v, w1k, w2k, w1v, w2v, batch, seq, u_kc, tq):
    t = u.shape[0]
    n = CMP_STRIDE
    nb = seq // n
    front = _cmp_front_pad(seq, tq)
    slab = u[:, u_kc:u_kc + 2 * NSA_KV_W].reshape(t // n, n * 2 * NSA_KV_W)
    cb = 2 * NSA_KV_W // NSA_DIM

    def xspec(j, base):
        return pl.BlockSpec((nb, NSA_DIM), lambda b, g: (b, j * cb + base + g))

    full = lambda a: pl.BlockSpec(a.shape, lambda b, g: (0,) * a.ndim)
    out = jax.ShapeDtypeStruct((batch, NSA_KV_HEADS, 2 * nb, NSA_DIM), BF16)
    ospec = pl.BlockSpec((1, 1, 2 * nb, NSA_DIM), lambda b, g: (b, g, 0, 0))
    return pl.pallas_call(
        functools.partial(_compress_kernel, front=front),
        grid=(batch, NSA_KV_HEADS),
        in_specs=([xspec(j, 0) for j in range(n)]
                  + [xspec(j, NSA_KV_HEADS) for j in range(n)]
                  + [full(pos_k), full(pos_v), full(w1k), full(w2k), full(w1v), full(w2v)]),
        out_specs=[ospec, ospec],
        out_shape=[out, out],
        compiler_params=_params(("parallel", "arbitrary")),
        name="nsa_compress",
    )(*([slab] * (2 * n)), pos_k, pos_v, w1k, w2k, w1v, w2v)


def _cmp_select_kernel(q_ref, kc_ref, vc_ref, pc_ref, ms_ref, o_ref, sel_ref, *,
                       tq, nb, n_s, top_n, front):
    i = pl.program_id(2)
    d = NSA_DIM
    w0 = pl.multiple_of(i * _cmp_cols_per_tile(tq), _cmp_cols_per_tile(tq))
    kc = kc_ref[0, 0, pl.ds(w0, nb), :]
    vc = vc_ref[0, 0, pl.ds(w0, nb), :]
    col = lax.broadcasted_iota(jnp.int32, (tq, nb), 1)
    before_start = jnp.where(col < front - w0, NEG_INF, 0.0)
    psum = jnp.zeros((tq, nb), F32)
    for r in range(NSA_GROUP):
        q = q_ref[:, r * d:(r + 1) * d]
        s = lax.dot_general(q, kc, NT, preferred_element_type=F32) + before_start
        s = jnp.concatenate([s[:, :nb - LANES], s[:, nb - LANES:] + pc_ref[r]], axis=1)
        m = jnp.max(s, axis=-1, keepdims=True)
        e = jnp.exp2(s - m)
        p = e / jnp.sum(e, axis=-1, keepdims=True)
        p = jnp.where(s > 0.5 * NEG_INF, p, 0.0)
        o_ref[:, r * d:(r + 1) * d] = jnp.dot(
            p.astype(BF16), vc, preferred_element_type=F32).astype(o_ref.dtype)
        psum = psum + p
    imp = lax.dot_general(ms_ref[pl.ds(w0, nb), :], psum, (((0,), (1,)), ((), ())),
                          preferred_element_type=F32, precision=HIGHEST)
    blk = lax.broadcasted_iota(jnp.int32, (n_s, tq), 0)
    pos = i * tq + lax.broadcasted_iota(jnp.int32, (n_s, tq), 1)
    cur = lax.shift_right_logical(pos, int(math.log2(SEL_BLOCK)))
    forced = jnp.where(blk == 0, 1, jnp.where(blk <= cur, jnp.where(blk > cur - SEL_LOCAL, 1, 0), 0))
    score = jnp.where(forced == 1, FORCE_SCORE, jnp.where(blk <= cur, imp, -FORCE_SCORE))
    rank = jnp.zeros((n_s, tq), jnp.int32)
    for j in range(n_s):
        rj = score[j:j + 1, :]
        tie = jnp.where(rj == score, jnp.where(blk > j, 1, 0), 0)
        rank = rank + jnp.where(rj > score, 1, tie)
    selb = jnp.where(rank < top_n, 0.0, NEG_INF)
    if n_s < LANES:
        selb = jnp.concatenate([selb, jnp.zeros((LANES - n_s, tq), F32)], axis=0)
    sel_ref[0, 0] = selb.T.astype(sel_ref.dtype)


def _cmp_select(u, kc, vc, pc, ms, batch, seq, tq):
    t = batch * seq
    nb = seq // CMP_STRIDE
    n_s = seq // SEL_BLOCK
    top_n = min(SEL_TOPK, n_s)
    nq = seq // tq
    qw = NSA_GROUP * NSA_DIM
    kv_spec = pl.BlockSpec((1, 1, 2 * nb, NSA_DIM), lambda b, g, i: (b, g, 0, 0))
    return pl.pallas_call(
        functools.partial(_cmp_select_kernel, tq=tq, nb=nb, n_s=n_s, top_n=top_n,
                          front=_cmp_front_pad(seq, tq)),
        grid=(batch, NSA_KV_HEADS, nq),
        in_specs=[pl.BlockSpec((tq, qw), lambda b, g, i: (b * nq + i, U_NQ // qw + g)),
                  kv_spec, kv_spec,
                  pl.BlockSpec((NSA_GROUP, tq, LANES), lambda b, g, i: (g, 0, 0)),
                  pl.BlockSpec((2 * nb, n_s), lambda b, g, i: (0, 0))],
        out_specs=[pl.BlockSpec((tq, qw), lambda b, g, i: (b * nq + i, g)),
                   pl.BlockSpec((1, 1, tq, LANES), lambda b, g, i: (b, g, i, 0))],
        out_shape=[jax.ShapeDtypeStruct((t, NSA_Q_W), BF16),
                   jax.ShapeDtypeStruct((batch, NSA_KV_HEADS, seq, LANES), BF16)],
        compiler_params=_params(("parallel", "parallel", "arbitrary")),
        name="nsa_cmp_select",
    )(u, kc, vc, pc, ms)


def _flash_kernel(*refs, mode, tq, tk):
    sel = mode == "sel"
    if sel:
        (q_ref, k_ref, v_ref, sel_ref, pw_ref, o_ref,
         qs_ref, ks_ref, m_ref, l_ref, acc_ref) = refs
    else:
        q_ref, k_ref, v_ref, pw_ref, o_ref, vs_ref = refs
    i = pl.program_id(2)
    d = NSA_DIM

    if sel:
        @pl.when(i == 0)
        def _():
            seq = k_ref.shape[0]
            ks_ref[:, :d] = k_ref[...]
            krow = lax.broadcasted_iota(jnp.int32, (seq, LANES), 0)
            klane = lax.broadcasted_iota(jnp.int32, (seq, LANES), 1)
            kblk = lax.shift_right_logical(krow, int(math.log2(SEL_BLOCK)))
            ks_ref[:, d:] = jnp.where(kblk == klane, 1.0, 0.0).astype(ks_ref.dtype)

        for r in range(NSA_GROUP):
            qs_ref[r * tq:(r + 1) * tq, :d] = q_ref[:, r * d:(r + 1) * d]
            qs_ref[r * tq:(r + 1) * tq, d:] = sel_ref[0, 0]
        m_ref[...] = jnp.full_like(m_ref, NEG_INF)
        l_ref[...] = jnp.zeros_like(l_ref)
        acc_ref[...] = jnp.zeros_like(acc_ref)

    if not sel:
        @pl.when(i == 0)
        def _():
            vs_ref[:, :d] = v_ref[...]
            vs_ref[:, d:] = jnp.ones((v_ref.shape[0], d), vs_ref.dtype)

    chains = [(0, NSA_GROUP)] if sel else [(r, r + 1) for r in range(NSA_GROUP)]

    def chain_logits(h0, h1, kt):
        q = qs_ref[h0 * tq:h1 * tq, :] if sel else q_ref[:, h0 * d:h1 * d]
        return lax.dot_general(q, kt, NT, preferred_element_type=F32)

    def online_update(rows, s, vt):
        m_prev = m_ref[rows]
        m_new = jnp.maximum(m_prev, jnp.max(s, axis=-1, keepdims=True))
        alpha = jnp.exp2(m_prev - m_new)
        p = jnp.exp2(s - jnp.concatenate([m_new] * (s.shape[1] // LANES), axis=1))
        l_new = alpha * l_ref[rows] + jnp.sum(p, axis=-1, keepdims=True)
        acc_new = alpha * acc_ref[rows] + jnp.dot(p.astype(BF16), vt,
                                                  preferred_element_type=F32)
        return m_new, l_new, acc_new

    def last_tiles(n):
        width = n * tk
        k0 = pl.multiple_of((i + 1 - n) * tk, tk)
        kt = ks_ref[pl.ds(k0, width), :] if sel else k_ref[pl.ds(k0, width), :]
        vt = v_ref[pl.ds(k0, width), :] if sel else vs_ref[pl.ds(k0, width), :]
        for h0, h1 in chains:
            rows = slice(h0 * tq, h1 * tq)
            s = chain_logits(h0, h1, kt) + pw_ref[0, rows, (3 - n) * tk:]
            if sel:
                _, l, acc = online_update(rows, s, vt)
            else:
                p = jnp.exp2(s - jnp.max(s, axis=-1, keepdims=True))
                acc = jnp.dot(p.astype(BF16), vt, preferred_element_type=F32)
                acc, l = acc[:, :d], acc[:, d:]
            out = acc * (1.0 / l)
            for h in range(h0, h1):
                o_ref[:, h * d:(h + 1) * d] = out[(h - h0) * tq:(h - h0 + 1) * tq, :].astype(
                    o_ref.dtype)

    if sel:
        n_far = jnp.maximum(i - 1, 0)
        odd = n_far % 2

        def far_pair(j):
            k0 = pl.multiple_of(j * (2 * tk), 2 * tk)
            kt = ks_ref[pl.ds(k0, 2 * tk), :]
            vt = v_ref[pl.ds(k0, 2 * tk), :]
            for h0, h1 in chains:
                rows = slice(h0 * tq, h1 * tq)
                m_ref[rows], l_ref[rows], acc_ref[rows] = online_update(
                    rows, chain_logits(h0, h1, kt), vt)

        n_pairs = n_far // 2

        def far(j, carry):
            far_pair(2 * j)
            far_pair(2 * j + 1)
            return carry
        lax.fori_loop(0, n_pairs // 2, far, 0)
        pl.when(n_pairs % 2 == 1)(lambda: far_pair(n_pairs - 1))
        pl.when(i == 0)(lambda: last_tiles(1))
        pl.when(jnp.logical_and(i >= 1, odd == 0))(lambda: last_tiles(2))
        pl.when(odd == 1)(lambda: last_tiles(3))
    else:
        pl.when(i == 0)(lambda: last_tiles(1))
        pl.when(i == 1)(lambda: last_tiles(2))
        pl.when(i >= 2)(lambda: last_tiles(3))


def _flash(u, k_blk, v_blk, pw, batch, seq, mode, sel=None, tq=ATT_TILE, tk=ATT_TILE):
    assert tq == tk and 2 * tk >= REL_MAX_DIST and 3 * tk > WINDOW >= 2 * tk
    t = batch * seq
    nq = seq // tq
    qw = NSA_GROUP * NSA_DIM
    rows = NSA_GROUP * tq
    d = NSA_DIM
    q_spec = pl.BlockSpec((tq, qw), lambda b, g, i: (b * nq + i, U_NQ // qw + g))
    k_spec = pl.BlockSpec((seq, d), lambda b, g, i: (b, k_blk + g))
    v_spec = pl.BlockSpec((seq, d), lambda b, g, i: (b, v_blk + g))
    b_spec = pl.BlockSpec((1, rows, 3 * tk), lambda b, g, i: (g, 0, 0))
    if mode == "sel":
        in_specs = [q_spec, k_spec, v_spec,
                    pl.BlockSpec((1, 1, tq, LANES), lambda b, g, i: (b, g, i, 0)), b_spec]
        args = (u, u, u, sel, pw)
        scratch = ([pltpu.VMEM((rows, 2 * d), BF16), pltpu.VMEM((seq, 2 * d), BF16)]
                   + [pltpu.VMEM((rows, LANES), F32)] * 3)
    else:
        in_specs = [q_spec, k_spec, v_spec, b_spec]
        args = (u, u, u, pw)
        scratch = [pltpu.VMEM((seq, 2 * d), BF16)]
    return pl.pallas_call(
        functools.partial(_flash_kernel, mode=mode, tq=tq, tk=tk),
        grid=(batch, NSA_KV_HEADS, nq),
        in_specs=in_specs,
        out_specs=pl.BlockSpec((tq, qw), lambda b, g, i: (b * nq + i, g)),
        out_shape=jax.ShapeDtypeStruct((t, NSA_Q_W), BF16),
        scratch_shapes=scratch,
        compiler_params=_params(("parallel", "parallel", "arbitrary")),
        name="nsa_flash_" + mode,
    )(*args)


def _combine_kernel(oc_ref, os_ref, ow_ref, ng_ref, o_ref):
    gates = jax.nn.sigmoid(ng_ref[...])
    d = NSA_DIM
    for h in range(NSA_HEADS):
        c0 = GLA_GATE_RANK + 3 * h
        sl = slice(h * d, (h + 1) * d)
        o = (gates[:, c0:c0 + 1] * oc_ref[:, sl].astype(F32)
             + gates[:, c0 + 1:c0 + 2] * os_ref[:, sl].astype(F32)
             + gates[:, c0 + 2:c0 + 3] * ow_ref[:, sl].astype(F32))
        o_ref[:, sl] = o.astype(o_ref.dtype)


def _combine(o_cmp, o_sel, o_win, small, tm=512):
    t, w = o_cmp.shape
    spec = pl.BlockSpec((tm, w), lambda i: (i, 0))
    return pl.pallas_call(
        _combine_kernel,
        grid=(t // tm,),
        in_specs=[spec, spec, spec, pl.BlockSpec((tm, LANES), lambda i: (i, 0))],
        out_specs=spec,
        out_shape=jax.ShapeDtypeStruct((t, w), BF16),
        compiler_params=_params(("parallel",)),
        name="nsa_combine",
    )(o_cmp, o_sel, o_win, small)


def _rel_bucket(dist):
    n = jnp.maximum(dist, 0)
    max_exact = REL_BUCKETS // 2
    nf = jnp.maximum(n, max_exact).astype(F32)
    large = max_exact + (jnp.log(nf / max_exact) / math.log(REL_MAX_DIST / max_exact)
                         * (REL_BUCKETS - max_exact)).astype(jnp.int32)
    large = jnp.minimum(large, REL_BUCKETS - 1)
    return jnp.where(n < max_exact, n, large)


def _bias_by_distance(table, dist):
    onehot = (_rel_bucket(jnp.asarray(dist, jnp.int32))[..., None]
              == jnp.arange(REL_BUCKETS, dtype=jnp.int32)).astype(F32)
    return jnp.dot(onehot, table, precision=HIGHEST)


def _bias_tables(rel_table, seq, tq, tk):
    table = rel_table.astype(F32) * LOG2E
    heads = table.shape[1]
    far = table[REL_BUCKETS - 1]
    period = 3 * tk
    vec = _bias_by_distance(table, np.arange(2 * tk)) - far
    vec = jnp.concatenate([vec, jnp.full((tk, heads), NEG_INF, F32)], axis=0).T
    skew = jnp.tile(vec, (1, tk))[:, :tk * (period - 1)].reshape(heads, tk, period - 1)
    tiles = skew[:, :, :2 * tq].transpose(0, 2, 1)
    p0 = tiles[:, :tq].reshape(NSA_KV_HEADS, NSA_GROUP * tq, tk)
    p1 = tiles[:, tq:].reshape(NSA_KV_HEADS, NSA_GROUP * tq, tk)
    a2 = (np.arange(NSA_GROUP * tq) % tq)[:, None]
    edge = np.where(2 * tk + a2 - np.arange(tk)[None, :] < WINDOW, 0.0, NEG_INF).astype(np.float32)
    edge = jnp.broadcast_to(jnp.asarray(edge), p0.shape)
    pw_sel = jnp.concatenate([jnp.zeros_like(p0), p1, p0], axis=2)
    pw_win = jnp.concatenate([edge, p1, p0], axis=2)
    front = _cmp_front_pad(seq, tq)
    nb = seq // CMP_STRIDE
    a = np.arange(tq)[:, None]
    rel_blk = np.arange(nb - LANES, nb)[None, :] - front
    dc = a - CMP_STRIDE * rel_blk - (CMP_BLOCK - 1)
    assert (a - CMP_STRIDE * (nb - LANES - 1 - front) - (CMP_BLOCK - 1)).min() >= REL_MAX_DIST
    pc = _bias_by_distance(table, np.maximum(dc, 0)) - far
    pc = jnp.where(jnp.asarray(dc >= 0)[..., None], pc, NEG_INF).transpose(2, 0, 1)
    return pw_sel, pw_win, pc


def _cmp_to_sel_matrix(seq, tq):
    nb = seq // CMP_STRIDE
    n_c = nb - 1
    n_s = seq // SEL_BLOCK
    front = _cmp_front_pad(seq, tq)
    m_mat = np.zeros((2 * nb, n_s), np.float32)
    j = np.arange(n_s)
    for m in range(SEL_BLOCK // CMP_STRIDE):
        for n in range(CMP_BLOCK // CMP_STRIDE):
            c = (SEL_BLOCK // CMP_STRIDE) * j + m - n
            ok = (c >= 0) & (c < n_c)
            np.add.at(m_mat, (front + c[ok], j[ok]), 1.0)
    return jnp.asarray(m_mat)


def _nsa(u, small, pos_k, pos_v, w1k, w2k, w1v, w2v, rel_table, batch, seq, u_kc):
    tq = ATT_TILE
    kc_blk = u_kc // NSA_DIM
    ksl_blk = kc_blk + 2 * NSA_KV_HEADS
    vsl_blk = kc_blk + 3 * NSA_KV_HEADS
    kw_blk = kc_blk + 4 * NSA_KV_HEADS
    vw_blk = kc_blk + 5 * NSA_KV_HEADS
    pw_sel, pw_win, pc = _bias_tables(rel_table, seq, tq, tq)
    ms = _cmp_to_sel_matrix(seq, tq)
    kc, vc = _compress(u, pos_k.astype(F32), pos_v.astype(F32),
                       w1k.astype(BF16), w2k.astype(BF16),
                       w1v.astype(BF16), w2v.astype(BF16), batch, seq, u_kc, tq)
    o_cmp, sel = _cmp_select(u, kc, vc, pc, ms, batch, seq, tq)
    o_sel = _flash(u, ksl_blk, vsl_blk, pw_sel, batch, seq, "sel", sel=sel)
    o_win = _flash(u, kw_blk, vw_blk, pw_win, batch, seq, "win")
    return _combine(o_cmp, o_sel, o_win, small)


def kernel(x, g_mix_norm, w_in, w_alpha2, b_alpha, gla_norm_g, cmp_pos_k, cmp_pos_v,
           phi_k_w1, phi_k_w2, phi_v_w1, phi_v_w2, rel_bias_table, w_gla_proj,
           w_nsa_proj, w_out, g_mlp_norm, w_up, w_down, g_final_norm):
    batch, seq, d = x.shape
    t = batch * seq
    depth = w_in.shape[0]
    u_kc = U_MG + 2 * d
    xf = x.reshape(t, d)
    for l in range(depth):
        w_main_t, w_small_t = _w_in_prep(w_in, l, d)
        wa_pad = jnp.concatenate(
            [w_alpha2[l], jnp.zeros((LANES - GLA_GATE_RANK, GLA_KEY_W), w_alpha2.dtype)],
            axis=0).astype(F32)

        h = _rmsnorm(xf, g_mix_norm[l], BF16)
        u = _matmul(h, w_main_t, BF16, MM_TILE, MM_TILE, b_transposed=True, name="in_proj")
        small = _matmul(h, w_small_t, F32, MM_TILE, LANES, b_transposed=True,
                        name="in_proj_small")
        o_gla = _gla(u, small, wa_pad, b_alpha[l].reshape(1, -1).astype(F32),
                     gla_norm_g[l].reshape(1, -1).astype(F32), batch, seq)
        o_nsa = _nsa(u, small, cmp_pos_k[l], cmp_pos_v[l], phi_k_w1[l], phi_k_w2[l],
                     phi_v_w1[l], phi_v_w2[l], rel_bias_table, batch, seq, u_kc)
        mix = _mix(o_gla, w_gla_proj[l].astype(BF16), o_nsa, w_nsa_proj[l].astype(BF16), u, d)
        xf = _matmul_residual(mix, w_out[l].astype(BF16), xf, MM_TILE, MM_TILE, d, "out_proj")
        h2 = _rmsnorm(xf, g_mlp_norm[l], BF16)
        act = _matmul(h2, w_up[l].astype(BF16), BF16, MM_TILE, MM_TILE, act="relu2", name="mlp_up")
        xf = _matmul_residual(act, w_down[l].astype(BF16), xf, MM_TILE, MM_TILE, 4096, "mlp_down")
    out = _rmsnorm(xf, g_final_norm, F32)
    return out.reshape(batch, seq, d)
```

```python
import functools
import math

import numpy as np
import jax
import jax.numpy as jnp
from jax import lax
from jax.experimental import pallas as pl
from jax.experimental.pallas import tpu as pltpu

F32 = jnp.float32
BF16 = jnp.bfloat16
HIGHEST = lax.Precision.HIGHEST

NORM_EPS = 1e-6
GLA_HEADS = 16
GLA_DK = 64
GLA_DV = 128
GLA_KEY_W = GLA_HEADS * GLA_DK
GLA_VAL_W = GLA_HEADS * GLA_DV
GLA_GATE_RANK = 16
GLA_TAU = 16.0
GLA_CHUNK = 64
NSA_HEADS = 16
NSA_KV_HEADS = 4
NSA_GROUP = NSA_HEADS // NSA_KV_HEADS
NSA_DIM = 128
NSA_Q_W = NSA_HEADS * NSA_DIM
NSA_KV_W = NSA_KV_HEADS * NSA_DIM
CMP_BLOCK = 32
CMP_STRIDE = 16
SEL_BLOCK = 64
SEL_TOPK = 16
SEL_LOCAL = 2
WINDOW = 512
REL_BUCKETS = 32
REL_MAX_DIST = 128
NEG_INF = -1e30
FORCE_SCORE = 1e4
LOG2E = math.log2(math.e)

LANES = 128
VMEM_LIMIT = 56 * 1024 * 1024
VMEM_LIMIT_KTILED = 62 * 1024 * 1024
ATT_TILE = 256
MM_TILE = 1024
GLA_PRE_BLOCK = 256
GLA_UNROLL = 8

U_GQ = 0
U_GK = U_GQ + GLA_KEY_W
U_GV = U_GK + GLA_KEY_W
U_GR = U_GV + GLA_VAL_W
U_NQ = U_GR + GLA_VAL_W
U_MG = U_NQ + NSA_Q_W

NT = (((1,), (1,)), ((), ()))
TN = (((0,), (0,)), ((), ()))


def _params(sem, vmem_limit=VMEM_LIMIT):
    return pltpu.CompilerParams(dimension_semantics=sem, vmem_limit_bytes=vmem_limit)


def _rmsnorm_kernel(x_ref, g_ref, o_ref):
    x = x_ref[...].astype(F32)
    ms = jnp.mean(x * x, axis=-1, keepdims=True)
    o_ref[...] = (x * lax.rsqrt(ms + NORM_EPS) * g_ref[...]).astype(o_ref.dtype)


def _rmsnorm(x2, g, out_dtype, tm=256):
    t, d = x2.shape
    return pl.pallas_call(
        _rmsnorm_kernel,
        grid=(t // tm,),
        in_specs=[pl.BlockSpec((tm, d), lambda i: (i, 0)),
                  pl.BlockSpec((1, d), lambda i: (0, 0))],
        out_specs=pl.BlockSpec((tm, d), lambda i: (i, 0)),
        out_shape=jax.ShapeDtypeStruct((t, d), out_dtype),
        compiler_params=_params(("parallel",)),
        name="rmsnorm",
    )(x2, g.reshape(1, d).astype(F32))


W_ROW_ALIGN = 16


def _w_in_prep_kernel(src_ref, w_ref, ga_ref, ng_ref, o_ref, os_ref, *, q_blocks, q_scale):
    del src_ref
    i = pl.program_id(0)
    is_q = jnp.logical_and(i >= q_blocks[0], i < q_blocks[1])
    scale = jnp.where(is_q, q_scale, 1.0).astype(F32)
    o_ref[...] = (w_ref[0] * scale).astype(o_ref.dtype)

    @pl.when(i == 0)
    def _():
        n_ga = ga_ref.shape[1]
        n_ng = ng_ref.shape[1]
        os_ref[0:n_ga] = ga_ref[0].astype(os_ref.dtype)
        os_ref[n_ga:n_ga + n_ng] = ng_ref[0].astype(os_ref.dtype)
        os_ref[n_ga + n_ng:] = jnp.zeros((os_ref.shape[0] - n_ga - n_ng, os_ref.shape[1]),
                                         os_ref.dtype)


def _w_in_prep(w_in, layer, d, tr=512):
    n = w_in.shape[2]
    s_ga = 2 * GLA_KEY_W + GLA_VAL_W
    s_gr = s_ga + GLA_GATE_RANK
    s_nq = s_gr + GLA_VAL_W
    s_kc = s_nq + NSA_Q_W
    s_ng = s_kc + 6 * NSA_KV_W
    s_mg = s_ng + 3 * NSA_HEADS
    assert n == s_mg + 2 * d
    w_t = jnp.swapaxes(w_in, 1, 2)
    pieces = [(0, s_ga, U_GQ), (s_gr, s_nq, U_GR), (s_nq, s_kc, U_NQ),
              (s_mg, n, U_MG), (s_kc, s_ng, U_MG + 2 * d)]
    src_rows = []
    for s0, s1, d0 in pieces:
        assert s0 % W_ROW_ALIGN == 0 and (s1 - s0) % tr == 0 and d0 == len(src_rows) * tr
        src_rows += [r // W_ROW_ALIGN for r in range(s0, s1, tr)]
    n_main = len(src_rows) * tr
    def rows_at(start, count):
        return pl.BlockSpec((pl.Element(1), pl.Element(count), pl.Element(d)),
                            lambda i, src: (layer, start, 0))

    return pl.pallas_call(
        functools.partial(_w_in_prep_kernel,
                          q_blocks=(U_NQ // tr, (U_NQ + NSA_Q_W) // tr),
                          q_scale=NSA_DIM ** -0.5 * LOG2E),
        grid_spec=pltpu.PrefetchScalarGridSpec(
            num_scalar_prefetch=1,
            grid=(n_main // tr,),
            in_specs=[pl.BlockSpec(
                (pl.Element(1), pl.Element(tr), pl.Element(d)),
                lambda i, src: (layer, src[i] * W_ROW_ALIGN, 0)),
                rows_at(s_ga, s_gr - s_ga), rows_at(s_ng, s_mg - s_ng)],
            out_specs=[pl.BlockSpec((tr, d), lambda i, src: (i, 0)),
                       pl.BlockSpec((LANES, d), lambda i, src: (0, 0))]),
        out_shape=[jax.ShapeDtypeStruct((n_main, d), BF16),
                   jax.ShapeDtypeStruct((LANES, d), BF16)],
        compiler_params=_params(("arbitrary",)),
        name="w_in_prep",
    )(jnp.asarray(src_rows, jnp.int32), w_t, w_t, w_t)


def _mm_nt_kernel(a_ref, bt_ref, o_ref):
    o_ref[...] = lax.dot_general(a_ref[...], bt_ref[...], NT,
                                 preferred_element_type=F32).astype(o_ref.dtype)


def _matmul_nt(a, b_t, out_dtype, tm, tn, name):
    m, k = a.shape
    n = b_t.shape[0]
    return pl.pallas_call(
        _mm_nt_kernel,
        grid=(m // tm, n // tn),
        in_specs=[pl.BlockSpec((tm, k), lambda i, j: (i, 0)),
                  pl.BlockSpec((tn, k), lambda i, j: (j, 0))],
        out_specs=pl.BlockSpec((tm, tn), lambda i, j: (i, j)),
        out_shape=jax.ShapeDtypeStruct((m, n), out_dtype),
        compiler_params=_params(("parallel", "arbitrary")),
        name=name,
    )(a, b_t)


def _mix_kernel(og_ref, wg_ref, on_ref, wn_ref, mg_ref, mn_ref, o_ref):
    yg = jnp.dot(og_ref[...], wg_ref[...], preferred_element_type=F32)
    yn = jnp.dot(on_ref[...], wn_ref[...], preferred_element_type=F32)
    o = (jax.nn.sigmoid(mg_ref[...].astype(F32)) * yg
         + jax.nn.sigmoid(mn_ref[...].astype(F32)) * yn)
    o_ref[...] = o.astype(o_ref.dtype)


def _mix(o_gla, w_g, o_nsa, w_n, u, d, tm=MM_TILE, tn=MM_TILE):
    t = o_gla.shape[0]
    mg_blk = U_MG // tn
    mn_blk = (U_MG + d) // tn
    return pl.pallas_call(
        _mix_kernel,
        grid=(t // tm, d // tn),
        in_specs=[pl.BlockSpec((tm, o_gla.shape[1]), lambda i, j: (i, 0)),
                  pl.BlockSpec((w_g.shape[0], tn), lambda i, j: (0, j)),
                  pl.BlockSpec((tm, o_nsa.shape[1]), lambda i, j: (i, 0)),
                  pl.BlockSpec((w_n.shape[0], tn), lambda i, j: (0, j)),
                  pl.BlockSpec((tm, tn), lambda i, j: (i, mg_blk + j)),
                  pl.BlockSpec((tm, tn), lambda i, j: (i, mn_blk + j))],
        out_specs=pl.BlockSpec((tm, tn), lambda i, j: (i, j)),
        out_shape=jax.ShapeDtypeStruct((t, d), BF16),
        compiler_params=_params(("parallel", "arbitrary")),
        name="mix",
    )(o_gla, w_g, o_nsa, w_n, u, u)


def _mm_res_kernel(a_ref, b_ref, r_ref, o_ref, acc_ref):
    kk = pl.program_id(2)

    @pl.when(kk == 0)
    def _():
        acc_ref[...] = r_ref[...]

    acc_ref[...] += jnp.dot(a_ref[...], b_ref[...], preferred_element_type=F32)

    @pl.when(kk == pl.num_programs(2) - 1)
    def _():
        o_ref[...] = acc_ref[...]


def _out_proj_kernel(a_ref, b_ref, r_ref, o_ref, ob_ref, ss_ref):
    y = r_ref[...] + jnp.dot(a_ref[...], b_ref[...], preferred_element_type=F32)
    o_ref[...] = y
    ob_ref[...] = y.astype(ob_ref.dtype)
    ss_ref[...] = jnp.broadcast_to(jnp.sum(y * y, axis=-1, keepdims=True), ss_ref.shape)


def _out_proj(a, b, res, tm=MM_TILE, tn=MM_TILE):
    m, k = a.shape
    _, n = b.shape
    return pl.pallas_call(
        _out_proj_kernel,
        grid=(m // tm, n // tn),
        in_specs=[pl.BlockSpec((tm, k), lambda i, j: (i, 0)),
                  pl.BlockSpec((k, tn), lambda i, j: (0, j)),
                  pl.BlockSpec((tm, tn), lambda i, j: (i, j))],
        out_specs=[pl.BlockSpec((tm, tn), lambda i, j: (i, j)),
                   pl.BlockSpec((tm, tn), lambda i, j: (i, j)),
                   pl.BlockSpec((tm, LANES), lambda i, j: (i, j))],
        out_shape=[jax.ShapeDtypeStruct((m, n), F32), jax.ShapeDtypeStruct((m, n), BF16),
                   jax.ShapeDtypeStruct((m, n // tn * LANES), F32)],
        compiler_params=_params(("parallel", "arbitrary"), VMEM_LIMIT_KTILED),
        name="out_proj",
    )(a, b, res)


def _mlp_up_kernel(a_ref, b_ref, ss_ref, o_ref, *, d):
    acc = jnp.dot(a_ref[...], b_ref[...], preferred_element_type=F32)
    ss = ss_ref[...]
    total = ss[:, :LANES]
    for p in range(1, ss.shape[1] // LANES):
        total = total + ss[:, p * LANES:(p + 1) * LANES]
    inv = 1.0 / (total[:, :1] * (1.0 / d) + NORM_EPS)
    r = jnp.maximum(acc, 0.0)
    o_ref[...] = (r * r * inv).astype(o_ref.dtype)


def _mlp_up(xb, w_scaled, ssq, tm=MM_TILE, tn=MM_TILE):
    m, k = xb.shape
    _, n = w_scaled.shape
    return pl.pallas_call(
        functools.partial(_mlp_up_kernel, d=k),
        grid=(m // tm, n // tn),
        in_specs=[pl.BlockSpec((tm, k), lambda i, j: (i, 0)),
                  pl.BlockSpec((k, tn), lambda i, j: (0, j)),
                  pl.BlockSpec((tm, ssq.shape[1]), lambda i, j: (i, 0))],
        out_specs=pl.BlockSpec((tm, tn), lambda i, j: (i, j)),
        out_shape=jax.ShapeDtypeStruct((m, n), BF16),
        compiler_params=_params(("parallel", "arbitrary")),
        name="mlp_up",
    )(xb, w_scaled, ssq)


def _matmul_residual(a, b, res, tm, tn, tk, name):
    m, k = a.shape
    _, n = b.shape
    return pl.pallas_call(
        _mm_res_kernel,
        grid=(m // tm, n // tn, k // tk),
        in_specs=[pl.BlockSpec((tm, tk), lambda i, j, kk: (i, kk)),
                  pl.BlockSpec((tk, tn), lambda i, j, kk: (kk, j)),
                  pl.BlockSpec((tm, tn), lambda i, j, kk: (i, j))],
        out_specs=pl.BlockSpec((tm, tn), lambda i, j, kk: (i, j)),
        out_shape=jax.ShapeDtypeStruct((m, n), F32),
        scratch_shapes=[pltpu.VMEM((tm, tn), F32)],
        compiler_params=_params(("parallel", "arbitrary", "arbitrary"), VMEM_LIMIT_KTILED),
        name=name,
    )(a, b, res)


def _gla_kernel(q_ref, k_ref, v_ref, r_ref, a_ref, wa_ref, ba_ref, g_ref, o_ref,
                bc_ref, st_ref, *, seq):
    c = GLA_CHUNK
    dv = GLA_DV
    shift = int(math.log2(c))

    nb = GLA_PRE_BLOCK
    prow = lax.broadcasted_iota(jnp.int32, (nb, nb), 0)
    pcol = lax.broadcasted_iota(jnp.int32, (nb, nb), 1)
    same_chunk = lax.shift_right_logical(prow, shift) == lax.shift_right_logical(pcol, shift)
    tri = jnp.where(prow >= pcol, jnp.where(same_chunk, 1.0, 0.0), 0.0).astype(BF16)

    def split(x):
        hi = x.astype(BF16)
        return hi, (x - hi.astype(F32)).astype(BF16)

    wa_hi, wa_lo = split(wa_ref[...])
    wa3 = jnp.concatenate([wa_hi, wa_hi, wa_lo], axis=0)

    def cumulate(b, carry):
        r0 = pl.multiple_of(b * nb, nb)
        a_hi, a_lo = split(a_ref[pl.ds(r0, nb), :])
        z = jnp.dot(jnp.concatenate([a_hi, a_lo, a_hi], axis=1), wa3,
                    preferred_element_type=F32) + ba_ref[...]
        la = (jnp.minimum(z, 0.0) - jnp.log(1.0 + jnp.exp(-jnp.abs(z)))) * (1.0 / GLA_TAU)
        cs = jnp.dot(tri, jnp.concatenate(split(la), axis=1), preferred_element_type=F32)
        w = la.shape[1]
        bc_ref[pl.ds(r0, nb), :] = cs[:, :w] + cs[:, w:]
        return carry

    lax.fori_loop(0, seq // nb, cumulate, 0, unroll=4)
    st_ref[...] = jnp.zeros_like(st_ref)

    row2 = lax.broadcasted_iota(jnp.int32, (2 * c, c), 0)
    col2 = lax.broadcasted_iota(jnp.int32, (2 * c, c), 1)
    causal2 = (row2 & (c - 1)) >= col2
    lane = lax.broadcasted_iota(jnp.int32, (1, 2 * GLA_DK), 1)
    head_mask = [(lane < GLA_DK).astype(F32), (lane >= GLA_DK).astype(F32)]
    gvec = g_ref[...]

    def body(i, carry):
        r0 = pl.multiple_of(i * c, c)
        bc = bc_ref[pl.ds(r0, c), :]
        last = bc[c - 1:c, :]
        q = q_ref[pl.ds(r0, c), :].astype(F32) * (GLA_DK ** -0.5)
        k = k_ref[pl.ds(r0, c), :].astype(F32)
        qe = q * jnp.exp(bc)
        ke = (k * jnp.exp(-bc)).astype(BF16)
        kd = (k * jnp.exp(last - bc)).astype(BF16)
        v = v_ref[pl.ds(r0, c), :]
        st = st_ref[...]
        q2 = jnp.concatenate([qe * head_mask[0], qe * head_mask[1]], axis=0).astype(BF16)
        att = lax.dot_general(q2, ke, NT, preferred_element_type=F32)
        att = jnp.where(causal2, att, 0.0).astype(BF16)
        o2 = (jnp.dot(att, v, preferred_element_type=F32)
              + lax.dot_general(q2, st.astype(BF16), NT, preferred_element_type=F32))
        outs = []
        for h in range(2):
            oh = o2[h * c:(h + 1) * c, h * dv:(h + 1) * dv]
            ms = jnp.mean(oh * oh, axis=-1, keepdims=True)
            y = oh * lax.rsqrt(ms + NORM_EPS) * gvec[:, h * dv:(h + 1) * dv]
            rr = r_ref[pl.ds(r0, c), h * dv:(h + 1) * dv].astype(F32)
            outs.append(y * (rr * jax.nn.sigmoid(rr)))
        o_ref[pl.ds(r0, c), :] = jnp.concatenate(outs, axis=1).astype(o_ref.dtype)
        kv = lax.dot_general(v, kd, TN, preferred_element_type=F32)
        st_ref[...] = st * jnp.exp(last) + kv
        return carry

    lax.fori_loop(0, seq // c, body, 0, unroll=GLA_UNROLL)


def _gla(u, small, w_alpha_pad, b_alpha, norm_g, batch, seq):
    t = batch * seq
    pairs = GLA_HEADS // 2
    kw = 2 * GLA_DK
    vw = 2 * GLA_DV
    return pl.pallas_call(
        functools.partial(_gla_kernel, seq=seq),
        grid=(batch, pairs),
        in_specs=[pl.BlockSpec((seq, kw), lambda b, j: (b, U_GQ // kw + j)),
                  pl.BlockSpec((seq, kw), lambda b, j: (b, U_GK // kw + j)),
                  pl.BlockSpec((seq, vw), lambda b, j: (b, U_GV // vw + j)),
                  pl.BlockSpec((seq, vw), lambda b, j: (b, U_GR // vw + j)),
                  pl.BlockSpec((seq, LANES), lambda b, j: (b, 0)),
                  pl.BlockSpec((LANES, kw), lambda b, j: (0, j)),
                  pl.BlockSpec((1, kw), lambda b, j: (0, j)),
                  pl.BlockSpec((1, vw), lambda b, j: (0, j))],
        out_specs=pl.BlockSpec((seq, vw), lambda b, j: (b, j)),
        out_shape=jax.ShapeDtypeStruct((t, GLA_VAL_W), BF16),
        scratch_shapes=[pltpu.VMEM((seq, kw), F32), pltpu.VMEM((vw, kw), F32)],
        compiler_params=_params(("parallel", "arbitrary")),
        name="gla",
    )(u, u, u, u, small, w_alpha_pad, b_alpha, norm_g)


def _cmp_cols_per_tile(tq):
    return tq // CMP_STRIDE


def _cmp_front_pad(seq, tq):
    return seq // CMP_STRIDE - _cmp_cols_per_tile(tq)


def _compress_kernel(xk_ref, xv_ref, pk_ref, pv_ref, w1k_ref, w2k_ref, w1v_ref, w2v_ref,
                     ok_ref, ov_ref, xf_ref, *, front):
    n = CMP_STRIDE
    d = NSA_DIM

    def one(x_ref, pos_ref, w1_ref, w2_ref, o_ref):
        nb = x_ref.shape[0] // n
        xf_ref[...] = x_ref[...].astype(F32)
        top = jnp.zeros((nb, w1_ref.shape[1]), F32)
        bot = jnp.zeros((nb, w1_ref.shape[1]), F32)
        for j in range(n):
            x = xf_ref[pl.ds(j, nb, stride=n), :]
            xa = (x + pos_ref[j:j + 1, :]).astype(BF16)
            xb = (x + pos_ref[n + j:n + j + 1, :]).astype(BF16)
            top = top + jnp.dot(xa, w1_ref[j * d:(j + 1) * d, :],
                                preferred_element_type=F32)
            bot = bot + jnp.dot(xb, w1_ref[(n + j) * d:(n + j + 1) * d, :],
                                preferred_element_type=F32)
        hid = top + pltpu.roll(bot, nb - 1, axis=0)
        hid = jnp.maximum(hid, 0.0).astype(BF16)
        total = o_ref.shape[2]
        o_ref[0, 0, 0:front] = jnp.zeros((front, d), o_ref.dtype)
        o_ref[0, 0, front:front + nb] = jnp.dot(
            hid, w2_ref[...], preferred_element_type=F32).astype(o_ref.dtype)
        o_ref[0, 0, front + nb:total] = jnp.zeros((total - front - nb, d), o_ref.dtype)

    one(xk_ref, pk_ref, w1k_ref, w2k_ref, ok_ref)
    one(xv_ref, pv_ref, w1v_ref, w2v_ref, ov_ref)


def _compress(u, pos_k, pos_v, w1k, w2k, w1v, w2v, batch, seq, u_kc, tq):
    nb = seq // CMP_STRIDE
    front = _cmp_front_pad(seq, tq)
    kc_blk = u_kc // NSA_DIM
    vc_blk = kc_blk + NSA_KV_HEADS
    full = lambda a: pl.BlockSpec(a.shape, lambda b, g: (0,) * a.ndim)
    out = jax.ShapeDtypeStruct((batch, NSA_KV_HEADS, 2 * nb, NSA_DIM), BF16)
    ospec = pl.BlockSpec((1, 1, 2 * nb, NSA_DIM), lambda b, g: (b, g, 0, 0))
    return pl.pallas_call(
        functools.partial(_compress_kernel, front=front),
        grid=(batch, NSA_KV_HEADS),
        in_specs=[pl.BlockSpec((seq, NSA_DIM), lambda b, g: (b, kc_blk + g)),
                  pl.BlockSpec((seq, NSA_DIM), lambda b, g: (b, vc_blk + g)),
                  full(pos_k), full(pos_v), full(w1k), full(w2k), full(w1v), full(w2v)],
        out_specs=[ospec, ospec],
        out_shape=[out, out],
        scratch_shapes=[pltpu.VMEM((seq, NSA_DIM), F32)],
        compiler_params=_params(("parallel", "arbitrary")),
        name="nsa_compress",
    )(u, u, pos_k, pos_v, w1k, w2k, w1v, w2v)


def _cmp_select_kernel(q_ref, kc_ref, vc_ref, pc_ref, ms_ref, o_ref, sel_ref, *,
                       tq, nb, n_s, top_n, front):
    i = pl.program_id(2)
    d = NSA_DIM
    w0 = pl.multiple_of(i * _cmp_cols_per_tile(tq), _cmp_cols_per_tile(tq))
    kc = kc_ref[0, 0, pl.ds(w0, nb), :]
    vc = vc_ref[0, 0, pl.ds(w0, nb), :]
    col = lax.broadcasted_iota(jnp.int32, (tq, nb), 1)
    before_start = jnp.where(col < front - w0, NEG_INF, 0.0)
    psum = jnp.zeros((tq, nb), F32)
    for r in range(NSA_GROUP):
        q = q_ref[:, r * d:(r + 1) * d]
        s = lax.dot_general(q, kc, NT, preferred_element_type=F32) + before_start
        s = jnp.concatenate([s[:, :nb - LANES], s[:, nb - LANES:] + pc_ref[r]], axis=1)
        m = jnp.max(s, axis=-1, keepdims=True)
        e = jnp.exp2(s - m)
        p = e / jnp.sum(e, axis=-1, keepdims=True)
        p = jnp.where(s > 0.5 * NEG_INF, p, 0.0)
        o_ref[:, r * d:(r + 1) * d] = jnp.dot(
            p.astype(BF16), vc, preferred_element_type=F32).astype(o_ref.dtype)
        psum = psum + p
    imp = lax.dot_general(ms_ref[pl.ds(w0, nb), :], psum, (((0,), (1,)), ((), ())),
                          preferred_element_type=F32, precision=HIGHEST)
    blk = lax.broadcasted_iota(jnp.int32, (n_s, tq), 0)
    pos = i * tq + lax.broadcasted_iota(jnp.int32, (n_s, tq), 1)
    cur = lax.shift_right_logical(pos, int(math.log2(SEL_BLOCK)))
    forced = jnp.where(blk == 0, 1, jnp.where(blk <= cur, jnp.where(blk > cur - SEL_LOCAL, 1, 0), 0))
    score = jnp.where(forced == 1, FORCE_SCORE, jnp.where(blk <= cur, imp, -FORCE_SCORE))
    rank = jnp.zeros((n_s, tq), jnp.int32)
    for j in range(n_s):
        rj = score[j:j + 1, :]
        tie = jnp.where(rj == score, jnp.where(blk > j, 1, 0), 0)
        rank = rank + jnp.where(rj > score, 1, tie)
    selb = jnp.where(rank < top_n, 0.0, NEG_INF)
    if n_s < LANES:
        selb = jnp.concatenate([selb, jnp.zeros((LANES - n_s, tq), F32)], axis=0)
    sel_ref[0, 0] = selb.T.astype(sel_ref.dtype)


def _cmp_select(u, kc, vc, pc, ms, batch, seq, tq):
    t = batch * seq
    nb = seq // CMP_STRIDE
    n_s = seq // SEL_BLOCK
    top_n = min(SEL_TOPK, n_s)
    nq = seq // tq
    qw = NSA_GROUP * NSA_DIM
    kv_spec = pl.BlockSpec((1, 1, 2 * nb, NSA_DIM), lambda b, g, i: (b, g, 0, 0))
    return pl.pallas_call(
        functools.partial(_cmp_select_kernel, tq=tq, nb=nb, n_s=n_s, top_n=top_n,
                          front=_cmp_front_pad(seq, tq)),
        grid=(batch, NSA_KV_HEADS, nq),
        in_specs=[pl.BlockSpec((tq, qw), lambda b, g, i: (b * nq + i, U_NQ // qw + g)),
                  kv_spec, kv_spec,
                  pl.BlockSpec((NSA_GROUP, tq, LANES), lambda b, g, i: (g, 0, 0)),
                  pl.BlockSpec((2 * nb, n_s), lambda b, g, i: (0, 0))],
        out_specs=[pl.BlockSpec((tq, qw), lambda b, g, i: (b * nq + i, g)),
                   pl.BlockSpec((1, 1, tq, LANES), lambda b, g, i: (b, g, i, 0))],
        out_shape=[jax.ShapeDtypeStruct((t, NSA_Q_W), BF16),
                   jax.ShapeDtypeStruct((batch, NSA_KV_HEADS, seq, LANES), BF16)],
        compiler_params=_params(("parallel", "parallel", "arbitrary")),
        name="nsa_cmp_select",
    )(u, kc, vc, pc, ms)


def _flash_kernel(*refs, mode, tq, tk):
    sel = mode == "sel"
    if sel:
        (q_ref, k_ref, v_ref, sel_ref, pw_ref, o_ref,
         qs_ref, ks_ref, m_ref, l_ref, acc_ref) = refs
    else:
        q_ref, k_ref, v_ref, pw_ref, o_ref, vs_ref = refs
    i = pl.program_id(2)
    d = NSA_DIM

    if sel:
        @pl.when(i == 0)
        def _():
            seq = k_ref.shape[0]
            ks_ref[:, :d] = k_ref[...]
            krow = lax.broadcasted_iota(jnp.int32, (seq, LANES), 0)
            klane = lax.broadcasted_iota(jnp.int32, (seq, LANES), 1)
            kblk = lax.shift_right_logical(krow, int(math.log2(SEL_BLOCK)))
            ks_ref[:, d:] = jnp.where(kblk == klane, 1.0, 0.0).astype(ks_ref.dtype)

        for r in range(NSA_GROUP):
            qs_ref[r * tq:(r + 1) * tq, :d] = q_ref[:, r * d:(r + 1) * d]
            qs_ref[r * tq:(r + 1) * tq, d:] = sel_ref[0, 0]
        m_ref[...] = jnp.full_like(m_ref, NEG_INF)
        l_ref[...] = jnp.zeros_like(l_ref)
        acc_ref[...] = jnp.zeros_like(acc_ref)

    if not sel:
        @pl.when(i == 0)
        def _():
            vs_ref[:, :d] = v_ref[...]
            vs_ref[:, d:] = jnp.ones((v_ref.shape[0], d), vs_ref.dtype)

    chains = [(0, NSA_GROUP)] if sel else [(r, r + 1) for r in range(NSA_GROUP)]

    def chain_logits(h0, h1, kt):
        q = qs_ref[h0 * tq:h1 * tq, :] if sel else q_ref[:, h0 * d:h1 * d]
        return lax.dot_general(q, kt, NT, preferred_element_type=F32)

    def online_update(rows, s, vt):
        m_prev = m_ref[rows]
        m_new = jnp.maximum(m_prev, jnp.max(s, axis=-1, keepdims=True))
        alpha = jnp.exp2(m_prev - m_new)
        p = jnp.exp2(s - jnp.concatenate([m_new] * (s.shape[1] // LANES), axis=1))
        l_new = alpha * l_ref[rows] + jnp.sum(p, axis=-1, keepdims=True)
        acc_new = alpha * acc_ref[rows] + jnp.dot(p.astype(BF16), vt,
                                                  preferred_element_type=F32)
        return m_new, l_new, acc_new

    def last_tiles(n):
        width = n * tk
        k0 = pl.multiple_of((i + 1 - n) * tk, tk)
        kt = ks_ref[pl.ds(k0, width), :] if sel else k_ref[pl.ds(k0, width), :]
        vt = v_ref[pl.ds(k0, width), :] if sel else vs_ref[pl.ds(k0, width), :]
        for h0, h1 in chains:
            rows = slice(h0 * tq, h1 * tq)
            s = chain_logits(h0, h1, kt) + pw_ref[0, rows, (3 - n) * tk:]
            if sel:
                _, l, acc = online_update(rows, s, vt)
            else:
                p = jnp.exp2(s - jnp.max(s, axis=-1, keepdims=True))
                acc = jnp.dot(p.astype(BF16), vt, preferred_element_type=F32)
                acc, l = acc[:, :d], acc[:, d:]
            out = acc * (1.0 / l)
            for h in range(h0, h1):
                o_ref[:, h * d:(h + 1) * d] = out[(h - h0) * tq:(h - h0 + 1) * tq, :].astype(
                    o_ref.dtype)

    if sel:
        n_far = jnp.maximum(i - 1, 0)
        odd = n_far % 2

        def far_pair(j):
            k0 = pl.multiple_of(j * (2 * tk), 2 * tk)
            kt = ks_ref[pl.ds(k0, 2 * tk), :]
            vt = v_ref[pl.ds(k0, 2 * tk), :]
            for h0, h1 in chains:
                rows = slice(h0 * tq, h1 * tq)
                m_ref[rows], l_ref[rows], acc_ref[rows] = online_update(
                    rows, chain_logits(h0, h1, kt), vt)

        n_pairs = n_far // 2

        def far(j, carry):
            far_pair(2 * j)
            far_pair(2 * j + 1)
            return carry
        lax.fori_loop(0, n_pairs // 2, far, 0)
        pl.when(n_pairs % 2 == 1)(lambda: far_pair(n_pairs - 1))
        pl.when(i == 0)(lambda: last_tiles(1))
        pl.when(jnp.logical_and(i >= 1, odd == 0))(lambda: last_tiles(2))
        pl.when(odd == 1)(lambda: last_tiles(3))
    else:
        pl.when(i == 0)(lambda: last_tiles(1))
        pl.when(i == 1)(lambda: last_tiles(2))
        pl.when(i >= 2)(lambda: last_tiles(3))


def _flash(u, k_blk, v_blk, pw, batch, seq, mode, sel=None, tq=ATT_TILE, tk=ATT_TILE):
    assert tq == tk and 2 * tk >= REL_MAX_DIST and 3 * tk > WINDOW >= 2 * tk
    t = batch * seq
    nq = seq // tq
    qw = NSA_GROUP * NSA_DIM
    rows = NSA_GROUP * tq
    d = NSA_DIM
    q_spec = pl.BlockSpec((tq, qw), lambda b, g, i: (b * nq + i, U_NQ // qw + g))
    k_spec = pl.BlockSpec((seq, d), lambda b, g, i: (b, k_blk + g))
    v_spec = pl.BlockSpec((seq, d), lambda b, g, i: (b, v_blk + g))
    b_spec = pl.BlockSpec((1, rows, 3 * tk), lambda b, g, i: (g, 0, 0))
    if mode == "sel":
        in_specs = [q_spec, k_spec, v_spec,
                    pl.BlockSpec((1, 1, tq, LANES), lambda b, g, i: (b, g, i, 0)), b_spec]
        args = (u, u, u, sel, pw)
        scratch = ([pltpu.VMEM((rows, 2 * d), BF16), pltpu.VMEM((seq, 2 * d), BF16)]
                   + [pltpu.VMEM((rows, LANES), F32)] * 3)
    else:
        in_specs = [q_spec, k_spec, v_spec, b_spec]
        args = (u, u, u, pw)
        scratch = [pltpu.VMEM((seq, 2 * d), BF16)]
    return pl.pallas_call(
        functools.partial(_flash_kernel, mode=mode, tq=tq, tk=tk),
        grid=(batch, NSA_KV_HEADS, nq),
        in_specs=in_specs,
        out_specs=pl.BlockSpec((tq, qw), lambda b, g, i: (b * nq + i, g)),
        out_shape=jax.ShapeDtypeStruct((t, NSA_Q_W), BF16),
        scratch_shapes=scratch,
        compiler_params=_params(("parallel", "parallel", "arbitrary")),
        name="nsa_flash_" + mode,
    )(*args)


def _combine_kernel(oc_ref, os_ref, ow_ref, ng_ref, o_ref):
    gates = jax.nn.sigmoid(ng_ref[...])
    d = NSA_DIM
    for h in range(NSA_HEADS):
        c0 = GLA_GATE_RANK + 3 * h
        sl = slice(h * d, (h + 1) * d)
        o = (gates[:, c0:c0 + 1] * oc_ref[:, sl].astype(F32)
             + gates[:, c0 + 1:c0 + 2] * os_ref[:, sl].astype(F32)
             + gates[:, c0 + 2:c0 + 3] * ow_ref[:, sl].astype(F32))
        o_ref[:, sl] = o.astype(o_ref.dtype)


def _combine(o_cmp, o_sel, o_win, small, tm=512):
    t, w = o_cmp.shape
    spec = pl.BlockSpec((tm, w), lambda i: (i, 0))
    return pl.pallas_call(
        _combine_kernel,
        grid=(t // tm,),
        in_specs=[spec, spec, spec, pl.BlockSpec((tm, LANES), lambda i: (i, 0))],
        out_specs=spec,
        out_shape=jax.ShapeDtypeStruct((t, w), BF16),
        compiler_params=_params(("parallel",)),
        name="nsa_combine",
    )(o_cmp, o_sel, o_win, small)


def _rel_bucket(dist):
    n = jnp.maximum(dist, 0)
    max_exact = REL_BUCKETS // 2
    nf = jnp.maximum(n, max_exact).astype(F32)
    large = max_exact + (jnp.log(nf / max_exact) / math.log(REL_MAX_DIST / max_exact)
                         * (REL_BUCKETS - max_exact)).astype(jnp.int32)
    large = jnp.minimum(large, REL_BUCKETS - 1)
    return jnp.where(n < max_exact, n, large)


def _bias_by_distance(table, dist):
    onehot = (_rel_bucket(jnp.asarray(dist, jnp.int32))[..., None]
              == jnp.arange(REL_BUCKETS, dtype=jnp.int32)).astype(F32)
    return jnp.dot(onehot, table, precision=HIGHEST)


def _bias_tables(rel_table, seq, tq, tk):
    table = rel_table.astype(F32) * LOG2E
    heads = table.shape[1]
    far = table[REL_BUCKETS - 1]
    period = 3 * tk + 1
    vec = _bias_by_distance(table, np.arange(2 * tk)) - far
    vec = jnp.concatenate([vec, jnp.full((period - 2 * tk, heads), NEG_INF, F32)], axis=0).T
    skew = jnp.tile(vec, (1, tk))[:, :tk * (period - 1)].reshape(heads, tk, period - 1)
    tiles = skew[:, :, :2 * tq].transpose(0, 2, 1)
    p0 = tiles[:, :tq].reshape(NSA_KV_HEADS, NSA_GROUP * tq, tk)
    p1 = tiles[:, tq:].reshape(NSA_KV_HEADS, NSA_GROUP * tq, tk)
    a2 = (np.arange(NSA_GROUP * tq) % tq)[:, None]
    edge = np.where(2 * tk + a2 - np.arange(tk)[None, :] < WINDOW, 0.0, NEG_INF).astype(np.float32)
    edge = jnp.broadcast_to(jnp.asarray(edge), p0.shape)
    pw_sel = jnp.concatenate([jnp.zeros_like(p0), p1, p0], axis=2)
    pw_win = jnp.concatenate([edge, p1, p0], axis=2)
    front = _cmp_front_pad(seq, tq)
    nb = seq // CMP_STRIDE
    a = np.arange(tq)[:, None]
    rel_blk = np.arange(nb - LANES, nb)[None, :] - front
    dc = a - CMP_STRIDE * rel_blk - (CMP_BLOCK - 1)
    assert (a - CMP_STRIDE * (nb - LANES - 1 - front) - (CMP_BLOCK - 1)).min() >= REL_MAX_DIST
    pc = _bias_by_distance(table, np.maximum(dc, 0)) - far
    pc = jnp.where(jnp.asarray(dc >= 0)[..., None], pc, NEG_INF).transpose(2, 0, 1)
    return pw_sel, pw_win, pc


def _cmp_to_sel_matrix(seq, tq):
    nb = seq // CMP_STRIDE
    n_c = nb - 1
    n_s = seq // SEL_BLOCK
    front = _cmp_front_pad(seq, tq)
    m_mat = np.zeros((2 * nb, n_s), np.float32)
    j = np.arange(n_s)
    for m in range(SEL_BLOCK // CMP_STRIDE):
        for n in range(CMP_BLOCK // CMP_STRIDE):
            c = (SEL_BLOCK // CMP_STRIDE) * j + m - n
            ok = (c >= 0) & (c < n_c)
            np.add.at(m_mat, (front + c[ok], j[ok]), 1.0)
    return jnp.asarray(m_mat)


def _nsa(u, small, pos_k, pos_v, w1k, w2k, w1v, w2v, rel_table, batch, seq, u_kc):
    tq = ATT_TILE
    kc_blk = u_kc // NSA_DIM
    ksl_blk = kc_blk + 2 * NSA_KV_HEADS
    vsl_blk = kc_blk + 3 * NSA_KV_HEADS
    kw_blk = kc_blk + 4 * NSA_KV_HEADS
    vw_blk = kc_blk + 5 * NSA_KV_HEADS
    pw_sel, pw_win, pc = _bias_tables(rel_table, seq, tq, tq)
    ms = _cmp_to_sel_matrix(seq, tq)
    kc, vc = _compress(u, pos_k.astype(F32), pos_v.astype(F32),
                       w1k.astype(BF16), w2k.astype(BF16),
                       w1v.astype(BF16), w2v.astype(BF16), batch, seq, u_kc, tq)
    o_cmp, sel = _cmp_select(u, kc, vc, pc, ms, batch, seq, tq)
    o_sel = _flash(u, ksl_blk, vsl_blk, pw_sel, batch, seq, "sel", sel=sel)
    o_win = _flash(u, kw_blk, vw_blk, pw_win, batch, seq, "win")
    return _combine(o_cmp, o_sel, o_win, small)


def kernel(x, g_mix_norm, w_in, w_alpha2, b_alpha, gla_norm_g, cmp_pos_k, cmp_pos_v,
           phi_k_w1, phi_k_w2, phi_v_w1, phi_v_w2, rel_bias_table, w_gla_proj,
           w_nsa_proj, w_out, g_mlp_norm, w_up, w_down, g_final_norm):
    batch, seq, d = x.shape
    t = batch * seq
    depth = w_in.shape[0]
    u_kc = U_MG + 2 * d
    xf = x.reshape(t, d)
    for l in range(depth):
        w_main_t, w_small_t = _w_in_prep(w_in, l, d)
        wa_pad = jnp.concatenate(
            [w_alpha2[l], jnp.zeros((LANES - GLA_GATE_RANK, GLA_KEY_W), w_alpha2.dtype)],
            axis=0).astype(F32)

        h = _rmsnorm(xf, g_mix_norm[l], BF16)
        u = _matmul_nt(h, w_main_t, BF16, MM_TILE, MM_TILE, "in_proj")
        small = _matmul_nt(h, w_small_t, F32, MM_TILE, LANES, "in_proj_small")
        o_gla = _gla(u, small, wa_pad, b_alpha[l].reshape(1, -1).astype(F32),
                     gla_norm_g[l].reshape(1, -1).astype(F32), batch, seq)
        o_nsa = _nsa(u, small, cmp_pos_k[l], cmp_pos_v[l], phi_k_w1[l], phi_k_w2[l],
                     phi_v_w1[l], phi_v_w2[l], rel_bias_table, batch, seq, u_kc)
        mix = _mix(o_gla, w_gla_proj[l].astype(BF16), o_nsa, w_nsa_proj[l].astype(BF16), u, d)
        xf, xb, ssq = _out_proj(mix, w_out[l].astype(BF16), xf)
        w_up_g = (w_up[l] * g_mlp_norm[l].astype(F32)[:, None]).astype(BF16)
        act = _mlp_up(xb, w_up_g, ssq)
        xf = _matmul_residual(act, w_down[l].astype(BF16), xf, MM_TILE, MM_TILE, 4096, "mlp_down")
    out = _rmsnorm(xf, g_final_norm, F32)
    return out.reshape(batch, seq, d)
```

```python
import functools
import math

import numpy as np
import jax
import jax.numpy as jnp
from jax import lax
from jax.experimental import pallas as pl
from jax.experimental.pallas import tpu as pltpu

F32 = jnp.float32
BF16 = jnp.bfloat16
HIGHEST = lax.Precision.HIGHEST

NORM_EPS = 1e-6
GLA_HEADS = 16
GLA_DK = 64
GLA_DV = 128
GLA_KEY_W = GLA_HEADS * GLA_DK
GLA_VAL_W = GLA_HEADS * GLA_DV
GLA_GATE_RANK = 16
GLA_TAU = 16.0
GLA_CHUNK = 64
NSA_HEADS = 16
NSA_KV_HEADS = 4
NSA_GROUP = NSA_HEADS // NSA_KV_HEADS
NSA_DIM = 128
NSA_Q_W = NSA_HEADS * NSA_DIM
NSA_KV_W = NSA_KV_HEADS * NSA_DIM
CMP_BLOCK = 32
CMP_STRIDE = 16
SEL_BLOCK = 64
SEL_TOPK = 16
SEL_LOCAL = 2
WINDOW = 512
REL_BUCKETS = 32
REL_MAX_DIST = 128
NEG_INF = -1e30
FORCE_SCORE = 1e4
LOG2E = math.log2(math.e)

LANES = 128
VMEM_LIMIT = 56 * 1024 * 1024
VMEM_LIMIT_KTILED = 62 * 1024 * 1024
ATT_TILE = 256
MM_TILE = 1024
GLA_PRE_BLOCK = 256
GLA_UNROLL = 16

U_GQ = 0
U_GK = U_GQ + GLA_KEY_W
U_GV = U_GK + GLA_KEY_W
U_GR = U_GV + GLA_VAL_W
U_NQ = U_GR + GLA_VAL_W
U_MG = U_NQ + NSA_Q_W

NT = (((1,), (1,)), ((), ()))
TN = (((0,), (0,)), ((), ()))


def _params(sem, vmem_limit=VMEM_LIMIT):
    return pltpu.CompilerParams(dimension_semantics=sem, vmem_limit_bytes=vmem_limit)


def _rmsnorm_kernel(x_ref, g_ref, o_ref):
    x = x_ref[...].astype(F32)
    ms = jnp.mean(x * x, axis=-1, keepdims=True)
    o_ref[...] = (x * lax.rsqrt(ms + NORM_EPS) * g_ref[...]).astype(o_ref.dtype)


def _rmsnorm(x2, g, out_dtype, tm=256):
    t, d = x2.shape
    return pl.pallas_call(
        _rmsnorm_kernel,
        grid=(t // tm,),
        in_specs=[pl.BlockSpec((tm, d), lambda i: (i, 0)),
                  pl.BlockSpec((1, d), lambda i: (0, 0))],
        out_specs=pl.BlockSpec((tm, d), lambda i: (i, 0)),
        out_shape=jax.ShapeDtypeStruct((t, d), out_dtype),
        compiler_params=_params(("parallel",)),
        name="rmsnorm",
    )(x2, g.reshape(1, d).astype(F32))


def _rmsnorm_proj_kernel(x_ref, g_ref, wt_ref, o_ref, p_ref):
    x = x_ref[...].astype(F32)
    ms = jnp.mean(x * x, axis=-1, keepdims=True)
    h = (x * lax.rsqrt(ms + NORM_EPS) * g_ref[...]).astype(o_ref.dtype)
    o_ref[...] = h
    p_ref[...] = lax.dot_general(h, wt_ref[...], NT, preferred_element_type=F32)


def _rmsnorm_proj(x2, g, w_t, tm=256):
    t, d = x2.shape
    n = w_t.shape[0]
    return pl.pallas_call(
        _rmsnorm_proj_kernel,
        grid=(t // tm,),
        in_specs=[pl.BlockSpec((tm, d), lambda i: (i, 0)),
                  pl.BlockSpec((1, d), lambda i: (0, 0)),
                  pl.BlockSpec((n, d), lambda i: (0, 0))],
        out_specs=[pl.BlockSpec((tm, d), lambda i: (i, 0)),
                   pl.BlockSpec((tm, n), lambda i: (i, 0))],
        out_shape=[jax.ShapeDtypeStruct((t, d), BF16), jax.ShapeDtypeStruct((t, n), F32)],
        compiler_params=_params(("parallel",)),
        name="rmsnorm_proj",
    )(x2, g.reshape(1, d).astype(F32), w_t)


W_ROW_ALIGN = 16


def _w_in_prep_kernel(src_ref, w_ref, ga_ref, ng_ref, o_ref, os_ref, *, q_blocks, q_scale):
    del src_ref
    i = pl.program_id(0)
    is_q = jnp.logical_and(i >= q_blocks[0], i < q_blocks[1])
    scale = jnp.where(is_q, q_scale, 1.0).astype(F32)
    o_ref[...] = (w_ref[0] * scale).astype(o_ref.dtype)

    @pl.when(i == 0)
    def _():
        n_ga = ga_ref.shape[1]
        n_ng = ng_ref.shape[1]
        os_ref[0:n_ga] = ga_ref[0].astype(os_ref.dtype)
        os_ref[n_ga:n_ga + n_ng] = ng_ref[0].astype(os_ref.dtype)
        os_ref[n_ga + n_ng:] = jnp.zeros((os_ref.shape[0] - n_ga - n_ng, os_ref.shape[1]),
                                         os_ref.dtype)


def _w_in_prep(w_in, layer, d, tr=512):
    n = w_in.shape[2]
    s_ga = 2 * GLA_KEY_W + GLA_VAL_W
    s_gr = s_ga + GLA_GATE_RANK
    s_nq = s_gr + GLA_VAL_W
    s_kc = s_nq + NSA_Q_W
    s_ng = s_kc + 6 * NSA_KV_W
    s_mg = s_ng + 3 * NSA_HEADS
    assert n == s_mg + 2 * d
    w_t = jnp.swapaxes(w_in, 1, 2)
    pieces = [(0, s_ga, U_GQ), (s_gr, s_nq, U_GR), (s_nq, s_kc, U_NQ),
              (s_mg, n, U_MG), (s_kc, s_ng, U_MG + 2 * d)]
    src_rows = []
    for s0, s1, d0 in pieces:
        assert s0 % W_ROW_ALIGN == 0 and (s1 - s0) % tr == 0 and d0 == len(src_rows) * tr
        src_rows += [r // W_ROW_ALIGN for r in range(s0, s1, tr)]
    n_main = len(src_rows) * tr
    def rows_at(start, count):
        return pl.BlockSpec((pl.Element(1), pl.Element(count), pl.Element(d)),
                            lambda i, src: (layer, start, 0))

    return pl.pallas_call(
        functools.partial(_w_in_prep_kernel,
                          q_blocks=(U_NQ // tr, (U_NQ + NSA_Q_W) // tr),
                          q_scale=NSA_DIM ** -0.5 * LOG2E),
        grid_spec=pltpu.PrefetchScalarGridSpec(
            num_scalar_prefetch=1,
            grid=(n_main // tr,),
            in_specs=[pl.BlockSpec(
                (pl.Element(1), pl.Element(tr), pl.Element(d)),
                lambda i, src: (layer, src[i] * W_ROW_ALIGN, 0)),
                rows_at(s_ga, s_gr - s_ga), rows_at(s_ng, s_mg - s_ng)],
            out_specs=[pl.BlockSpec((tr, d), lambda i, src: (i, 0)),
                       pl.BlockSpec((LANES, d), lambda i, src: (0, 0))]),
        out_shape=[jax.ShapeDtypeStruct((n_main, d), BF16),
                   jax.ShapeDtypeStruct((LANES, d), BF16)],
        compiler_params=_params(("arbitrary",)),
        name="w_in_prep",
    )(jnp.asarray(src_rows, jnp.int32), w_t, w_t, w_t)


def _mm_nt_kernel(a_ref, bt_ref, o_ref):
    o_ref[...] = lax.dot_general(a_ref[...], bt_ref[...], NT,
                                 preferred_element_type=F32).astype(o_ref.dtype)


def _matmul_nt(a, b_t, out_dtype, tm, tn, name):
    m, k = a.shape
    n = b_t.shape[0]
    return pl.pallas_call(
        _mm_nt_kernel,
        grid=(m // tm, n // tn),
        in_specs=[pl.BlockSpec((tm, k), lambda i, j: (i, 0)),
                  pl.BlockSpec((tn, k), lambda i, j: (j, 0))],
        out_specs=pl.BlockSpec((tm, tn), lambda i, j: (i, j)),
        out_shape=jax.ShapeDtypeStruct((m, n), out_dtype),
        compiler_params=_params(("parallel", "arbitrary")),
        name=name,
    )(a, b_t)


def _mix_kernel(og_ref, wg_ref, on_ref, wn_ref, mg_ref, mn_ref, o_ref):
    yg = jnp.dot(og_ref[...], wg_ref[...], preferred_element_type=F32)
    yn = jnp.dot(on_ref[...], wn_ref[...], preferred_element_type=F32)
    o = (jax.nn.sigmoid(mg_ref[...].astype(F32)) * yg
         + jax.nn.sigmoid(mn_ref[...].astype(F32)) * yn)
    o_ref[...] = o.astype(o_ref.dtype)


def _mix(o_gla, w_g, o_nsa, w_n, u, d, tm=MM_TILE, tn=MM_TILE):
    t = o_gla.shape[0]
    mg_blk = U_MG // tn
    mn_blk = (U_MG + d) // tn
    return pl.pallas_call(
        _mix_kernel,
        grid=(t // tm, d // tn),
        in_specs=[pl.BlockSpec((tm, o_gla.shape[1]), lambda i, j: (i, 0)),
                  pl.BlockSpec((w_g.shape[0], tn), lambda i, j: (0, j)),
                  pl.BlockSpec((tm, o_nsa.shape[1]), lambda i, j: (i, 0)),
                  pl.BlockSpec((w_n.shape[0], tn), lambda i, j: (0, j)),
                  pl.BlockSpec((tm, tn), lambda i, j: (i, mg_blk + j)),
                  pl.BlockSpec((tm, tn), lambda i, j: (i, mn_blk + j))],
        out_specs=pl.BlockSpec((tm, tn), lambda i, j: (i, j)),
        out_shape=jax.ShapeDtypeStruct((t, d), BF16),
        compiler_params=_params(("parallel", "arbitrary")),
        name="mix",
    )(o_gla, w_g, o_nsa, w_n, u, u)


def _mm_res_kernel(a_ref, b_ref, r_ref, o_ref, acc_ref):
    kk = pl.program_id(2)

    @pl.when(kk == 0)
    def _():
        acc_ref[...] = r_ref[...]

    acc_ref[...] += jnp.dot(a_ref[...], b_ref[...], preferred_element_type=F32)

    @pl.when(kk == pl.num_programs(2) - 1)
    def _():
        o_ref[...] = acc_ref[...]


def _out_proj_kernel(a_ref, b_ref, r_ref, o_ref, ob_ref, ss_ref):
    y = r_ref[...] + jnp.dot(a_ref[...], b_ref[...], preferred_element_type=F32)
    o_ref[...] = y
    ob_ref[...] = y.astype(ob_ref.dtype)
    ss_ref[...] = jnp.broadcast_to(jnp.sum(y * y, axis=-1, keepdims=True), ss_ref.shape)


def _out_proj(a, b, res, tm=MM_TILE, tn=MM_TILE):
    m, k = a.shape
    _, n = b.shape
    return pl.pallas_call(
        _out_proj_kernel,
        grid=(m // tm, n // tn),
        in_specs=[pl.BlockSpec((tm, k), lambda i, j: (i, 0)),
                  pl.BlockSpec((k, tn), lambda i, j: (0, j)),
                  pl.BlockSpec((tm, tn), lambda i, j: (i, j))],
        out_specs=[pl.BlockSpec((tm, tn), lambda i, j: (i, j)),
                   pl.BlockSpec((tm, tn), lambda i, j: (i, j)),
                   pl.BlockSpec((tm, LANES), lambda i, j: (i, j))],
        out_shape=[jax.ShapeDtypeStruct((m, n), F32), jax.ShapeDtypeStruct((m, n), BF16),
                   jax.ShapeDtypeStruct((m, n // tn * LANES), F32)],
        compiler_params=_params(("parallel", "arbitrary"), VMEM_LIMIT_KTILED),
        name="out_proj",
    )(a, b, res)


def _mlp_up_kernel(a_ref, b_ref, ss_ref, o_ref, *, d):
    acc = jnp.dot(a_ref[...], b_ref[...], preferred_element_type=F32)
    ss = ss_ref[...]
    total = ss[:, :LANES]
    for p in range(1, ss.shape[1] // LANES):
        total = total + ss[:, p * LANES:(p + 1) * LANES]
    inv = 1.0 / (total[:, :1] * (1.0 / d) + NORM_EPS)
    r = jnp.maximum(acc, 0.0)
    o_ref[...] = (r * r * inv).astype(o_ref.dtype)


def _mlp_up(xb, w_scaled, ssq, tm=MM_TILE, tn=MM_TILE):
    m, k = xb.shape
    _, n = w_scaled.shape
    return pl.pallas_call(
        functools.partial(_mlp_up_kernel, d=k),
        grid=(m // tm, n // tn),
        in_specs=[pl.BlockSpec((tm, k), lambda i, j: (i, 0)),
                  pl.BlockSpec((k, tn), lambda i, j: (0, j)),
                  pl.BlockSpec((tm, ssq.shape[1]), lambda i, j: (i, 0))],
        out_specs=pl.BlockSpec((tm, tn), lambda i, j: (i, j)),
        out_shape=jax.ShapeDtypeStruct((m, n), BF16),
        compiler_params=_params(("parallel", "arbitrary")),
        name="mlp_up",
    )(xb, w_scaled, ssq)


def _matmul_residual(a, b, res, tm, tn, tk, name):
    m, k = a.shape
    _, n = b.shape
    return pl.pallas_call(
        _mm_res_kernel,
        grid=(m // tm, n // tn, k // tk),
        in_specs=[pl.BlockSpec((tm, tk), lambda i, j, kk: (i, kk)),
                  pl.BlockSpec((tk, tn), lambda i, j, kk: (kk, j)),
                  pl.BlockSpec((tm, tn), lambda i, j, kk: (i, j))],
        out_specs=pl.BlockSpec((tm, tn), lambda i, j, kk: (i, j)),
        out_shape=jax.ShapeDtypeStruct((m, n), F32),
        scratch_shapes=[pltpu.VMEM((tm, tn), F32)],
        compiler_params=_params(("parallel", "arbitrary", "arbitrary"), VMEM_LIMIT_KTILED),
        name=name,
    )(a, b, res)


def _gla_kernel(q_ref, k_ref, v_ref, r_ref, a_ref, wa_ref, ba_ref, g_ref, o_ref,
                bc_ref, st_ref, *, seq):
    c = GLA_CHUNK
    dv = GLA_DV
    shift = int(math.log2(c))

    nb = GLA_PRE_BLOCK
    prow = lax.broadcasted_iota(jnp.int32, (nb, nb), 0)
    pcol = lax.broadcasted_iota(jnp.int32, (nb, nb), 1)
    same_chunk = lax.shift_right_logical(prow, shift) == lax.shift_right_logical(pcol, shift)
    tri = jnp.where(prow >= pcol, jnp.where(same_chunk, 1.0, 0.0), 0.0).astype(BF16)

    def split(x):
        hi = x.astype(BF16)
        return hi, (x - hi.astype(F32)).astype(BF16)

    wa_hi, wa_lo = split(wa_ref[...])
    wa3 = jnp.concatenate([wa_hi, wa_hi, wa_lo], axis=0)

    def cumulate(b, carry):
        r0 = pl.multiple_of(b * nb, nb)
        a_hi, a_lo = split(a_ref[pl.ds(r0, nb), :])
        z = jnp.dot(jnp.concatenate([a_hi, a_lo, a_hi], axis=1), wa3,
                    preferred_element_type=F32) + ba_ref[...]
        la = (jnp.minimum(z, 0.0) - jnp.log(1.0 + jnp.exp(-jnp.abs(z)))) * (1.0 / GLA_TAU)
        cs = jnp.dot(tri, jnp.concatenate(split(la), axis=1), preferred_element_type=F32)
        w = la.shape[1]
        bc_ref[pl.ds(r0, nb), :] = cs[:, :w] + cs[:, w:]
        return carry

    lax.fori_loop(0, seq // nb, cumulate, 0, unroll=8)
    st_ref[...] = jnp.zeros_like(st_ref)

    row2 = lax.broadcasted_iota(jnp.int32, (2 * c, c), 0)
    col2 = lax.broadcasted_iota(jnp.int32, (2 * c, c), 1)
    causal2 = (row2 & (c - 1)) >= col2
    lane = lax.broadcasted_iota(jnp.int32, (1, 2 * GLA_DK), 1)
    head_mask = [(lane < GLA_DK).astype(F32), (lane >= GLA_DK).astype(F32)]
    gvec = g_ref[...]

    def body(i, carry):
        r0 = pl.multiple_of(i * c, c)
        bc = bc_ref[pl.ds(r0, c), :]
        last = bc[c - 1:c, :]
        q = q_ref[pl.ds(r0, c), :].astype(F32) * (GLA_DK ** -0.5)
        k = k_ref[pl.ds(r0, c), :].astype(F32)
        qe = q * jnp.exp(bc)
        ke = (k * jnp.exp(-bc)).astype(BF16)
        kd = (k * jnp.exp(last - bc)).astype(BF16)
        v = v_ref[pl.ds(r0, c), :]
        st = st_ref[...]
        q2 = jnp.concatenate([qe * head_mask[0], qe * head_mask[1]], axis=0).astype(BF16)
        att = lax.dot_general(q2, ke, NT, preferred_element_type=F32)
        att = jnp.where(causal2, att, 0.0).astype(BF16)
        o2 = (jnp.dot(att, v, preferred_element_type=F32)
              + lax.dot_general(q2, st.astype(BF16), NT, preferred_element_type=F32))
        outs = []
        for h in range(2):
            oh = o2[h * c:(h + 1) * c, h * dv:(h + 1) * dv]
            ms = jnp.mean(oh * oh, axis=-1, keepdims=True)
            y = oh * lax.rsqrt(ms + NORM_EPS) * gvec[:, h * dv:(h + 1) * dv]
            rr = r_ref[pl.ds(r0, c), h * dv:(h + 1) * dv].astype(F32)
            outs.append(y * (rr * jax.nn.sigmoid(rr)))
        o_ref[pl.ds(r0, c), :] = jnp.concatenate(outs, axis=1).astype(o_ref.dtype)
        kv = lax.dot_general(v, kd, TN, preferred_element_type=F32)
        st_ref[...] = st * jnp.exp(last) + kv
        return carry

    lax.fori_loop(0, seq // c, body, 0, unroll=GLA_UNROLL)


def _gla(u, small, w_alpha_pad, b_alpha, norm_g, batch, seq):
    t = batch * seq
    pairs = GLA_HEADS // 2
    kw = 2 * GLA_DK
    vw = 2 * GLA_DV
    return pl.pallas_call(
        functools.partial(_gla_kernel, seq=seq),
        grid=(batch, pairs),
        in_specs=[pl.BlockSpec((seq, kw), lambda b, j: (b, U_GQ // kw + j)),
                  pl.BlockSpec((seq, kw), lambda b, j: (b, U_GK // kw + j)),
                  pl.BlockSpec((seq, vw), lambda b, j: (b, U_GV // vw + j)),
                  pl.BlockSpec((seq, vw), lambda b, j: (b, U_GR // vw + j)),
                  pl.BlockSpec((seq, LANES), lambda b, j: (b, 0)),
                  pl.BlockSpec((LANES, kw), lambda b, j: (0, j)),
                  pl.BlockSpec((1, kw), lambda b, j: (0, j)),
                  pl.BlockSpec((1, vw), lambda b, j: (0, j))],
        out_specs=pl.BlockSpec((seq, vw), lambda b, j: (b, j)),
        out_shape=jax.ShapeDtypeStruct((t, GLA_VAL_W), BF16),
        scratch_shapes=[pltpu.VMEM((seq, kw), F32), pltpu.VMEM((vw, kw), F32)],
        compiler_params=_params(("parallel", "arbitrary")),
        name="gla",
    )(u, u, u, u, small, w_alpha_pad, b_alpha, norm_g)


def _cmp_cols_per_tile(tq):
    return tq // CMP_STRIDE


def _cmp_front_pad(seq, tq):
    return seq // CMP_STRIDE - _cmp_cols_per_tile(tq)


def _compress_kernel(xk_ref, xv_ref, pk_ref, pv_ref, w1k_ref, w2k_ref, w1v_ref, w2v_ref,
                     ok_ref, ov_ref, xf_ref, *, front):
    n = CMP_STRIDE
    d = NSA_DIM

    def one(x_ref, pos_ref, w1_ref, w2_ref, o_ref):
        nb = x_ref.shape[0] // n
        xf_ref[...] = x_ref[...].astype(F32)
        top = jnp.zeros((nb, w1_ref.shape[1]), F32)
        bot = jnp.zeros((nb, w1_ref.shape[1]), F32)
        for j in range(n):
            x = xf_ref[pl.ds(j, nb, stride=n), :]
            xa = (x + pos_ref[j:j + 1, :]).astype(BF16)
            xb = (x + pos_ref[n + j:n + j + 1, :]).astype(BF16)
            top = top + jnp.dot(xa, w1_ref[j * d:(j + 1) * d, :],
                                preferred_element_type=F32)
            bot = bot + jnp.dot(xb, w1_ref[(n + j) * d:(n + j + 1) * d, :],
                                preferred_element_type=F32)
        hid = top + pltpu.roll(bot, nb - 1, axis=0)
        hid = jnp.maximum(hid, 0.0).astype(BF16)
        total = o_ref.shape[2]
        o_ref[0, 0, 0:front] = jnp.zeros((front, d), o_ref.dtype)
        o_ref[0, 0, front:front + nb] = jnp.dot(
            hid, w2_ref[...], preferred_element_type=F32).astype(o_ref.dtype)
        o_ref[0, 0, front + nb:total] = jnp.zeros((total - front - nb, d), o_ref.dtype)

    one(xk_ref, pk_ref, w1k_ref, w2k_ref, ok_ref)
    one(xv_ref, pv_ref, w1v_ref, w2v_ref, ov_ref)


def _compress(u, pos_k, pos_v, w1k, w2k, w1v, w2v, batch, seq, u_kc, tq):
    nb = seq // CMP_STRIDE
    front = _cmp_front_pad(seq, tq)
    kc_blk = u_kc // NSA_DIM
    vc_blk = kc_blk + NSA_KV_HEADS
    full = lambda a: pl.BlockSpec(a.shape, lambda b, g: (0,) * a.ndim)
    out = jax.ShapeDtypeStruct((batch, NSA_KV_HEADS, 2 * nb, NSA_DIM), BF16)
    ospec = pl.BlockSpec((1, 1, 2 * nb, NSA_DIM), lambda b, g: (b, g, 0, 0))
    return pl.pallas_call(
        functools.partial(_compress_kernel, front=front),
        grid=(batch, NSA_KV_HEADS),
        in_specs=[pl.BlockSpec((seq, NSA_DIM), lambda b, g: (b, kc_blk + g)),
                  pl.BlockSpec((seq, NSA_DIM), lambda b, g: (b, vc_blk + g)),
                  full(pos_k), full(pos_v), full(w1k), full(w2k), full(w1v), full(w2v)],
        out_specs=[ospec, ospec],
        out_shape=[out, out],
        scratch_shapes=[pltpu.VMEM((seq, NSA_DIM), F32)],
        compiler_params=_params(("parallel", "arbitrary")),
        name="nsa_compress",
    )(u, u, pos_k, pos_v, w1k, w2k, w1v, w2v)


def _cmp_select_kernel(q_ref, kc_ref, vc_ref, pc_ref, ms_ref, o_ref, sel_ref, score_ref, *,
                       tq, nb, n_s, top_n, front):
    i = pl.program_id(2)
    d = NSA_DIM
    w0 = pl.multiple_of(i * _cmp_cols_per_tile(tq), _cmp_cols_per_tile(tq))
    kc = kc_ref[0, 0, pl.ds(w0, nb), :]
    vc = vc_ref[0, 0, pl.ds(w0, nb), :]
    rows = NSA_GROUP * tq
    q = jnp.concatenate([q_ref[:, r * d:(r + 1) * d] for r in range(NSA_GROUP)], axis=0)
    col = lax.broadcasted_iota(jnp.int32, (rows, nb), 1)
    before_start = jnp.where(col < front - w0, NEG_INF, 0.0)
    s = lax.dot_general(q, kc, NT, preferred_element_type=F32) + before_start
    bias = pc_ref[...].reshape(rows, LANES)
    s = jnp.concatenate([s[:, :nb - LANES], s[:, nb - LANES:] + bias], axis=1)
    m = jnp.max(s, axis=-1, keepdims=True)
    e = jnp.exp2(s - m)
    p = e * (1.0 / jnp.sum(e, axis=-1, keepdims=True))
    p = jnp.where(s > 0.5 * NEG_INF, p, 0.0)
    o = jnp.dot(p.astype(BF16), vc, preferred_element_type=F32)
    psum = p[0:tq]
    for r in range(NSA_GROUP):
        o_ref[:, r * d:(r + 1) * d] = o[r * tq:(r + 1) * tq].astype(o_ref.dtype)
        if r:
            psum = psum + p[r * tq:(r + 1) * tq]
    p_hi = psum.astype(BF16)
    p_lo = (psum - p_hi.astype(F32)).astype(BF16)
    imp2 = lax.dot_general(ms_ref[pl.ds(w0, nb), :], jnp.concatenate([p_hi, p_lo], axis=0),
                           (((0,), (1,)), ((), ())), preferred_element_type=F32)
    imp = imp2[:, :tq] + imp2[:, tq:]
    blk = lax.broadcasted_iota(jnp.int32, (n_s, tq), 0)
    pos = i * tq + lax.broadcasted_iota(jnp.int32, (n_s, tq), 1)
    cur = lax.shift_right_logical(pos, int(math.log2(SEL_BLOCK)))
    forced = jnp.where(blk == 0, 1, jnp.where(blk <= cur, jnp.where(blk > cur - SEL_LOCAL, 1, 0), 0))
    score = jnp.where(forced == 1, FORCE_SCORE, jnp.where(blk <= cur, imp, -FORCE_SCORE))
    score_ref[...] = score
    per_trip = tq // SEL_BLOCK

    def count_ahead(g, rank):
        for r in range(per_trip):
            j = g * per_trip + r
            rj = score_ref[pl.ds(j, 1), :]
            tie = jnp.where(rj == score, jnp.where(blk > j, 1, 0), 0)
            rank = rank + jnp.where(rj > score, 1, tie)
        return rank

    rank = lax.fori_loop(0, i + 1, count_ahead, jnp.zeros((n_s, tq), jnp.int32))
    selb = jnp.where(rank < top_n, 0.0, NEG_INF)
    if n_s < LANES:
        selb = jnp.concatenate([selb, jnp.zeros((LANES - n_s, tq), F32)], axis=0)
    sel_ref[0, 0] = selb.T.astype(sel_ref.dtype)


def _cmp_select(u, kc, vc, pc, ms, batch, seq, tq):
    t = batch * seq
    nb = seq // CMP_STRIDE
    n_s = seq // SEL_BLOCK
    top_n = min(SEL_TOPK, n_s)
    nq = seq // tq
    qw = NSA_GROUP * NSA_DIM
    kv_spec = pl.BlockSpec((1, 1, 2 * nb, NSA_DIM), lambda b, g, i: (b, g, 0, 0))
    return pl.pallas_call(
        functools.partial(_cmp_select_kernel, tq=tq, nb=nb, n_s=n_s, top_n=top_n,
                          front=_cmp_front_pad(seq, tq)),
        grid=(batch, NSA_KV_HEADS, nq),
        in_specs=[pl.BlockSpec((tq, qw), lambda b, g, i: (b * nq + i, U_NQ // qw + g)),
                  kv_spec, kv_spec,
                  pl.BlockSpec((NSA_GROUP, tq, LANES), lambda b, g, i: (g, 0, 0)),
                  pl.BlockSpec((2 * nb, n_s), lambda b, g, i: (0, 0))],
        out_specs=[pl.BlockSpec((tq, qw), lambda b, g, i: (b * nq + i, g)),
                   pl.BlockSpec((1, 1, tq, LANES), lambda b, g, i: (b, g, i, 0))],
        out_shape=[jax.ShapeDtypeStruct((t, NSA_Q_W), BF16),
                   jax.ShapeDtypeStruct((batch, NSA_KV_HEADS, seq, LANES), BF16)],
        scratch_shapes=[pltpu.VMEM((n_s, tq), F32)],
        compiler_params=_params(("parallel", "parallel", "arbitrary")),
        name="nsa_cmp_select",
    )(u, kc, vc, pc, ms)


def _flash_kernel(*refs, mode, tq, tk):
    sel = mode == "sel"
    if sel:
        (q_ref, k_ref, v_ref, sel_ref, pw_ref, o_ref,
         qs_ref, ks_ref, m_ref, l_ref, acc_ref) = refs
    else:
        q_ref, k_ref, v_ref, pw_ref, o_ref, vs_ref = refs
    i = pl.program_id(2)
    d = NSA_DIM

    if sel:
        @pl.when(i == 0)
        def _():
            seq = k_ref.shape[0]
            ks_ref[:, :d] = k_ref[...]
            krow = lax.broadcasted_iota(jnp.int32, (seq, LANES), 0)
            klane = lax.broadcasted_iota(jnp.int32, (seq, LANES), 1)
            kblk = lax.shift_right_logical(krow, int(math.log2(SEL_BLOCK)))
            ks_ref[:, d:] = jnp.where(kblk == klane, 1.0, 0.0).astype(ks_ref.dtype)

        for r in range(NSA_GROUP):
            qs_ref[r * tq:(r + 1) * tq, :d] = q_ref[:, r * d:(r + 1) * d]
            qs_ref[r * tq:(r + 1) * tq, d:] = sel_ref[0, 0]
        m_ref[...] = jnp.full_like(m_ref, NEG_INF)
        l_ref[...] = jnp.zeros_like(l_ref)
        acc_ref[...] = jnp.zeros_like(acc_ref)

    if not sel:
        @pl.when(i == 0)
        def _():
            vs_ref[:, :d] = v_ref[...]
            vs_ref[:, d:] = jnp.ones((v_ref.shape[0], d), vs_ref.dtype)

    chains = [(0, NSA_GROUP)] if sel else [(r, r + 1) for r in range(NSA_GROUP)]

    def chain_logits(h0, h1, kt):
        q = qs_ref[h0 * tq:h1 * tq, :] if sel else q_ref[:, h0 * d:h1 * d]
        return lax.dot_general(q, kt, NT, preferred_element_type=F32)

    def online_update(rows, s, vt):
        m_prev = m_ref[rows]
        m_new = jnp.maximum(m_prev, jnp.max(s, axis=-1, keepdims=True))
        alpha = jnp.exp2(m_prev - m_new)
        p = jnp.exp2(s - jnp.concatenate([m_new] * (s.shape[1] // LANES), axis=1))
        l_new = alpha * l_ref[rows] + jnp.sum(p, axis=-1, keepdims=True)
        acc_new = alpha * acc_ref[rows] + jnp.dot(p.astype(BF16), vt,
                                                  preferred_element_type=F32)
        return m_new, l_new, acc_new

    def last_tiles(n):
        width = n * tk
        k0 = pl.multiple_of((i + 1 - n) * tk, tk)
        kt = ks_ref[pl.ds(k0, width), :] if sel else k_ref[pl.ds(k0, width), :]
        vt = v_ref[pl.ds(k0, width), :] if sel else vs_ref[pl.ds(k0, width), :]
        for h0, h1 in chains:
            rows = slice(h0 * tq, h1 * tq)
            s = chain_logits(h0, h1, kt) + pw_ref[0, rows, (3 - n) * tk:]
            if sel:
                _, l, acc = online_update(rows, s, vt)
            else:
                p = jnp.exp2(s - jnp.max(s, axis=-1, keepdims=True))
                acc = jnp.dot(p.astype(BF16), vt, preferred_element_type=F32)
                acc, l = acc[:, :d], acc[:, d:]
            out = acc * (1.0 / l)
            for h in range(h0, h1):
                o_ref[:, h * d:(h + 1) * d] = out[(h - h0) * tq:(h - h0 + 1) * tq, :].astype(
                    o_ref.dtype)

    if sel:
        n_far = jnp.maximum(i - 1, 0)
        odd = n_far % 2

        def far_pair(j):
            k0 = pl.multiple_of(j * (2 * tk), 2 * tk)
            kt = ks_ref[pl.ds(k0, 2 * tk), :]
            vt = v_ref[pl.ds(k0, 2 * tk), :]
            for h0, h1 in chains:
                rows = slice(h0 * tq, h1 * tq)
                m_ref[rows], l_ref[rows], acc_ref[rows] = online_update(
                    rows, chain_logits(h0, h1, kt), vt)

        n_pairs = n_far // 2

        def far(j, carry):
            far_pair(2 * j)
            far_pair(2 * j + 1)
            return carry
        lax.fori_loop(0, n_pairs // 2, far, 0)
        pl.when(n_pairs % 2 == 1)(lambda: far_pair(n_pairs - 1))
        pl.when(i == 0)(lambda: last_tiles(1))
        pl.when(jnp.logical_and(i >= 1, odd == 0))(lambda: last_tiles(2))
        pl.when(odd == 1)(lambda: last_tiles(3))
    else:
        pl.when(i == 0)(lambda: last_tiles(1))
        pl.when(i == 1)(lambda: last_tiles(2))
        pl.when(i >= 2)(lambda: last_tiles(3))


def _flash(u, k_blk, v_blk, pw, batch, seq, mode, sel=None, tq=ATT_TILE, tk=ATT_TILE):
    assert tq == tk and 2 * tk >= REL_MAX_DIST and 3 * tk > WINDOW >= 2 * tk
    t = batch * seq
    nq = seq // tq
    qw = NSA_GROUP * NSA_DIM
    rows = NSA_GROUP * tq
    d = NSA_DIM
    q_spec = pl.BlockSpec((tq, qw), lambda b, g, i: (b * nq + i, U_NQ // qw + g))
    k_spec = pl.BlockSpec((seq, d), lambda b, g, i: (b, k_blk + g))
    v_spec = pl.BlockSpec((seq, d), lambda b, g, i: (b, v_blk + g))
    b_spec = pl.BlockSpec((1, rows, 3 * tk), lambda b, g, i: (g, 0, 0))
    if mode == "sel":
        in_specs = [q_spec, k_spec, v_spec,
                    pl.BlockSpec((1, 1, tq, LANES), lambda b, g, i: (b, g, i, 0)), b_spec]
        args = (u, u, u, sel, pw)
        scratch = ([pltpu.VMEM((rows, 2 * d), BF16), pltpu.VMEM((seq, 2 * d), BF16)]
                   + [pltpu.VMEM((rows, LANES), F32)] * 3)
    else:
        in_specs = [q_spec, k_spec, v_spec, b_spec]
        args = (u, u, u, pw)
        scratch = [pltpu.VMEM((seq, 2 * d), BF16)]
    return pl.pallas_call(
        functools.partial(_flash_kernel, mode=mode, tq=tq, tk=tk),
        grid=(batch, NSA_KV_HEADS, nq),
        in_specs=in_specs,
        out_specs=pl.BlockSpec((tq, qw), lambda b, g, i: (b * nq + i, g)),
        out_shape=jax.ShapeDtypeStruct((t, NSA_Q_W), BF16),
        scratch_shapes=scratch,
        compiler_params=_params(("parallel", "parallel", "arbitrary")),
        name="nsa_flash_" + mode,
    )(*args)


def _combine_kernel(oc_ref, os_ref, ow_ref, ng_ref, o_ref):
    gates = jax.nn.sigmoid(ng_ref[...])
    d = NSA_DIM
    for h in range(NSA_HEADS):
        c0 = GLA_GATE_RANK + 3 * h
        sl = slice(h * d, (h + 1) * d)
        o = (gates[:, c0:c0 + 1] * oc_ref[:, sl].astype(F32)
             + gates[:, c0 + 1:c0 + 2] * os_ref[:, sl].astype(F32)
             + gates[:, c0 + 2:c0 + 3] * ow_ref[:, sl].astype(F32))
        o_ref[:, sl] = o.astype(o_ref.dtype)


def _combine(o_cmp, o_sel, o_win, small, tm=512):
    t, w = o_cmp.shape
    spec = pl.BlockSpec((tm, w), lambda i: (i, 0))
    return pl.pallas_call(
        _combine_kernel,
        grid=(t // tm,),
        in_specs=[spec, spec, spec, pl.BlockSpec((tm, LANES), lambda i: (i, 0))],
        out_specs=spec,
        out_shape=jax.ShapeDtypeStruct((t, w), BF16),
        compiler_params=_params(("parallel",)),
        name="nsa_combine",
    )(o_cmp, o_sel, o_win, small)


def _rel_bucket(dist):
    n = jnp.maximum(dist, 0)
    max_exact = REL_BUCKETS // 2
    nf = jnp.maximum(n, max_exact).astype(F32)
    large = max_exact + (jnp.log(nf / max_exact) / math.log(REL_MAX_DIST / max_exact)
                         * (REL_BUCKETS - max_exact)).astype(jnp.int32)
    large = jnp.minimum(large, REL_BUCKETS - 1)
    return jnp.where(n < max_exact, n, large)


def _bias_by_distance(table, dist):
    onehot = (_rel_bucket(jnp.asarray(dist, jnp.int32))[..., None]
              == jnp.arange(REL_BUCKETS, dtype=jnp.int32)).astype(F32)
    return jnp.dot(onehot, table, precision=HIGHEST)


def _bias_tables(rel_table, seq, tq, tk):
    table = rel_table.astype(F32) * LOG2E
    heads = table.shape[1]
    far = table[REL_BUCKETS - 1]
    period = 3 * tk + 1
    vec = _bias_by_distance(table, np.arange(2 * tk)) - far
    vec = jnp.concatenate([vec, jnp.full((period - 2 * tk, heads), NEG_INF, F32)], axis=0).T
    skew = jnp.tile(vec, (1, tk))[:, :tk * (period - 1)].reshape(heads, tk, period - 1)
    tiles = skew[:, :, :2 * tq].transpose(0, 2, 1)
    p0 = tiles[:, :tq].reshape(NSA_KV_HEADS, NSA_GROUP * tq, tk)
    p1 = tiles[:, tq:].reshape(NSA_KV_HEADS, NSA_GROUP * tq, tk)
    a2 = (np.arange(NSA_GROUP * tq) % tq)[:, None]
    edge = np.where(2 * tk + a2 - np.arange(tk)[None, :] < WINDOW, 0.0, NEG_INF).astype(np.float32)
    edge = jnp.broadcast_to(jnp.asarray(edge), p0.shape)
    pw_sel = jnp.concatenate([jnp.zeros_like(p0), p1, p0], axis=2)
    pw_win = jnp.concatenate([edge, p1, p0], axis=2)
    front = _cmp_front_pad(seq, tq)
    nb = seq // CMP_STRIDE
    a = np.arange(tq)[:, None]
    rel_blk = np.arange(nb - LANES, nb)[None, :] - front
    dc = a - CMP_STRIDE * rel_blk - (CMP_BLOCK - 1)
    assert (a - CMP_STRIDE * (nb - LANES - 1 - front) - (CMP_BLOCK - 1)).min() >= REL_MAX_DIST
    pc = _bias_by_distance(table, np.maximum(dc, 0)) - far
    pc = jnp.where(jnp.asarray(dc >= 0)[..., None], pc, NEG_INF).transpose(2, 0, 1)
    return pw_sel, pw_win, pc


def _cmp_to_sel_matrix(seq, tq):
    nb = seq // CMP_STRIDE
    n_c = nb - 1
    n_s = seq // SEL_BLOCK
    front = _cmp_front_pad(seq, tq)
    m_mat = np.zeros((2 * nb, n_s), np.float32)
    j = np.arange(n_s)
    for m in range(SEL_BLOCK // CMP_STRIDE):
        for n in range(CMP_BLOCK // CMP_STRIDE):
            c = (SEL_BLOCK // CMP_STRIDE) * j + m - n
            ok = (c >= 0) & (c < n_c)
            np.add.at(m_mat, (front + c[ok], j[ok]), 1.0)
    return jnp.asarray(m_mat)


def _nsa(u, small, pos_k, pos_v, w1k, w2k, w1v, w2v, rel_table, batch, seq, u_kc):
    tq = ATT_TILE
    kc_blk = u_kc // NSA_DIM
    ksl_blk = kc_blk + 2 * NSA_KV_HEADS
    vsl_blk = kc_blk + 3 * NSA_KV_HEADS
    kw_blk = kc_blk + 4 * NSA_KV_HEADS
    vw_blk = kc_blk + 5 * NSA_KV_HEADS
    pw_sel, pw_win, pc = _bias_tables(rel_table, seq, tq, tq)
    ms = _cmp_to_sel_matrix(seq, tq).astype(BF16)
    kc, vc = _compress(u, pos_k.astype(F32), pos_v.astype(F32),
                       w1k.astype(BF16), w2k.astype(BF16),
                       w1v.astype(BF16), w2v.astype(BF16), batch, seq, u_kc, tq)
    o_cmp, sel = _cmp_select(u, kc, vc, pc, ms, batch, seq, tq)
    o_sel = _flash(u, ksl_blk, vsl_blk, pw_sel, batch, seq, "sel", sel=sel)
    o_win = _flash(u, kw_blk, vw_blk, pw_win, batch, seq, "win")
    return _combine(o_cmp, o_sel, o_win, small)


def kernel(x, g_mix_norm, w_in, w_alpha2, b_alpha, gla_norm_g, cmp_pos_k, cmp_pos_v,
           phi_k_w1, phi_k_w2, phi_v_w1, phi_v_w2, rel_bias_table, w_gla_proj,
           w_nsa_proj, w_out, g_mlp_norm, w_up, w_down, g_final_norm):
    batch, seq, d = x.shape
    t = batch * seq
    depth = w_in.shape[0]
    u_kc = U_MG + 2 * d
    xf = x.reshape(t, d)
    for l in range(depth):
        w_main_t, w_small_t = _w_in_prep(w_in, l, d)
        wa_pad = jnp.concatenate(
            [w_alpha2[l], jnp.zeros((LANES - GLA_GATE_RANK, GLA_KEY_W), w_alpha2.dtype)],
            axis=0).astype(F32)

        h, small = _rmsnorm_proj(xf, g_mix_norm[l], w_small_t)
        u = _matmul_nt(h, w_main_t, BF16, MM_TILE, MM_TILE, "in_proj")
        o_gla = _gla(u, small, wa_pad, b_alpha[l].reshape(1, -1).astype(F32),
                     gla_norm_g[l].reshape(1, -1).astype(F32), batch, seq)
        o_nsa = _nsa(u, small, cmp_pos_k[l], cmp_pos_v[l], phi_k_w1[l], phi_k_w2[l],
                     phi_v_w1[l], phi_v_w2[l], rel_bias_table, batch, seq, u_kc)
        mix = _mix(o_gla, w_gla_proj[l].astype(BF16), o_nsa, w_nsa_proj[l].astype(BF16), u, d)
        xf, xb, ssq = _out_proj(mix, w_out[l].astype(BF16), xf)
        w_up_g = (w_up[l] * g_mlp_norm[l].astype(F32)[:, None]).astype(BF16)
        act = _mlp_up(xb, w_up_g, ssq)
        xf = _matmul_residual(act, w_down[l].astype(BF16), xf, MM_TILE, MM_TILE, 4096, "mlp_down")
    out = _rmsnorm(xf, g_final_norm, F32)
    return out.reshape(batch, seq, d)
```

```python
import functools
import math

import numpy as np
import jax
import jax.numpy as jnp
from jax import lax
from jax.experimental import pallas as pl
from jax.experimental.pallas import tpu as pltpu

F32 = jnp.float32
BF16 = jnp.bfloat16
HIGHEST = lax.Precision.HIGHEST

NORM_EPS = 1e-6
GLA_HEADS = 16
GLA_DK = 64
GLA_DV = 128
GLA_KEY_W = GLA_HEADS * GLA_DK
GLA_VAL_W = GLA_HEADS * GLA_DV
GLA_GATE_RANK = 16
GLA_TAU = 16.0
GLA_CHUNK = 64
NSA_HEADS = 16
NSA_KV_HEADS = 4
NSA_GROUP = NSA_HEADS // NSA_KV_HEADS
NSA_DIM = 128
NSA_Q_W = NSA_HEADS * NSA_DIM
NSA_KV_W = NSA_KV_HEADS * NSA_DIM
CMP_BLOCK = 32
CMP_STRIDE = 16
SEL_BLOCK = 64
SEL_TOPK = 16
SEL_LOCAL = 2
WINDOW = 512
REL_BUCKETS = 32
REL_MAX_DIST = 128
NEG_INF = -1e30
FORCE_SCORE = 1e4
LOG2E = math.log2(math.e)

LANES = 128
VMEM_LIMIT = 56 * 1024 * 1024
VMEM_LIMIT_KTILED = 62 * 1024 * 1024
ATT_TILE = 256
MM_TILE = 1024
GLA_PRE_BLOCK = 256
GLA_UNROLL = 16

U_GQ = 0
U_GK = U_GQ + GLA_KEY_W
U_GV = U_GK + GLA_KEY_W
U_GR = U_GV + GLA_VAL_W
U_NQ = U_GR + GLA_VAL_W
U_MG = U_NQ + NSA_Q_W

NT = (((1,), (1,)), ((), ()))
TN = (((0,), (0,)), ((), ()))


def _params(sem, vmem_limit=VMEM_LIMIT):
    return pltpu.CompilerParams(dimension_semantics=sem, vmem_limit_bytes=vmem_limit)


def _rmsnorm_kernel(x_ref, g_ref, o_ref):
    x = x_ref[...].astype(F32)
    ms = jnp.mean(x * x, axis=-1, keepdims=True)
    o_ref[...] = (x * lax.rsqrt(ms + NORM_EPS) * g_ref[...]).astype(o_ref.dtype)


def _rmsnorm(x2, g, out_dtype, tm=256):
    t, d = x2.shape
    return pl.pallas_call(
        _rmsnorm_kernel,
        grid=(t // tm,),
        in_specs=[pl.BlockSpec((tm, d), lambda i: (i, 0)),
                  pl.BlockSpec((1, d), lambda i: (0, 0))],
        out_specs=pl.BlockSpec((tm, d), lambda i: (i, 0)),
        out_shape=jax.ShapeDtypeStruct((t, d), out_dtype),
        compiler_params=_params(("parallel",)),
        name="rmsnorm",
    )(x2, g.reshape(1, d).astype(F32))


def _rmsnorm_proj_kernel(x_ref, g_ref, wt_ref, o_ref, p_ref):
    x = x_ref[...].astype(F32)
    ms = jnp.mean(x * x, axis=-1, keepdims=True)
    h = (x * lax.rsqrt(ms + NORM_EPS) * g_ref[...]).astype(o_ref.dtype)
    o_ref[...] = h
    p_ref[...] = lax.dot_general(h, wt_ref[...], NT, preferred_element_type=F32)


def _rmsnorm_proj(x2, g, w_t, tm=256):
    t, d = x2.shape
    n = w_t.shape[0]
    return pl.pallas_call(
        _rmsnorm_proj_kernel,
        grid=(t // tm,),
        in_specs=[pl.BlockSpec((tm, d), lambda i: (i, 0)),
                  pl.BlockSpec((1, d), lambda i: (0, 0)),
                  pl.BlockSpec((n, d), lambda i: (0, 0))],
        out_specs=[pl.BlockSpec((tm, d), lambda i: (i, 0)),
                   pl.BlockSpec((tm, n), lambda i: (i, 0))],
        out_shape=[jax.ShapeDtypeStruct((t, d), BF16), jax.ShapeDtypeStruct((t, n), F32)],
        compiler_params=_params(("parallel",)),
        name="rmsnorm_proj",
    )(x2, g.reshape(1, d).astype(F32), w_t)


W_ROW_ALIGN = 16


def _w_in_prep_kernel(src_ref, w_ref, ga_ref, ng_ref, o_ref, os_ref, *, q_blocks, q_scale):
    del src_ref
    i = pl.program_id(0)
    is_q = jnp.logical_and(i >= q_blocks[0], i < q_blocks[1])
    scale = jnp.where(is_q, q_scale, 1.0).astype(F32)
    o_ref[...] = (w_ref[0] * scale).astype(o_ref.dtype)

    @pl.when(i == 0)
    def _():
        n_ga = ga_ref.shape[1]
        n_ng = ng_ref.shape[1]
        os_ref[0:n_ga] = ga_ref[0].astype(os_ref.dtype)
        os_ref[n_ga:n_ga + n_ng] = ng_ref[0].astype(os_ref.dtype)
        os_ref[n_ga + n_ng:] = jnp.zeros((os_ref.shape[0] - n_ga - n_ng, os_ref.shape[1]),
                                         os_ref.dtype)


def _w_in_prep(w_in, layer, d, tr=512):
    n = w_in.shape[2]
    s_ga = 2 * GLA_KEY_W + GLA_VAL_W
    s_gr = s_ga + GLA_GATE_RANK
    s_nq = s_gr + GLA_VAL_W
    s_kc = s_nq + NSA_Q_W
    s_ng = s_kc + 6 * NSA_KV_W
    s_mg = s_ng + 3 * NSA_HEADS
    assert n == s_mg + 2 * d
    w_t = jnp.swapaxes(w_in, 1, 2)
    pieces = [(0, s_ga, U_GQ), (s_gr, s_nq, U_GR), (s_nq, s_kc, U_NQ),
              (s_mg, n, U_MG), (s_kc, s_ng, U_MG + 2 * d)]
    src_rows = []
    for s0, s1, d0 in pieces:
        assert s0 % W_ROW_ALIGN == 0 and (s1 - s0) % tr == 0 and d0 == len(src_rows) * tr
        src_rows += [r // W_ROW_ALIGN for r in range(s0, s1, tr)]
    n_main = len(src_rows) * tr
    def rows_at(start, count):
        return pl.BlockSpec((pl.Element(1), pl.Element(count), pl.Element(d)),
                            lambda i, src: (layer, start, 0))

    return pl.pallas_call(
        functools.partial(_w_in_prep_kernel,
                          q_blocks=(U_NQ // tr, (U_NQ + NSA_Q_W) // tr),
                          q_scale=NSA_DIM ** -0.5 * LOG2E),
        grid_spec=pltpu.PrefetchScalarGridSpec(
            num_scalar_prefetch=1,
            grid=(n_main // tr,),
            in_specs=[pl.BlockSpec(
                (pl.Element(1), pl.Element(tr), pl.Element(d)),
                lambda i, src: (layer, src[i] * W_ROW_ALIGN, 0)),
                rows_at(s_ga, s_gr - s_ga), rows_at(s_ng, s_mg - s_ng)],
            out_specs=[pl.BlockSpec((tr, d), lambda i, src: (i, 0)),
                       pl.BlockSpec((LANES, d), lambda i, src: (0, 0))]),
        out_shape=[jax.ShapeDtypeStruct((n_main, d), BF16),
                   jax.ShapeDtypeStruct((LANES, d), BF16)],
        compiler_params=_params(("arbitrary",)),
        name="w_in_prep",
    )(jnp.asarray(src_rows, jnp.int32), w_t, w_t, w_t)


def _mm_nt_kernel(a_ref, bt_ref, o_ref):
    o_ref[...] = lax.dot_general(a_ref[...], bt_ref[...], NT,
                                 preferred_element_type=F32).astype(o_ref.dtype)


def _matmul_nt(a, b_t, out_dtype, tm, tn, name):
    m, k = a.shape
    n = b_t.shape[0]
    return pl.pallas_call(
        _mm_nt_kernel,
        grid=(m // tm, n // tn),
        in_specs=[pl.BlockSpec((tm, k), lambda i, j: (i, 0)),
                  pl.BlockSpec((tn, k), lambda i, j: (j, 0))],
        out_specs=pl.BlockSpec((tm, tn), lambda i, j: (i, j)),
        out_shape=jax.ShapeDtypeStruct((m, n), out_dtype),
        compiler_params=_params(("parallel", "arbitrary")),
        name=name,
    )(a, b_t)


def _mix_kernel(og_ref, wg_ref, on_ref, wn_ref, mg_ref, mn_ref, o_ref):
    yg = jnp.dot(og_ref[...], wg_ref[...], preferred_element_type=F32)
    yn = jnp.dot(on_ref[...], wn_ref[...], preferred_element_type=F32)
    o = (jax.nn.sigmoid(mg_ref[...].astype(F32)) * yg
         + jax.nn.sigmoid(mn_ref[...].astype(F32)) * yn)
    o_ref[...] = o.astype(o_ref.dtype)


def _mix(o_gla, w_g, o_nsa, w_n, u, d, tm=MM_TILE, tn=MM_TILE):
    t = o_gla.shape[0]
    mg_blk = U_MG // tn
    mn_blk = (U_MG + d) // tn
    return pl.pallas_call(
        _mix_kernel,
        grid=(t // tm, d // tn),
        in_specs=[pl.BlockSpec((tm, o_gla.shape[1]), lambda i, j: (i, 0)),
                  pl.BlockSpec((w_g.shape[0], tn), lambda i, j: (0, j)),
                  pl.BlockSpec((tm, o_nsa.shape[1]), lambda i, j: (i, 0)),
                  pl.BlockSpec((w_n.shape[0], tn), lambda i, j: (0, j)),
                  pl.BlockSpec((tm, tn), lambda i, j: (i, mg_blk + j)),
                  pl.BlockSpec((tm, tn), lambda i, j: (i, mn_blk + j))],
        out_specs=pl.BlockSpec((tm, tn), lambda i, j: (i, j)),
        out_shape=jax.ShapeDtypeStruct((t, d), BF16),
        compiler_params=_params(("parallel", "arbitrary")),
        name="mix",
    )(o_gla, w_g, o_nsa, w_n, u, u)


def _mm_res_kernel(a_ref, b_ref, r_ref, o_ref, acc_ref):
    kk = pl.program_id(2)

    @pl.when(kk == 0)
    def _():
        acc_ref[...] = r_ref[...]

    acc_ref[...] += jnp.dot(a_ref[...], b_ref[...], preferred_element_type=F32)

    @pl.when(kk == pl.num_programs(2) - 1)
    def _():
        o_ref[...] = acc_ref[...]


def _out_proj_kernel(a_ref, b_ref, r_ref, o_ref, ob_ref, ss_ref):
    y = r_ref[...] + jnp.dot(a_ref[...], b_ref[...], preferred_element_type=F32)
    o_ref[...] = y
    ob_ref[...] = y.astype(ob_ref.dtype)
    ss_ref[...] = jnp.broadcast_to(jnp.sum(y * y, axis=-1, keepdims=True), ss_ref.shape)


def _out_proj(a, b, res, tm=MM_TILE, tn=MM_TILE):
    m, k = a.shape
    _, n = b.shape
    return pl.pallas_call(
        _out_proj_kernel,
        grid=(m // tm, n // tn),
        in_specs=[pl.BlockSpec((tm, k), lambda i, j: (i, 0)),
                  pl.BlockSpec((k, tn), lambda i, j: (0, j)),
                  pl.BlockSpec((tm, tn), lambda i, j: (i, j))],
        out_specs=[pl.BlockSpec((tm, tn), lambda i, j: (i, j)),
                   pl.BlockSpec((tm, tn), lambda i, j: (i, j)),
                   pl.BlockSpec((tm, LANES), lambda i, j: (i, j))],
        out_shape=[jax.ShapeDtypeStruct((m, n), F32), jax.ShapeDtypeStruct((m, n), BF16),
                   jax.ShapeDtypeStruct((m, n // tn * LANES), F32)],
        compiler_params=_params(("parallel", "arbitrary"), VMEM_LIMIT_KTILED),
        name="out_proj",
    )(a, b, res)


def _mlp_up_kernel(a_ref, b_ref, ss_ref, o_ref, *, d):
    acc = jnp.dot(a_ref[...], b_ref[...], preferred_element_type=F32)
    ss = ss_ref[...]
    total = ss[:, :LANES]
    for p in range(1, ss.shape[1] // LANES):
        total = total + ss[:, p * LANES:(p + 1) * LANES]
    inv = 1.0 / (total[:, :1] * (1.0 / d) + NORM_EPS)
    r = jnp.maximum(acc, 0.0)
    o_ref[...] = (r * r * inv).astype(o_ref.dtype)


def _mlp_up(xb, w_scaled, ssq, tm=MM_TILE, tn=MM_TILE):
    m, k = xb.shape
    _, n = w_scaled.shape
    return pl.pallas_call(
        functools.partial(_mlp_up_kernel, d=k),
        grid=(m // tm, n // tn),
        in_specs=[pl.BlockSpec((tm, k), lambda i, j: (i, 0)),
                  pl.BlockSpec((k, tn), lambda i, j: (0, j)),
                  pl.BlockSpec((tm, ssq.shape[1]), lambda i, j: (i, 0))],
        out_specs=pl.BlockSpec((tm, tn), lambda i, j: (i, j)),
        out_shape=jax.ShapeDtypeStruct((m, n), BF16),
        compiler_params=_params(("parallel", "arbitrary")),
        name="mlp_up",
    )(xb, w_scaled, ssq)


def _matmul_residual(a, b, res, tm, tn, tk, name):
    m, k = a.shape
    _, n = b.shape
    return pl.pallas_call(
        _mm_res_kernel,
        grid=(m // tm, n // tn, k // tk),
        in_specs=[pl.BlockSpec((tm, tk), lambda i, j, kk: (i, kk)),
                  pl.BlockSpec((tk, tn), lambda i, j, kk: (kk, j)),
                  pl.BlockSpec((tm, tn), lambda i, j, kk: (i, j))],
        out_specs=pl.BlockSpec((tm, tn), lambda i, j, kk: (i, j)),
        out_shape=jax.ShapeDtypeStruct((m, n), F32),
        scratch_shapes=[pltpu.VMEM((tm, tn), F32)],
        compiler_params=_params(("parallel", "arbitrary", "arbitrary"), VMEM_LIMIT_KTILED),
        name=name,
    )(a, b, res)


def _gla_kernel(q_ref, k_ref, v_ref, r_ref, a_ref, wa_ref, ba_ref, g_ref, o_ref,
                bc_ref, st_ref, *, seq):
    c = GLA_CHUNK
    dv = GLA_DV
    shift = int(math.log2(c))

    nb = GLA_PRE_BLOCK
    prow = lax.broadcasted_iota(jnp.int32, (nb, nb), 0)
    pcol = lax.broadcasted_iota(jnp.int32, (nb, nb), 1)
    same_chunk = lax.shift_right_logical(prow, shift) == lax.shift_right_logical(pcol, shift)
    tri = jnp.where(prow >= pcol, jnp.where(same_chunk, 1.0, 0.0), 0.0).astype(BF16)

    def split(x):
        hi = x.astype(BF16)
        return hi, (x - hi.astype(F32)).astype(BF16)

    wa_hi, wa_lo = split(wa_ref[...])
    wa3 = jnp.concatenate([wa_hi, wa_hi, wa_lo], axis=0)

    def cumulate(b, carry):
        r0 = pl.multiple_of(b * nb, nb)
        a_hi, a_lo = split(a_ref[pl.ds(r0, nb), :])
        z = jnp.dot(jnp.concatenate([a_hi, a_lo, a_hi], axis=1), wa3,
                    preferred_element_type=F32) + ba_ref[...]
        la = (jnp.minimum(z, 0.0) - jnp.log(1.0 + jnp.exp(-jnp.abs(z)))) * (1.0 / GLA_TAU)
        cs = jnp.dot(tri, jnp.concatenate(split(la), axis=1), preferred_element_type=F32)
        w = la.shape[1]
        bc_ref[pl.ds(r0, nb), :] = cs[:, :w] + cs[:, w:]
        return carry

    lax.fori_loop(0, seq // nb, cumulate, 0, unroll=8)
    st_ref[...] = jnp.zeros_like(st_ref)

    row2 = lax.broadcasted_iota(jnp.int32, (2 * c, c), 0)
    col2 = lax.broadcasted_iota(jnp.int32, (2 * c, c), 1)
    causal2 = (row2 & (c - 1)) >= col2
    lane = lax.broadcasted_iota(jnp.int32, (1, 2 * GLA_DK), 1)
    head_mask = [(lane < GLA_DK).astype(F32), (lane >= GLA_DK).astype(F32)]
    gvec = g_ref[...]

    def body(i, carry):
        r0 = pl.multiple_of(i * c, c)
        bc = bc_ref[pl.ds(r0, c), :]
        last = bc[c - 1:c, :]
        q = q_ref[pl.ds(r0, c), :].astype(F32) * (GLA_DK ** -0.5)
        k = k_ref[pl.ds(r0, c), :].astype(F32)
        qe = q * jnp.exp(bc)
        ke = (k * jnp.exp(-bc)).astype(BF16)
        kd = (k * jnp.exp(last - bc)).astype(BF16)
        v = v_ref[pl.ds(r0, c), :]
        st = st_ref[...]
        q2 = jnp.concatenate([qe * head_mask[0], qe * head_mask[1]], axis=0).astype(BF16)
        att = lax.dot_general(q2, ke, NT, preferred_element_type=F32)
        att = jnp.where(causal2, att, 0.0).astype(BF16)
        o2 = (jnp.dot(att, v, preferred_element_type=F32)
              + lax.dot_general(q2, st.astype(BF16), NT, preferred_element_type=F32))
        outs = []
        for h in range(2):
            oh = o2[h * c:(h + 1) * c, h * dv:(h + 1) * dv]
            ms = jnp.mean(oh * oh, axis=-1, keepdims=True)
            y = oh * lax.rsqrt(ms + NORM_EPS) * gvec[:, h * dv:(h + 1) * dv]
            rr = r_ref[pl.ds(r0, c), h * dv:(h + 1) * dv].astype(F32)
            outs.append(y * (rr * jax.nn.sigmoid(rr)))
        o_ref[pl.ds(r0, c), :] = jnp.concatenate(outs, axis=1).astype(o_ref.dtype)
        kv = lax.dot_general(v, kd, TN, preferred_element_type=F32)
        st_ref[...] = st * jnp.exp(last) + kv
        return carry

    lax.fori_loop(0, seq // c, body, 0, unroll=GLA_UNROLL)


def _gla(u, small, w_alpha_pad, b_alpha, norm_g, batch, seq):
    t = batch * seq
    pairs = GLA_HEADS // 2
    kw = 2 * GLA_DK
    vw = 2 * GLA_DV
    return pl.pallas_call(
        functools.partial(_gla_kernel, seq=seq),
        grid=(batch, pairs),
        in_specs=[pl.BlockSpec((seq, kw), lambda b, j: (b, U_GQ // kw + j)),
                  pl.BlockSpec((seq, kw), lambda b, j: (b, U_GK // kw + j)),
                  pl.BlockSpec((seq, vw), lambda b, j: (b, U_GV // vw + j)),
                  pl.BlockSpec((seq, vw), lambda b, j: (b, U_GR // vw + j)),
                  pl.BlockSpec((seq, LANES), lambda b, j: (b, 0)),
                  pl.BlockSpec((LANES, kw), lambda b, j: (0, j)),
                  pl.BlockSpec((1, kw), lambda b, j: (0, j)),
                  pl.BlockSpec((1, vw), lambda b, j: (0, j))],
        out_specs=pl.BlockSpec((seq, vw), lambda b, j: (b, j)),
        out_shape=jax.ShapeDtypeStruct((t, GLA_VAL_W), BF16),
        scratch_shapes=[pltpu.VMEM((seq, kw), F32), pltpu.VMEM((vw, kw), F32)],
        compiler_params=_params(("parallel", "arbitrary")),
        name="gla",
    )(u, u, u, u, small, w_alpha_pad, b_alpha, norm_g)


def _cmp_cols_per_tile(tq):
    return tq // CMP_STRIDE


def _cmp_front_pad(seq, tq):
    return seq // CMP_STRIDE - _cmp_cols_per_tile(tq)


def _compress_kernel(xk_ref, xv_ref, pk_ref, pv_ref, w1k_ref, w2k_ref, w1v_ref, w2v_ref,
                     ok_ref, ov_ref, xf_ref, *, front):
    n = CMP_STRIDE
    d = NSA_DIM

    def one(x_ref, pos_ref, w1_ref, w2_ref, o_ref):
        nb = x_ref.shape[0] // n
        xf_ref[...] = x_ref[...].astype(F32)
        top = jnp.zeros((nb, w1_ref.shape[1]), F32)
        bot = jnp.zeros((nb, w1_ref.shape[1]), F32)
        for j in range(n):
            x = xf_ref[pl.ds(j, nb, stride=n), :]
            xa = (x + pos_ref[j:j + 1, :]).astype(BF16)
            xb = (x + pos_ref[n + j:n + j + 1, :]).astype(BF16)
            top = top + jnp.dot(xa, w1_ref[j * d:(j + 1) * d, :],
                                preferred_element_type=F32)
            bot = bot + jnp.dot(xb, w1_ref[(n + j) * d:(n + j + 1) * d, :],
                                preferred_element_type=F32)
        hid = top + pltpu.roll(bot, nb - 1, axis=0)
        hid = jnp.maximum(hid, 0.0).astype(BF16)
        total = o_ref.shape[2]
        o_ref[0, 0, 0:front] = jnp.zeros((front, d), o_ref.dtype)
        o_ref[0, 0, front:front + nb] = jnp.dot(
            hid, w2_ref[...], preferred_element_type=F32).astype(o_ref.dtype)
        o_ref[0, 0, front + nb:total] = jnp.zeros((total - front - nb, d), o_ref.dtype)

    one(xk_ref, pk_ref, w1k_ref, w2k_ref, ok_ref)
    one(xv_ref, pv_ref, w1v_ref, w2v_ref, ov_ref)


def _compress(u, pos_k, pos_v, w1k, w2k, w1v, w2v, batch, seq, u_kc, tq):
    nb = seq // CMP_STRIDE
    front = _cmp_front_pad(seq, tq)
    kc_blk = u_kc // NSA_DIM
    vc_blk = kc_blk + NSA_KV_HEADS
    full = lambda a: pl.BlockSpec(a.shape, lambda b, g: (0,) * a.ndim)
    out = jax.ShapeDtypeStruct((batch, NSA_KV_HEADS, 2 * nb, NSA_DIM), BF16)
    ospec = pl.BlockSpec((1, 1, 2 * nb, NSA_DIM), lambda b, g: (b, g, 0, 0))
    return pl.pallas_call(
        functools.partial(_compress_kernel, front=front),
        grid=(batch, NSA_KV_HEADS),
        in_specs=[pl.BlockSpec((seq, NSA_DIM), lambda b, g: (b, kc_blk + g)),
                  pl.BlockSpec((seq, NSA_DIM), lambda b, g: (b, vc_blk + g)),
                  full(pos_k), full(pos_v), full(w1k), full(w2k), full(w1v), full(w2v)],
        out_specs=[ospec, ospec],
        out_shape=[out, out],
        scratch_shapes=[pltpu.VMEM((seq, NSA_DIM), F32)],
        compiler_params=_params(("parallel", "arbitrary")),
        name="nsa_compress",
    )(u, u, pos_k, pos_v, w1k, w2k, w1v, w2v)


def _cmp_select_kernel(q_ref, kc_ref, vc_ref, pc_ref, ms_ref, o_ref, sel_ref, score_ref, *,
                       tq, nb, n_s, top_n, front):
    i = pl.program_id(2)
    d = NSA_DIM
    w0 = pl.multiple_of(i * _cmp_cols_per_tile(tq), _cmp_cols_per_tile(tq))
    kc = kc_ref[0, 0, pl.ds(w0, nb), :]
    vc = vc_ref[0, 0, pl.ds(w0, nb), :]
    rows = NSA_GROUP * tq
    q = jnp.concatenate([q_ref[:, r * d:(r + 1) * d] for r in range(NSA_GROUP)], axis=0)
    col = lax.broadcasted_iota(jnp.int32, (rows, nb), 1)
    before_start = jnp.where(col < front - w0, NEG_INF, 0.0)
    s = lax.dot_general(q, kc, NT, preferred_element_type=F32) + before_start
    bias = pc_ref[...].reshape(rows, LANES)
    s = jnp.concatenate([s[:, :nb - LANES], s[:, nb - LANES:] + bias], axis=1)
    m = jnp.max(s, axis=-1, keepdims=True)
    e = jnp.exp2(s - m)
    p = e * (1.0 / jnp.sum(e, axis=-1, keepdims=True))
    p = jnp.where(s > 0.5 * NEG_INF, p, 0.0)
    o = jnp.dot(p.astype(BF16), vc, preferred_element_type=F32)
    psum = p[0:tq]
    for r in range(NSA_GROUP):
        o_ref[:, r * d:(r + 1) * d] = o[r * tq:(r + 1) * tq].astype(o_ref.dtype)
        if r:
            psum = psum + p[r * tq:(r + 1) * tq]
    p_hi = psum.astype(BF16)
    p_lo = (psum - p_hi.astype(F32)).astype(BF16)
    imp2 = lax.dot_general(ms_ref[pl.ds(w0, nb), :], jnp.concatenate([p_hi, p_lo], axis=0),
                           (((0,), (1,)), ((), ())), preferred_element_type=F32)
    imp = imp2[:, :tq] + imp2[:, tq:]
    blk = lax.broadcasted_iota(jnp.int32, (n_s, tq), 0)
    pos = i * tq + lax.broadcasted_iota(jnp.int32, (n_s, tq), 1)
    cur = lax.shift_right_logical(pos, int(math.log2(SEL_BLOCK)))
    forced = jnp.where(blk == 0, 1, jnp.where(blk <= cur, jnp.where(blk > cur - SEL_LOCAL, 1, 0), 0))
    score = jnp.where(forced == 1, FORCE_SCORE, jnp.where(blk <= cur, imp, -FORCE_SCORE))
    score_ref[...] = score
    per_trip = tq // SEL_BLOCK

    def count_ahead(g, rank):
        for r in range(per_trip):
            j = g * per_trip + r
            rj = score_ref[pl.ds(j, 1), :]
            tie = jnp.where(rj == score, jnp.where(blk > j, 1, 0), 0)
            rank = rank + jnp.where(rj > score, 1, tie)
        return rank

    rank = lax.fori_loop(0, i + 1, count_ahead, jnp.zeros((n_s, tq), jnp.int32))
    selb = jnp.where(rank < top_n, 0.0, NEG_INF)
    if n_s < LANES:
        selb = jnp.concatenate([selb, jnp.zeros((LANES - n_s, tq), F32)], axis=0)
    sel_ref[0, 0] = selb.T.astype(sel_ref.dtype)


def _cmp_select(u, kc, vc, pc, ms, batch, seq, tq):
    t = batch * seq
    nb = seq // CMP_STRIDE
    n_s = seq // SEL_BLOCK
    top_n = min(SEL_TOPK, n_s)
    nq = seq // tq
    qw = NSA_GROUP * NSA_DIM
    kv_spec = pl.BlockSpec((1, 1, 2 * nb, NSA_DIM), lambda b, g, i: (b, g, 0, 0))
    return pl.pallas_call(
        functools.partial(_cmp_select_kernel, tq=tq, nb=nb, n_s=n_s, top_n=top_n,
                          front=_cmp_front_pad(seq, tq)),
        grid=(batch, NSA_KV_HEADS, nq),
        in_specs=[pl.BlockSpec((tq, qw), lambda b, g, i: (b * nq + i, U_NQ // qw + g)),
                  kv_spec, kv_spec,
                  pl.BlockSpec((NSA_GROUP, tq, LANES), lambda b, g, i: (g, 0, 0)),
                  pl.BlockSpec((2 * nb, n_s), lambda b, g, i: (0, 0))],
        out_specs=[pl.BlockSpec((tq, qw), lambda b, g, i: (b * nq + i, g)),
                   pl.BlockSpec((1, 1, tq, LANES), lambda b, g, i: (b, g, i, 0))],
        out_shape=[jax.ShapeDtypeStruct((t, NSA_Q_W), BF16),
                   jax.ShapeDtypeStruct((batch, NSA_KV_HEADS, seq, LANES), BF16)],
        scratch_shapes=[pltpu.VMEM((n_s, tq), F32)],
        compiler_params=_params(("parallel", "parallel", "arbitrary")),
        name="nsa_cmp_select",
    )(u, kc, vc, pc, ms)


def _flash_kernel(*refs, mode, tq, tk):
    sel = mode == "sel"
    if sel:
        (q_ref, k_ref, v_ref, sel_ref, pw_ref, o_ref,
         qs_ref, ks_ref, m_ref, l_ref, acc_ref) = refs
    else:
        (q_ref, k_ref, v_ref, pw_ref, oc_ref, os_ref, ng_ref, o_ref, vs_ref) = refs
    i = pl.program_id(2)
    d = NSA_DIM

    if sel:
        @pl.when(i == 0)
        def _():
            seq = k_ref.shape[0]
            ks_ref[:, :d] = k_ref[...]
            krow = lax.broadcasted_iota(jnp.int32, (seq, LANES), 0)
            klane = lax.broadcasted_iota(jnp.int32, (seq, LANES), 1)
            kblk = lax.shift_right_logical(krow, int(math.log2(SEL_BLOCK)))
            ks_ref[:, d:] = jnp.where(kblk == klane, 1.0, 0.0).astype(ks_ref.dtype)

        for r in range(NSA_GROUP):
            qs_ref[r * tq:(r + 1) * tq, :d] = q_ref[:, r * d:(r + 1) * d]
            qs_ref[r * tq:(r + 1) * tq, d:] = sel_ref[0, 0]
        m_ref[...] = jnp.full_like(m_ref, NEG_INF)
        l_ref[...] = jnp.zeros_like(l_ref)
        acc_ref[...] = jnp.zeros_like(acc_ref)

    if not sel:
        group = pl.program_id(1)
        per_group = 3 * NSA_GROUP
        shift = lax.rem(LANES - per_group * group, LANES)
        sig = pltpu.roll(jax.nn.sigmoid(ng_ref[...]), shift, axis=1)

        def gate(h, br):
            c = GLA_GATE_RANK + 3 * h + br
            return jnp.broadcast_to(sig[:, c:c + 1], (tq, d))

        @pl.when(i == 0)
        def _():
            vs_ref[:, :d] = v_ref[...]
            vs_ref[:, d:] = jnp.ones((v_ref.shape[0], d), vs_ref.dtype)

    chains = [(0, NSA_GROUP)] if sel else [(r, r + 1) for r in range(NSA_GROUP)]

    def chain_logits(h0, h1, kt):
        q = qs_ref[h0 * tq:h1 * tq, :] if sel else q_ref[:, h0 * d:h1 * d]
        return lax.dot_general(q, kt, NT, preferred_element_type=F32)

    def online_update(rows, s, vt):
        m_prev = m_ref[rows]
        m_new = jnp.maximum(m_prev, jnp.max(s, axis=-1, keepdims=True))
        alpha = jnp.exp2(m_prev - m_new)
        p = jnp.exp2(s - jnp.concatenate([m_new] * (s.shape[1] // LANES), axis=1))
        l_new = alpha * l_ref[rows] + jnp.sum(p, axis=-1, keepdims=True)
        acc_new = alpha * acc_ref[rows] + jnp.dot(p.astype(BF16), vt,
                                                  preferred_element_type=F32)
        return m_new, l_new, acc_new

    def last_tiles(n):
        width = n * tk
        k0 = pl.multiple_of((i + 1 - n) * tk, tk)
        kt = ks_ref[pl.ds(k0, width), :] if sel else k_ref[pl.ds(k0, width), :]
        vt = v_ref[pl.ds(k0, width), :] if sel else vs_ref[pl.ds(k0, width), :]
        for h0, h1 in chains:
            rows = slice(h0 * tq, h1 * tq)
            s = chain_logits(h0, h1, kt) + pw_ref[0, rows, (3 - n) * tk:]
            if sel:
                _, l, acc = online_update(rows, s, vt)
            else:
                p = jnp.exp2(s - jnp.max(s, axis=-1, keepdims=True))
                acc = jnp.dot(p.astype(BF16), vt, preferred_element_type=F32)
                acc, l = acc[:, :d], acc[:, d:]
            out = acc * (1.0 / l)
            for h in range(h0, h1):
                out_h = out[(h - h0) * tq:(h - h0 + 1) * tq, :]
                if not sel:
                    lanes = slice(h * d, (h + 1) * d)
                    out_h = (gate(h, 2) * out_h
                             + gate(h, 0) * oc_ref[:, lanes].astype(F32)
                             + gate(h, 1) * os_ref[:, lanes].astype(F32))
                o_ref[:, h * d:(h + 1) * d] = out_h.astype(o_ref.dtype)

    if sel:
        n_far = jnp.maximum(i - 1, 0)
        odd = n_far % 2

        def far_pair(j):
            k0 = pl.multiple_of(j * (2 * tk), 2 * tk)
            kt = ks_ref[pl.ds(k0, 2 * tk), :]
            vt = v_ref[pl.ds(k0, 2 * tk), :]
            for h0, h1 in chains:
                rows = slice(h0 * tq, h1 * tq)
                m_ref[rows], l_ref[rows], acc_ref[rows] = online_update(
                    rows, chain_logits(h0, h1, kt), vt)

        n_pairs = n_far // 2

        def far(j, carry):
            far_pair(2 * j)
            far_pair(2 * j + 1)
            return carry
        lax.fori_loop(0, n_pairs // 2, far, 0)
        pl.when(n_pairs % 2 == 1)(lambda: far_pair(n_pairs - 1))
        pl.when(i == 0)(lambda: last_tiles(1))
        pl.when(jnp.logical_and(i >= 1, odd == 0))(lambda: last_tiles(2))
        pl.when(odd == 1)(lambda: last_tiles(3))
    else:
        pl.when(i == 0)(lambda: last_tiles(1))
        pl.when(i == 1)(lambda: last_tiles(2))
        pl.when(i >= 2)(lambda: last_tiles(3))


def _flash(u, k_blk, v_blk, pw, batch, seq, mode, sel=None, merge=None,
           tq=ATT_TILE, tk=ATT_TILE):
    assert tq == tk and 2 * tk >= REL_MAX_DIST and 3 * tk > WINDOW >= 2 * tk
    t = batch * seq
    nq = seq // tq
    qw = NSA_GROUP * NSA_DIM
    rows = NSA_GROUP * tq
    d = NSA_DIM
    q_spec = pl.BlockSpec((tq, qw), lambda b, g, i: (b * nq + i, U_NQ // qw + g))
    k_spec = pl.BlockSpec((seq, d), lambda b, g, i: (b, k_blk + g))
    v_spec = pl.BlockSpec((seq, d), lambda b, g, i: (b, v_blk + g))
    b_spec = pl.BlockSpec((1, rows, 3 * tk), lambda b, g, i: (g, 0, 0))
    if mode == "sel":
        in_specs = [q_spec, k_spec, v_spec,
                    pl.BlockSpec((1, 1, tq, LANES), lambda b, g, i: (b, g, i, 0)), b_spec]
        args = (u, u, u, sel, pw)
        scratch = ([pltpu.VMEM((rows, 2 * d), BF16), pltpu.VMEM((seq, 2 * d), BF16)]
                   + [pltpu.VMEM((rows, LANES), F32)] * 3)
    else:
        o_spec = pl.BlockSpec((tq, qw), lambda b, g, i: (b * nq + i, g))
        o_cmp, o_sel, gate_logits = merge
        in_specs = [q_spec, k_spec, v_spec, b_spec, o_spec, o_spec,
                    pl.BlockSpec((tq, LANES), lambda b, g, i: (b * nq + i, 0))]
        args = (u, u, u, pw, o_cmp, o_sel, gate_logits)
        scratch = [pltpu.VMEM((seq, 2 * d), BF16)]
    return pl.pallas_call(
        functools.partial(_flash_kernel, mode=mode, tq=tq, tk=tk),
        grid=(batch, NSA_KV_HEADS, nq),
        in_specs=in_specs,
        out_specs=pl.BlockSpec((tq, qw), lambda b, g, i: (b * nq + i, g)),
        out_shape=jax.ShapeDtypeStruct((t, NSA_Q_W), BF16),
        scratch_shapes=scratch,
        compiler_params=_params(("parallel", "parallel", "arbitrary")),
        name="nsa_flash_" + mode,
    )(*args)


def _rel_bucket(dist):
    n = jnp.maximum(dist, 0)
    max_exact = REL_BUCKETS // 2
    nf = jnp.maximum(n, max_exact).astype(F32)
    large = max_exact + (jnp.log(nf / max_exact) / math.log(REL_MAX_DIST / max_exact)
                         * (REL_BUCKETS - max_exact)).astype(jnp.int32)
    large = jnp.minimum(large, REL_BUCKETS - 1)
    return jnp.where(n < max_exact, n, large)


def _bias_by_distance(table, dist):
    onehot = (_rel_bucket(jnp.asarray(dist, jnp.int32))[..., None]
              == jnp.arange(REL_BUCKETS, dtype=jnp.int32)).astype(F32)
    return jnp.dot(onehot, table, precision=HIGHEST)


def _bias_tables(rel_table, seq, tq, tk):
    table = rel_table.astype(F32) * LOG2E
    heads = table.shape[1]
    far = table[REL_BUCKETS - 1]
    period = 3 * tk + 1
    vec = _bias_by_distance(table, np.arange(2 * tk)) - far
    vec = jnp.concatenate([vec, jnp.full((period - 2 * tk, heads), NEG_INF, F32)], axis=0).T
    skew = jnp.tile(vec, (1, tk))[:, :tk * (period - 1)].reshape(heads, tk, period - 1)
    tiles = skew[:, :, :2 * tq].transpose(0, 2, 1)
    p0 = tiles[:, :tq].reshape(NSA_KV_HEADS, NSA_GROUP * tq, tk)
    p1 = tiles[:, tq:].reshape(NSA_KV_HEADS, NSA_GROUP * tq, tk)
    a2 = (np.arange(NSA_GROUP * tq) % tq)[:, None]
    edge = np.where(2 * tk + a2 - np.arange(tk)[None, :] < WINDOW, 0.0, NEG_INF).astype(np.float32)
    edge = jnp.broadcast_to(jnp.asarray(edge), p0.shape)
    pw_sel = jnp.concatenate([jnp.zeros_like(p0), p1, p0], axis=2)
    pw_win = jnp.concatenate([edge, p1, p0], axis=2)
    front = _cmp_front_pad(seq, tq)
    nb = seq // CMP_STRIDE
    a = np.arange(tq)[:, None]
    rel_blk = np.arange(nb - LANES, nb)[None, :] - front
    dc = a - CMP_STRIDE * rel_blk - (CMP_BLOCK - 1)
    assert (a - CMP_STRIDE * (nb - LANES - 1 - front) - (CMP_BLOCK - 1)).min() >= REL_MAX_DIST
    pc = _bias_by_distance(table, np.maximum(dc, 0)) - far
    pc = jnp.where(jnp.asarray(dc >= 0)[..., None], pc, NEG_INF).transpose(2, 0, 1)
    return pw_sel, pw_win, pc


def _cmp_to_sel_matrix(seq, tq):
    nb = seq // CMP_STRIDE
    n_c = nb - 1
    n_s = seq // SEL_BLOCK
    front = _cmp_front_pad(seq, tq)
    m_mat = np.zeros((2 * nb, n_s), np.float32)
    j = np.arange(n_s)
    for m in range(SEL_BLOCK // CMP_STRIDE):
        for n in range(CMP_BLOCK // CMP_STRIDE):
            c = (SEL_BLOCK // CMP_STRIDE) * j + m - n
            ok = (c >= 0) & (c < n_c)
            np.add.at(m_mat, (front + c[ok], j[ok]), 1.0)
    return jnp.asarray(m_mat)


def _nsa(u, small, pos_k, pos_v, w1k, w2k, w1v, w2v, rel_table, batch, seq, u_kc):
    tq = ATT_TILE
    kc_blk = u_kc // NSA_DIM
    ksl_blk = kc_blk + 2 * NSA_KV_HEADS
    vsl_blk = kc_blk + 3 * NSA_KV_HEADS
    kw_blk = kc_blk + 4 * NSA_KV_HEADS
    vw_blk = kc_blk + 5 * NSA_KV_HEADS
    pw_sel, pw_win, pc = _bias_tables(rel_table, seq, tq, tq)
    ms = _cmp_to_sel_matrix(seq, tq).astype(BF16)
    kc, vc = _compress(u, pos_k.astype(F32), pos_v.astype(F32),
                       w1k.astype(BF16), w2k.astype(BF16),
                       w1v.astype(BF16), w2v.astype(BF16), batch, seq, u_kc, tq)
    o_cmp, sel = _cmp_select(u, kc, vc, pc, ms, batch, seq, tq)
    o_sel = _flash(u, ksl_blk, vsl_blk, pw_sel, batch, seq, "sel", sel=sel)
    return _flash(u, kw_blk, vw_blk, pw_win, batch, seq, "win",
                  merge=(o_cmp, o_sel, small))


def kernel(x, g_mix_norm, w_in, w_alpha2, b_alpha, gla_norm_g, cmp_pos_k, cmp_pos_v,
           phi_k_w1, phi_k_w2, phi_v_w1, phi_v_w2, rel_bias_table, w_gla_proj,
           w_nsa_proj, w_out, g_mlp_norm, w_up, w_down, g_final_norm):
    batch, seq, d = x.shape
    t = batch * seq
    depth = w_in.shape[0]
    u_kc = U_MG + 2 * d
    xf = x.reshape(t, d)
    for l in range(depth):
        w_main_t, w_small_t = _w_in_prep(w_in, l, d)
        wa_pad = jnp.concatenate(
            [w_alpha2[l], jnp.zeros((LANES - GLA_GATE_RANK, GLA_KEY_W), w_alpha2.dtype)],
            axis=0).astype(F32)

        h, small = _rmsnorm_proj(xf, g_mix_norm[l], w_small_t)
        u = _matmul_nt(h, w_main_t, BF16, MM_TILE, MM_TILE, "in_proj")
        o_gla = _gla(u, small, wa_pad, b_alpha[l].reshape(1, -1).astype(F32),
                     gla_norm_g[l].reshape(1, -1).astype(F32), batch, seq)
        o_nsa = _nsa(u, small, cmp_pos_k[l], cmp_pos_v[l], phi_k_w1[l], phi_k_w2[l],
                     phi_v_w1[l], phi_v_w2[l], rel_bias_table, batch, seq, u_kc)
        mix = _mix(o_gla, w_gla_proj[l].astype(BF16), o_nsa, w_nsa_proj[l].astype(BF16), u, d)
        xf, xb, ssq = _out_proj(mix, w_out[l].astype(BF16), xf)
        w_up_g = (w_up[l] * g_mlp_norm[l].astype(F32)[:, None]).astype(BF16)
        act = _mlp_up(xb, w_up_g, ssq)
        xf = _matmul_residual(act, w_down[l].astype(BF16), xf, MM_TILE, MM_TILE, 4096, "mlp_down")
    out = _rmsnorm(xf, g_final_norm, F32)
    return out.reshape(batch, seq, d)
```

```python
import functools
import math

import numpy as np
import jax
import jax.numpy as jnp
from jax import lax
from jax.experimental import pallas as pl
from jax.experimental.pallas import tpu as pltpu

F32 = jnp.float32
BF16 = jnp.bfloat16
HIGHEST = lax.Precision.HIGHEST

NORM_EPS = 1e-6
GLA_HEADS = 16
GLA_DK = 64
GLA_DV = 128
GLA_KEY_W = GLA_HEADS * GLA_DK
GLA_VAL_W = GLA_HEADS * GLA_DV
GLA_GATE_RANK = 16
GLA_TAU = 16.0
GLA_CHUNK = 64
NSA_HEADS = 16
NSA_KV_HEADS = 4
NSA_GROUP = NSA_HEADS // NSA_KV_HEADS
NSA_DIM = 128
NSA_Q_W = NSA_HEADS * NSA_DIM
NSA_KV_W = NSA_KV_HEADS * NSA_DIM
CMP_BLOCK = 32
CMP_STRIDE = 16
SEL_BLOCK = 64
SEL_TOPK = 16
SEL_LOCAL = 2
WINDOW = 512
REL_BUCKETS = 32
REL_MAX_DIST = 128
NEG_INF = -1e30
FORCE_SCORE = 1e4
LOG2E = math.log2(math.e)

LANES = 128
VMEM_LIMIT = 56 * 1024 * 1024
VMEM_LIMIT_KTILED = 62 * 1024 * 1024
ATT_TILE = 256
CMP_TILE = 512
MM_TILE = 1024
GLA_PRE_BLOCK = 256
GLA_UNROLL = 16

U_GQ = 0
U_GK = U_GQ + GLA_KEY_W
U_GV = U_GK + GLA_KEY_W
U_GR = U_GV + GLA_VAL_W
U_NQ = U_GR + GLA_VAL_W
U_MG = U_NQ + NSA_Q_W

NT = (((1,), (1,)), ((), ()))
TN = (((0,), (0,)), ((), ()))


def _params(sem, vmem_limit=VMEM_LIMIT):
    return pltpu.CompilerParams(dimension_semantics=sem, vmem_limit_bytes=vmem_limit)


def _rmsnorm_kernel(x_ref, g_ref, o_ref):
    x = x_ref[...].astype(F32)
    ms = jnp.mean(x * x, axis=-1, keepdims=True)
    o_ref[...] = (x * lax.rsqrt(ms + NORM_EPS) * g_ref[...]).astype(o_ref.dtype)


def _rmsnorm(x2, g, out_dtype, tm=512):
    t, d = x2.shape
    return pl.pallas_call(
        _rmsnorm_kernel,
        grid=(t // tm,),
        in_specs=[pl.BlockSpec((tm, d), lambda i: (i, 0)),
                  pl.BlockSpec((1, d), lambda i: (0, 0))],
        out_specs=pl.BlockSpec((tm, d), lambda i: (i, 0)),
        out_shape=jax.ShapeDtypeStruct((t, d), out_dtype),
        compiler_params=_params(("parallel",)),
        name="rmsnorm",
    )(x2, g.reshape(1, d).astype(F32))


def _rmsnorm_proj_kernel(x_ref, g_ref, wt_ref, o_ref, p_ref):
    x = x_ref[...].astype(F32)
    ms = jnp.mean(x * x, axis=-1, keepdims=True)
    h = (x * lax.rsqrt(ms + NORM_EPS) * g_ref[...]).astype(o_ref.dtype)
    o_ref[...] = h
    p_ref[...] = lax.dot_general(h, wt_ref[...], NT, preferred_element_type=F32)


def _rmsnorm_proj(x2, g, w_t, tm=512):
    t, d = x2.shape
    n = w_t.shape[0]
    return pl.pallas_call(
        _rmsnorm_proj_kernel,
        grid=(t // tm,),
        in_specs=[pl.BlockSpec((tm, d), lambda i: (i, 0)),
                  pl.BlockSpec((1, d), lambda i: (0, 0)),
                  pl.BlockSpec((n, d), lambda i: (0, 0))],
        out_specs=[pl.BlockSpec((tm, d), lambda i: (i, 0)),
                   pl.BlockSpec((tm, n), lambda i: (i, 0))],
        out_shape=[jax.ShapeDtypeStruct((t, d), BF16), jax.ShapeDtypeStruct((t, n), F32)],
        compiler_params=_params(("parallel",)),
        name="rmsnorm_proj",
    )(x2, g.reshape(1, d).astype(F32), w_t)


W_ROW_ALIGN = 16


def _w_in_prep_kernel(src_ref, w_ref, ga_ref, ng_ref, o_ref, os_ref, *, q_blocks, q_scale):
    del src_ref
    i = pl.program_id(0)
    is_q = jnp.logical_and(i >= q_blocks[0], i < q_blocks[1])
    scale = jnp.where(is_q, q_scale, 1.0).astype(F32)
    o_ref[...] = (w_ref[0] * scale).astype(o_ref.dtype)

    @pl.when(i == 0)
    def _():
        n_ga = ga_ref.shape[1]
        n_ng = ng_ref.shape[1]
        os_ref[0:n_ga] = ga_ref[0].astype(os_ref.dtype)
        os_ref[n_ga:n_ga + n_ng] = ng_ref[0].astype(os_ref.dtype)
        os_ref[n_ga + n_ng:] = jnp.zeros((os_ref.shape[0] - n_ga - n_ng, os_ref.shape[1]),
                                         os_ref.dtype)


def _w_in_prep(w_in, layer, d, tr=512):
    n = w_in.shape[2]
    s_ga = 2 * GLA_KEY_W + GLA_VAL_W
    s_gr = s_ga + GLA_GATE_RANK
    s_nq = s_gr + GLA_VAL_W
    s_kc = s_nq + NSA_Q_W
    s_ng = s_kc + 6 * NSA_KV_W
    s_mg = s_ng + 3 * NSA_HEADS
    assert n == s_mg + 2 * d
    w_t = jnp.swapaxes(w_in, 1, 2)
    pieces = [(0, s_ga, U_GQ), (s_gr, s_nq, U_GR), (s_nq, s_kc, U_NQ),
              (s_mg, n, U_MG), (s_kc, s_ng, U_MG + 2 * d)]
    src_rows = []
    for s0, s1, d0 in pieces:
        assert s0 % W_ROW_ALIGN == 0 and (s1 - s0) % tr == 0 and d0 == len(src_rows) * tr
        src_rows += [r // W_ROW_ALIGN for r in range(s0, s1, tr)]
    n_main = len(src_rows) * tr
    def rows_at(start, count):
        return pl.BlockSpec((pl.Element(1), pl.Element(count), pl.Element(d)),
                            lambda i, src: (layer, start, 0))

    return pl.pallas_call(
        functools.partial(_w_in_prep_kernel,
                          q_blocks=(U_NQ // tr, (U_NQ + NSA_Q_W) // tr),
                          q_scale=NSA_DIM ** -0.5 * LOG2E),
        grid_spec=pltpu.PrefetchScalarGridSpec(
            num_scalar_prefetch=1,
            grid=(n_main // tr,),
            in_specs=[pl.BlockSpec(
                (pl.Element(1), pl.Element(tr), pl.Element(d)),
                lambda i, src: (layer, src[i] * W_ROW_ALIGN, 0)),
                rows_at(s_ga, s_gr - s_ga), rows_at(s_ng, s_mg - s_ng)],
            out_specs=[pl.BlockSpec((tr, d), lambda i, src: (i, 0)),
                       pl.BlockSpec((LANES, d), lambda i, src: (0, 0))]),
        out_shape=[jax.ShapeDtypeStruct((n_main, d), BF16),
                   jax.ShapeDtypeStruct((LANES, d), BF16)],
        compiler_params=_params(("arbitrary",)),
        name="w_in_prep",
    )(jnp.asarray(src_rows, jnp.int32), w_t, w_t, w_t)


def _mm_nt_kernel(a_ref, bt_ref, o_ref):
    o_ref[...] = lax.dot_general(a_ref[...], bt_ref[...], NT,
                                 preferred_element_type=F32).astype(o_ref.dtype)


def _matmul_nt(a, b_t, out_dtype, tm, tn, name):
    m, k = a.shape
    n = b_t.shape[0]
    return pl.pallas_call(
        _mm_nt_kernel,
        grid=(m // tm, n // tn),
        in_specs=[pl.BlockSpec((tm, k), lambda i, j: (i, 0)),
                  pl.BlockSpec((tn, k), lambda i, j: (j, 0))],
        out_specs=pl.BlockSpec((tm, tn), lambda i, j: (i, j)),
        out_shape=jax.ShapeDtypeStruct((m, n), out_dtype),
        compiler_params=_params(("parallel", "arbitrary")),
        name=name,
    )(a, b_t)


def _mix_kernel(og_ref, wg_ref, on_ref, wn_ref, mg_ref, mn_ref, o_ref):
    yg = jnp.dot(og_ref[...], wg_ref[...], preferred_element_type=F32)
    yn = jnp.dot(on_ref[...], wn_ref[...], preferred_element_type=F32)
    o = (jax.nn.sigmoid(mg_ref[...].astype(F32)) * yg
         + jax.nn.sigmoid(mn_ref[...].astype(F32)) * yn)
    o_ref[...] = o.astype(o_ref.dtype)


def _mix(o_gla, w_g, o_nsa, w_n, u, d, tm=MM_TILE, tn=MM_TILE):
    t = o_gla.shape[0]
    mg_blk = U_MG // tn
    mn_blk = (U_MG + d) // tn
    return pl.pallas_call(
        _mix_kernel,
        grid=(t // tm, d // tn),
        in_specs=[pl.BlockSpec((tm, o_gla.shape[1]), lambda i, j: (i, 0)),
                  pl.BlockSpec((w_g.shape[0], tn), lambda i, j: (0, j)),
                  pl.BlockSpec((tm, o_nsa.shape[1]), lambda i, j: (i, 0)),
                  pl.BlockSpec((w_n.shape[0], tn), lambda i, j: (0, j)),
                  pl.BlockSpec((tm, tn), lambda i, j: (i, mg_blk + j)),
                  pl.BlockSpec((tm, tn), lambda i, j: (i, mn_blk + j))],
        out_specs=pl.BlockSpec((tm, tn), lambda i, j: (i, j)),
        out_shape=jax.ShapeDtypeStruct((t, d), BF16),
        compiler_params=_params(("parallel", "arbitrary")),
        name="mix",
    )(o_gla, w_g, o_nsa, w_n, u, u)


def _mm_res_kernel(a_ref, b_ref, r_ref, o_ref, acc_ref):
    kk = pl.program_id(2)

    @pl.when(kk == 0)
    def _():
        acc_ref[...] = r_ref[...]

    acc_ref[...] += jnp.dot(a_ref[...], b_ref[...], preferred_element_type=F32)

    @pl.when(kk == pl.num_programs(2) - 1)
    def _():
        o_ref[...] = acc_ref[...]


def _out_proj_kernel(a_ref, b_ref, r_ref, o_ref, ob_ref, ss_ref):
    y = r_ref[...] + jnp.dot(a_ref[...], b_ref[...], preferred_element_type=F32)
    o_ref[...] = y
    ob_ref[...] = y.astype(ob_ref.dtype)
    ss_ref[...] = jnp.broadcast_to(jnp.sum(y * y, axis=-1, keepdims=True), ss_ref.shape)


def _out_proj(a, b, res, tm=MM_TILE, tn=MM_TILE):
    m, k = a.shape
    _, n = b.shape
    return pl.pallas_call(
        _out_proj_kernel,
        grid=(m // tm, n // tn),
        in_specs=[pl.BlockSpec((tm, k), lambda i, j: (i, 0)),
                  pl.BlockSpec((k, tn), lambda i, j: (0, j)),
                  pl.BlockSpec((tm, tn), lambda i, j: (i, j))],
        out_specs=[pl.BlockSpec((tm, tn), lambda i, j: (i, j)),
                   pl.BlockSpec((tm, tn), lambda i, j: (i, j)),
                   pl.BlockSpec((tm, LANES), lambda i, j: (i, j))],
        out_shape=[jax.ShapeDtypeStruct((m, n), F32), jax.ShapeDtypeStruct((m, n), BF16),
                   jax.ShapeDtypeStruct((m, n // tn * LANES), F32)],
        compiler_params=_params(("parallel", "arbitrary"), VMEM_LIMIT_KTILED),
        name="out_proj",
    )(a, b, res)


def _mlp_up_kernel(a_ref, b_ref, ss_ref, o_ref, *, d):
    acc = jnp.dot(a_ref[...], b_ref[...], preferred_element_type=F32)
    ss = ss_ref[...]
    total = ss[:, :LANES]
    for p in range(1, ss.shape[1] // LANES):
        total = total + ss[:, p * LANES:(p + 1) * LANES]
    inv = 1.0 / (total[:, :1] * (1.0 / d) + NORM_EPS)
    r = jnp.maximum(acc, 0.0)
    o_ref[...] = (r * r * inv).astype(o_ref.dtype)


def _mlp_up(xb, w_scaled, ssq, tm=MM_TILE, tn=MM_TILE):
    m, k = xb.shape
    _, n = w_scaled.shape
    return pl.pallas_call(
        functools.partial(_mlp_up_kernel, d=k),
        grid=(m // tm, n // tn),
        in_specs=[pl.BlockSpec((tm, k), lambda i, j: (i, 0)),
                  pl.BlockSpec((k, tn), lambda i, j: (0, j)),
                  pl.BlockSpec((tm, ssq.shape[1]), lambda i, j: (i, 0))],
        out_specs=pl.BlockSpec((tm, tn), lambda i, j: (i, j)),
        out_shape=jax.ShapeDtypeStruct((m, n), BF16),
        compiler_params=_params(("parallel", "arbitrary")),
        name="mlp_up",
    )(xb, w_scaled, ssq)


def _matmul_residual(a, b, res, tm, tn, tk, name):
    m, k = a.shape
    _, n = b.shape
    return pl.pallas_call(
        _mm_res_kernel,
        grid=(m // tm, n // tn, k // tk),
        in_specs=[pl.BlockSpec((tm, tk), lambda i, j, kk: (i, kk)),
                  pl.BlockSpec((tk, tn), lambda i, j, kk: (kk, j)),
                  pl.BlockSpec((tm, tn), lambda i, j, kk: (i, j))],
        out_specs=pl.BlockSpec((tm, tn), lambda i, j, kk: (i, j)),
        out_shape=jax.ShapeDtypeStruct((m, n), F32),
        scratch_shapes=[pltpu.VMEM((tm, tn), F32)],
        compiler_params=_params(("parallel", "arbitrary", "arbitrary"), VMEM_LIMIT_KTILED),
        name=name,
    )(a, b, res)


def _gla_kernel(q_ref, k_ref, v_ref, r_ref, a_ref, wa_ref, ba_ref, g_ref, o_ref,
                bc_ref, st_ref, *, seq):
    c = GLA_CHUNK
    dv = GLA_DV
    shift = int(math.log2(c))

    nb = GLA_PRE_BLOCK
    prow = lax.broadcasted_iota(jnp.int32, (nb, nb), 0)
    pcol = lax.broadcasted_iota(jnp.int32, (nb, nb), 1)
    same_chunk = lax.shift_right_logical(prow, shift) == lax.shift_right_logical(pcol, shift)
    tri = jnp.where(prow >= pcol, jnp.where(same_chunk, 1.0, 0.0), 0.0).astype(BF16)

    def split(x):
        hi = x.astype(BF16)
        return hi, (x - hi.astype(F32)).astype(BF16)

    wa_hi, wa_lo = split(wa_ref[...])
    wa3 = jnp.concatenate([wa_hi, wa_hi, wa_lo], axis=0)

    def cumulate(b, carry):
        r0 = pl.multiple_of(b * nb, nb)
        a_hi, a_lo = split(a_ref[pl.ds(r0, nb), :])
        z = jnp.dot(jnp.concatenate([a_hi, a_lo, a_hi], axis=1), wa3,
                    preferred_element_type=F32) + ba_ref[...]
        la = (jnp.minimum(z, 0.0) - jnp.log(1.0 + jnp.exp(-jnp.abs(z)))) * (1.0 / GLA_TAU)
        cs = jnp.dot(tri, jnp.concatenate(split(la), axis=1), preferred_element_type=F32)
        w = la.shape[1]
        bc_ref[pl.ds(r0, nb), :] = cs[:, :w] + cs[:, w:]
        return carry

    lax.fori_loop(0, seq // nb, cumulate, 0, unroll=8)
    st_ref[...] = jnp.zeros_like(st_ref)

    row2 = lax.broadcasted_iota(jnp.int32, (2 * c, c), 0)
    col2 = lax.broadcasted_iota(jnp.int32, (2 * c, c), 1)
    causal2 = (row2 & (c - 1)) >= col2
    lane = lax.broadcasted_iota(jnp.int32, (1, 2 * GLA_DK), 1)
    head_mask = [(lane < GLA_DK).astype(F32), (lane >= GLA_DK).astype(F32)]
    gvec = g_ref[...]

    def body(i, carry):
        r0 = pl.multiple_of(i * c, c)
        bc = bc_ref[pl.ds(r0, c), :]
        last = bc[c - 1:c, :]
        q = q_ref[pl.ds(r0, c), :].astype(F32) * (GLA_DK ** -0.5)
        k = k_ref[pl.ds(r0, c), :].astype(F32)
        qe = q * jnp.exp(bc)
        ke = (k * jnp.exp(-bc)).astype(BF16)
        kd = (k * jnp.exp(last - bc)).astype(BF16)
        v = v_ref[pl.ds(r0, c), :]
        st = st_ref[...]
        q2 = jnp.concatenate([qe * head_mask[0], qe * head_mask[1]], axis=0).astype(BF16)
        att = lax.dot_general(q2, ke, NT, preferred_element_type=F32)
        att = jnp.where(causal2, att, 0.0).astype(BF16)
        o2 = (jnp.dot(att, v, preferred_element_type=F32)
              + lax.dot_general(q2, st.astype(BF16), NT, preferred_element_type=F32))
        outs = []
        for h in range(2):
            oh = o2[h * c:(h + 1) * c, h * dv:(h + 1) * dv]
            ms = jnp.mean(oh * oh, axis=-1, keepdims=True)
            y = oh * lax.rsqrt(ms + NORM_EPS) * gvec[:, h * dv:(h + 1) * dv]
            rr = r_ref[pl.ds(r0, c), h * dv:(h + 1) * dv].astype(F32)
            outs.append(y * (rr * jax.nn.sigmoid(rr)))
        o_ref[pl.ds(r0, c), :] = jnp.concatenate(outs, axis=1).astype(o_ref.dtype)
        kv = lax.dot_general(v, kd, TN, preferred_element_type=F32)
        st_ref[...] = st * jnp.exp(last) + kv
        return carry

    lax.fori_loop(0, seq // c, body, 0, unroll=GLA_UNROLL)


def _gla(u, small, w_alpha_pad, b_alpha, norm_g, batch, seq):
    t = batch * seq
    pairs = GLA_HEADS // 2
    kw = 2 * GLA_DK
    vw = 2 * GLA_DV
    return pl.pallas_call(
        functools.partial(_gla_kernel, seq=seq),
        grid=(batch, pairs),
        in_specs=[pl.BlockSpec((seq, kw), lambda b, j: (b, U_GQ // kw + j)),
                  pl.BlockSpec((seq, kw), lambda b, j: (b, U_GK // kw + j)),
                  pl.BlockSpec((seq, vw), lambda b, j: (b, U_GV // vw + j)),
                  pl.BlockSpec((seq, vw), lambda b, j: (b, U_GR // vw + j)),
                  pl.BlockSpec((seq, LANES), lambda b, j: (b, 0)),
                  pl.BlockSpec((LANES, kw), lambda b, j: (0, j)),
                  pl.BlockSpec((1, kw), lambda b, j: (0, j)),
                  pl.BlockSpec((1, vw), lambda b, j: (0, j))],
        out_specs=pl.BlockSpec((seq, vw), lambda b, j: (b, j)),
        out_shape=jax.ShapeDtypeStruct((t, GLA_VAL_W), BF16),
        scratch_shapes=[pltpu.VMEM((seq, kw), F32), pltpu.VMEM((vw, kw), F32)],
        compiler_params=_params(("parallel", "arbitrary")),
        name="gla",
    )(u, u, u, u, small, w_alpha_pad, b_alpha, norm_g)


def _cmp_cols_per_tile(tq):
    return tq // CMP_STRIDE


def _cmp_front_pad(seq, tq):
    return seq // CMP_STRIDE - _cmp_cols_per_tile(tq)


def _compress_kernel(xk_ref, xv_ref, pk_ref, pv_ref, w1k_ref, w2k_ref, w1v_ref, w2v_ref,
                     ok_ref, ov_ref, xf_ref, *, front):
    n = CMP_STRIDE
    d = NSA_DIM

    def one(x_ref, pos_ref, w1_ref, w2_ref, o_ref):
        nb = x_ref.shape[0] // n
        xf_ref[...] = x_ref[...].astype(F32)
        top = jnp.zeros((nb, w1_ref.shape[1]), F32)
        bot = jnp.zeros((nb, w1_ref.shape[1]), F32)
        for j in range(n):
            x = xf_ref[pl.ds(j, nb, stride=n), :]
            xa = (x + pos_ref[j:j + 1, :]).astype(BF16)
            xb = (x + pos_ref[n + j:n + j + 1, :]).astype(BF16)
            top = top + jnp.dot(xa, w1_ref[j * d:(j + 1) * d, :],
                                preferred_element_type=F32)
            bot = bot + jnp.dot(xb, w1_ref[(n + j) * d:(n + j + 1) * d, :],
                                preferred_element_type=F32)
        hid = top + pltpu.roll(bot, nb - 1, axis=0)
        hid = jnp.maximum(hid, 0.0).astype(BF16)
        total = o_ref.shape[2]
        o_ref[0, 0, 0:front] = jnp.zeros((front, d), o_ref.dtype)
        o_ref[0, 0, front:front + nb] = jnp.dot(
            hid, w2_ref[...], preferred_element_type=F32).astype(o_ref.dtype)
        o_ref[0, 0, front + nb:total] = jnp.zeros((total - front - nb, d), o_ref.dtype)

    one(xk_ref, pk_ref, w1k_ref, w2k_ref, ok_ref)
    one(xv_ref, pv_ref, w1v_ref, w2v_ref, ov_ref)


def _compress(u, pos_k, pos_v, w1k, w2k, w1v, w2v, batch, seq, u_kc, tq):
    nb = seq // CMP_STRIDE
    front = _cmp_front_pad(seq, tq)
    kc_blk = u_kc // NSA_DIM
    vc_blk = kc_blk + NSA_KV_HEADS
    full = lambda a: pl.BlockSpec(a.shape, lambda b, g: (0,) * a.ndim)
    out = jax.ShapeDtypeStruct((batch, NSA_KV_HEADS, 2 * nb, NSA_DIM), BF16)
    ospec = pl.BlockSpec((1, 1, 2 * nb, NSA_DIM), lambda b, g: (b, g, 0, 0))
    return pl.pallas_call(
        functools.partial(_compress_kernel, front=front),
        grid=(batch, NSA_KV_HEADS),
        in_specs=[pl.BlockSpec((seq, NSA_DIM), lambda b, g: (b, kc_blk + g)),
                  pl.BlockSpec((seq, NSA_DIM), lambda b, g: (b, vc_blk + g)),
                  full(pos_k), full(pos_v), full(w1k), full(w2k), full(w1v), full(w2v)],
        out_specs=[ospec, ospec],
        out_shape=[out, out],
        scratch_shapes=[pltpu.VMEM((seq, NSA_DIM), F32)],
        compiler_params=_params(("parallel", "arbitrary")),
        name="nsa_compress",
    )(u, u, pos_k, pos_v, w1k, w2k, w1v, w2v)


def _cmp_select_kernel(q_ref, kc_ref, vc_ref, pc_ref, ms_ref, o_ref, sel_ref, score_ref, *,
                       tq, nb, n_s, top_n, front):
    i = pl.program_id(2)
    d = NSA_DIM
    w0 = pl.multiple_of(i * _cmp_cols_per_tile(tq), _cmp_cols_per_tile(tq))
    kc = kc_ref[0, 0, pl.ds(w0, nb), :]
    vc = vc_ref[0, 0, pl.ds(w0, nb), :]
    rows = NSA_GROUP * tq
    q = jnp.concatenate([q_ref[:, r * d:(r + 1) * d] for r in range(NSA_GROUP)], axis=0)
    col = lax.broadcasted_iota(jnp.int32, (rows, nb), 1)
    before_start = jnp.where(col < front - w0, NEG_INF, 0.0)
    s = lax.dot_general(q, kc, NT, preferred_element_type=F32) + before_start
    bias = pc_ref[...].reshape(rows, LANES)
    s = jnp.concatenate([s[:, :nb - LANES], s[:, nb - LANES:] + bias], axis=1)
    m = jnp.max(s, axis=-1, keepdims=True)
    e = jnp.exp2(s - m)
    p = e * (1.0 / jnp.sum(e, axis=-1, keepdims=True))
    p = jnp.where(s > 0.5 * NEG_INF, p, 0.0)
    o = jnp.dot(p.astype(BF16), vc, preferred_element_type=F32)
    psum = p[0:tq]
    for r in range(NSA_GROUP):
        o_ref[:, r * d:(r + 1) * d] = o[r * tq:(r + 1) * tq].astype(o_ref.dtype)
        if r:
            psum = psum + p[r * tq:(r + 1) * tq]
    p_hi = psum.astype(BF16)
    p_lo = (psum - p_hi.astype(F32)).astype(BF16)
    imp2 = lax.dot_general(ms_ref[pl.ds(w0, nb), :], jnp.concatenate([p_hi, p_lo], axis=0),
                           (((0,), (1,)), ((), ())), preferred_element_type=F32)
    imp = imp2[:, :tq] + imp2[:, tq:]
    blk = lax.broadcasted_iota(jnp.int32, (n_s, tq), 0)
    pos = i * tq + lax.broadcasted_iota(jnp.int32, (n_s, tq), 1)
    cur = lax.shift_right_logical(pos, int(math.log2(SEL_BLOCK)))
    forced = jnp.where(blk == 0, 1, jnp.where(blk <= cur, jnp.where(blk > cur - SEL_LOCAL, 1, 0), 0))
    score = jnp.where(forced == 1, FORCE_SCORE, jnp.where(blk <= cur, imp, -FORCE_SCORE))
    score_ref[...] = score
    per_trip = tq // SEL_BLOCK

    def count_ahead(g, rank):
        for r in range(per_trip):
            j = g * per_trip + r
            rj = score_ref[pl.ds(j, 1), :]
            tie = jnp.where(rj == score, jnp.where(blk > j, 1, 0), 0)
            rank = rank + jnp.where(rj > score, 1, tie)
        return rank

    rank = lax.fori_loop(0, i + 1, count_ahead, jnp.zeros((n_s, tq), jnp.int32))
    selb = jnp.where(rank < top_n, 0.0, NEG_INF)
    if n_s < LANES:
        selb = jnp.concatenate([selb, jnp.zeros((LANES - n_s, tq), F32)], axis=0)
    sel_ref[0, 0] = selb.T.astype(sel_ref.dtype)


def _cmp_select(u, kc, vc, pc, ms, batch, seq, tq):
    t = batch * seq
    nb = seq // CMP_STRIDE
    n_s = seq // SEL_BLOCK
    top_n = min(SEL_TOPK, n_s)
    nq = seq // tq
    qw = NSA_GROUP * NSA_DIM
    kv_spec = pl.BlockSpec((1, 1, 2 * nb, NSA_DIM), lambda b, g, i: (b, g, 0, 0))
    return pl.pallas_call(
        functools.partial(_cmp_select_kernel, tq=tq, nb=nb, n_s=n_s, top_n=top_n,
                          front=_cmp_front_pad(seq, tq)),
        grid=(batch, NSA_KV_HEADS, nq),
        in_specs=[pl.BlockSpec((tq, qw), lambda b, g, i: (b * nq + i, U_NQ // qw + g)),
                  kv_spec, kv_spec,
                  pl.BlockSpec((NSA_GROUP, tq, LANES), lambda b, g, i: (g, 0, 0)),
                  pl.BlockSpec((2 * nb, n_s), lambda b, g, i: (0, 0))],
        out_specs=[pl.BlockSpec((tq, qw), lambda b, g, i: (b * nq + i, g)),
                   pl.BlockSpec((1, 1, tq, LANES), lambda b, g, i: (b, g, i, 0))],
        out_shape=[jax.ShapeDtypeStruct((t, NSA_Q_W), BF16),
                   jax.ShapeDtypeStruct((batch, NSA_KV_HEADS, seq, LANES), BF16)],
        scratch_shapes=[pltpu.VMEM((n_s, tq), F32)],
        compiler_params=_params(("parallel", "parallel", "arbitrary")),
        name="nsa_cmp_select",
    )(u, kc, vc, pc, ms)


def _flash_kernel(*refs, mode, tq, tk):
    sel = mode == "sel"
    if sel:
        (q_ref, k_ref, v_ref, sel_ref, pw_ref, o_ref,
         qs_ref, ks_ref, m_ref, l_ref, acc_ref) = refs
    else:
        (q_ref, k_ref, v_ref, pw_ref, oc_ref, os_ref, ng_ref, o_ref, vs_ref) = refs
    i = pl.program_id(2)
    d = NSA_DIM

    if sel:
        @pl.when(i == 0)
        def _():
            seq = k_ref.shape[0]
            ks_ref[:, :d] = k_ref[...]
            krow = lax.broadcasted_iota(jnp.int32, (seq, LANES), 0)
            klane = lax.broadcasted_iota(jnp.int32, (seq, LANES), 1)
            kblk = lax.shift_right_logical(krow, int(math.log2(SEL_BLOCK)))
            ks_ref[:, d:] = jnp.where(kblk == klane, 1.0, 0.0).astype(ks_ref.dtype)

        for r in range(NSA_GROUP):
            qs_ref[r * tq:(r + 1) * tq, :d] = q_ref[:, r * d:(r + 1) * d]
            qs_ref[r * tq:(r + 1) * tq, d:] = sel_ref[0, 0]
        m_ref[...] = jnp.full_like(m_ref, NEG_INF)
        l_ref[...] = jnp.zeros_like(l_ref)
        acc_ref[...] = jnp.zeros_like(acc_ref)

    if not sel:
        group = pl.program_id(1)
        per_group = 3 * NSA_GROUP
        shift = lax.rem(LANES - per_group * group, LANES)
        sig = pltpu.roll(jax.nn.sigmoid(ng_ref[...]), shift, axis=1)

        def gate(h, br):
            c = GLA_GATE_RANK + 3 * h + br
            return jnp.broadcast_to(sig[:, c:c + 1], (tq, d))

        @pl.when(i == 0)
        def _():
            vs_ref[:, :d] = v_ref[...]
            vs_ref[:, d:] = jnp.ones((v_ref.shape[0], d), vs_ref.dtype)

    chains = [(0, NSA_GROUP)] if sel else [(r, r + 1) for r in range(NSA_GROUP)]

    def chain_logits(h0, h1, kt):
        q = qs_ref[h0 * tq:h1 * tq, :] if sel else q_ref[:, h0 * d:h1 * d]
        return lax.dot_general(q, kt, NT, preferred_element_type=F32)

    def online_update(rows, s, vt):
        m_prev = m_ref[rows]
        m_new = jnp.maximum(m_prev, jnp.max(s, axis=-1, keepdims=True))
        alpha = jnp.exp2(m_prev - m_new)
        p = jnp.exp2(s - jnp.concatenate([m_new] * (s.shape[1] // LANES), axis=1))
        l_new = alpha * l_ref[rows] + jnp.sum(p, axis=-1, keepdims=True)
        acc_new = alpha * acc_ref[rows] + jnp.dot(p.astype(BF16), vt,
                                                  preferred_element_type=F32)
        return m_new, l_new, acc_new

    def last_tiles(n):
        width = n * tk
        k0 = pl.multiple_of((i + 1 - n) * tk, tk)
        kt = ks_ref[pl.ds(k0, width), :] if sel else k_ref[pl.ds(k0, width), :]
        vt = v_ref[pl.ds(k0, width), :] if sel else vs_ref[pl.ds(k0, width), :]
        for h0, h1 in chains:
            rows = slice(h0 * tq, h1 * tq)
            s = chain_logits(h0, h1, kt) + pw_ref[0, rows, (3 - n) * tk:]
            if sel:
                _, l, acc = online_update(rows, s, vt)
            else:
                p = jnp.exp2(s - jnp.max(s, axis=-1, keepdims=True))
                acc = jnp.dot(p.astype(BF16), vt, preferred_element_type=F32)
                acc, l = acc[:, :d], acc[:, d:]
            out = acc * (1.0 / l)
            for h in range(h0, h1):
                out_h = out[(h - h0) * tq:(h - h0 + 1) * tq, :]
                if not sel:
                    lanes = slice(h * d, (h + 1) * d)
                    out_h = (gate(h, 2) * out_h
                             + gate(h, 0) * oc_ref[:, lanes].astype(F32)
                             + gate(h, 1) * os_ref[:, lanes].astype(F32))
                o_ref[:, h * d:(h + 1) * d] = out_h.astype(o_ref.dtype)

    if sel:
        n_far = jnp.maximum(i - 1, 0)
        odd = n_far % 2

        def far_pair(j):
            k0 = pl.multiple_of(j * (2 * tk), 2 * tk)
            kt = ks_ref[pl.ds(k0, 2 * tk), :]
            vt = v_ref[pl.ds(k0, 2 * tk), :]
            for h0, h1 in chains:
                rows = slice(h0 * tq, h1 * tq)
                m_ref[rows], l_ref[rows], acc_ref[rows] = online_update(
                    rows, chain_logits(h0, h1, kt), vt)

        n_pairs = n_far // 2

        def far(j, carry):
            far_pair(2 * j)
            far_pair(2 * j + 1)
            return carry
        lax.fori_loop(0, n_pairs // 2, far, 0)
        pl.when(n_pairs % 2 == 1)(lambda: far_pair(n_pairs - 1))
        pl.when(i == 0)(lambda: last_tiles(1))
        pl.when(jnp.logical_and(i >= 1, odd == 0))(lambda: last_tiles(2))
        pl.when(odd == 1)(lambda: last_tiles(3))
    else:
        pl.when(i == 0)(lambda: last_tiles(1))
        pl.when(i == 1)(lambda: last_tiles(2))
        pl.when(i >= 2)(lambda: last_tiles(3))


def _flash(u, k_blk, v_blk, pw, batch, seq, mode, sel=None, merge=None,
           tq=ATT_TILE, tk=ATT_TILE):
    assert tq == tk and 2 * tk >= REL_MAX_DIST and 3 * tk > WINDOW >= 2 * tk
    t = batch * seq
    nq = seq // tq
    qw = NSA_GROUP * NSA_DIM
    rows = NSA_GROUP * tq
    d = NSA_DIM
    q_spec = pl.BlockSpec((tq, qw), lambda b, g, i: (b * nq + i, U_NQ // qw + g))
    k_spec = pl.BlockSpec((seq, d), lambda b, g, i: (b, k_blk + g))
    v_spec = pl.BlockSpec((seq, d), lambda b, g, i: (b, v_blk + g))
    b_spec = pl.BlockSpec((1, rows, 3 * tk), lambda b, g, i: (g, 0, 0))
    if mode == "sel":
        in_specs = [q_spec, k_spec, v_spec,
                    pl.BlockSpec((1, 1, tq, LANES), lambda b, g, i: (b, g, i, 0)), b_spec]
        args = (u, u, u, sel, pw)
        scratch = ([pltpu.VMEM((rows, 2 * d), BF16), pltpu.VMEM((seq, 2 * d), BF16)]
                   + [pltpu.VMEM((rows, LANES), F32)] * 3)
    else:
        o_spec = pl.BlockSpec((tq, qw), lambda b, g, i: (b * nq + i, g))
        o_cmp, o_sel, gate_logits = merge
        in_specs = [q_spec, k_spec, v_spec, b_spec, o_spec, o_spec,
                    pl.BlockSpec((tq, LANES), lambda b, g, i: (b * nq + i, 0))]
        args = (u, u, u, pw, o_cmp, o_sel, gate_logits)
        scratch = [pltpu.VMEM((seq, 2 * d), BF16)]
    return pl.pallas_call(
        functools.partial(_flash_kernel, mode=mode, tq=tq, tk=tk),
        grid=(batch, NSA_KV_HEADS, nq),
        in_specs=in_specs,
        out_specs=pl.BlockSpec((tq, qw), lambda b, g, i: (b * nq + i, g)),
        out_shape=jax.ShapeDtypeStruct((t, NSA_Q_W), BF16),
        scratch_shapes=scratch,
        compiler_params=_params(("parallel", "parallel", "arbitrary")),
        name="nsa_flash_" + mode,
    )(*args)


def _rel_bucket(dist):
    n = jnp.maximum(dist, 0)
    max_exact = REL_BUCKETS // 2
    nf = jnp.maximum(n, max_exact).astype(F32)
    large = max_exact + (jnp.log(nf / max_exact) / math.log(REL_MAX_DIST / max_exact)
                         * (REL_BUCKETS - max_exact)).astype(jnp.int32)
    large = jnp.minimum(large, REL_BUCKETS - 1)
    return jnp.where(n < max_exact, n, large)


def _bias_by_distance(table, dist):
    onehot = (_rel_bucket(jnp.asarray(dist, jnp.int32))[..., None]
              == jnp.arange(REL_BUCKETS, dtype=jnp.int32)).astype(F32)
    return jnp.dot(onehot, table, precision=HIGHEST)


def _bias_tables(rel_table, seq, tq, tk, tq_cmp):
    table = rel_table.astype(F32) * LOG2E
    heads = table.shape[1]
    far = table[REL_BUCKETS - 1]
    period = 3 * tk + 1
    vec = _bias_by_distance(table, np.arange(2 * tk)) - far
    vec = jnp.concatenate([vec, jnp.full((period - 2 * tk, heads), NEG_INF, F32)], axis=0).T
    skew = jnp.tile(vec, (1, tk))[:, :tk * (period - 1)].reshape(heads, tk, period - 1)
    tiles = skew[:, :, :2 * tq].transpose(0, 2, 1)
    p0 = tiles[:, :tq].reshape(NSA_KV_HEADS, NSA_GROUP * tq, tk)
    p1 = tiles[:, tq:].reshape(NSA_KV_HEADS, NSA_GROUP * tq, tk)
    a2 = (np.arange(NSA_GROUP * tq) % tq)[:, None]
    edge = np.where(2 * tk + a2 - np.arange(tk)[None, :] < WINDOW, 0.0, NEG_INF).astype(np.float32)
    edge = jnp.broadcast_to(jnp.asarray(edge), p0.shape)
    pw_sel = jnp.concatenate([jnp.zeros_like(p0), p1, p0], axis=2)
    pw_win = jnp.concatenate([edge, p1, p0], axis=2)
    front = _cmp_front_pad(seq, tq_cmp)
    nb = seq // CMP_STRIDE
    a = np.arange(tq_cmp)[:, None]
    rel_blk = np.arange(nb - LANES, nb)[None, :] - front
    dc = a - CMP_STRIDE * rel_blk - (CMP_BLOCK - 1)
    assert (a - CMP_STRIDE * (nb - LANES - 1 - front) - (CMP_BLOCK - 1)).min() >= REL_MAX_DIST
    pc = _bias_by_distance(table, np.maximum(dc, 0)) - far
    pc = jnp.where(jnp.asarray(dc >= 0)[..., None], pc, NEG_INF).transpose(2, 0, 1)
    return pw_sel, pw_win, pc


def _cmp_to_sel_matrix(seq, tq):
    nb = seq // CMP_STRIDE
    n_c = nb - 1
    n_s = seq // SEL_BLOCK
    front = _cmp_front_pad(seq, tq)
    m_mat = np.zeros((2 * nb, n_s), np.float32)
    j = np.arange(n_s)
    for m in range(SEL_BLOCK // CMP_STRIDE):
        for n in range(CMP_BLOCK // CMP_STRIDE):
            c = (SEL_BLOCK // CMP_STRIDE) * j + m - n
            ok = (c >= 0) & (c < n_c)
            np.add.at(m_mat, (front + c[ok], j[ok]), 1.0)
    return jnp.asarray(m_mat)


def _nsa(u, small, pos_k, pos_v, w1k, w2k, w1v, w2v, rel_table, batch, seq, u_kc):
    tq = ATT_TILE
    tq_cmp = min(CMP_TILE, seq)
    kc_blk = u_kc // NSA_DIM
    ksl_blk = kc_blk + 2 * NSA_KV_HEADS
    vsl_blk = kc_blk + 3 * NSA_KV_HEADS
    kw_blk = kc_blk + 4 * NSA_KV_HEADS
    vw_blk = kc_blk + 5 * NSA_KV_HEADS
    pw_sel, pw_win, pc = _bias_tables(rel_table, seq, tq, tq, tq_cmp)
    ms = _cmp_to_sel_matrix(seq, tq_cmp).astype(BF16)
    kc, vc = _compress(u, pos_k.astype(F32), pos_v.astype(F32),
                       w1k.astype(BF16), w2k.astype(BF16),
                       w1v.astype(BF16), w2v.astype(BF16), batch, seq, u_kc, tq_cmp)
    o_cmp, sel = _cmp_select(u, kc, vc, pc, ms, batch, seq, tq_cmp)
    o_sel = _flash(u, ksl_blk, vsl_blk, pw_sel, batch, seq, "sel", sel=sel)
    return _flash(u, kw_blk, vw_blk, pw_win, batch, seq, "win",
                  merge=(o_cmp, o_sel, small))


def kernel(x, g_mix_norm, w_in, w_alpha2, b_alpha, gla_norm_g, cmp_pos_k, cmp_pos_v,
           phi_k_w1, phi_k_w2, phi_v_w1, phi_v_w2, rel_bias_table, w_gla_proj,
           w_nsa_proj, w_out, g_mlp_norm, w_up, w_down, g_final_norm):
    batch, seq, d = x.shape
    t = batch * seq
    depth = w_in.shape[0]
    u_kc = U_MG + 2 * d
    xf = x.reshape(t, d)
    for l in range(depth):
        w_main_t, w_small_t = _w_in_prep(w_in, l, d)
        wa_pad = jnp.concatenate(
            [w_alpha2[l], jnp.zeros((LANES - GLA_GATE_RANK, GLA_KEY_W), w_alpha2.dtype)],
            axis=0).astype(F32)

        h, small = _rmsnorm_proj(xf, g_mix_norm[l], w_small_t)
        u = _matmul_nt(h, w_main_t, BF16, MM_TILE, MM_TILE, "in_proj")
        o_gla = _gla(u, small, wa_pad, b_alpha[l].reshape(1, -1).astype(F32),
                     gla_norm_g[l].reshape(1, -1).astype(F32), batch, seq)
        o_nsa = _nsa(u, small, cmp_pos_k[l], cmp_pos_v[l], phi_k_w1[l], phi_k_w2[l],
                     phi_v_w1[l], phi_v_w2[l], rel_bias_table, batch, seq, u_kc)
        mix = _mix(o_gla, w_gla_proj[l].astype(BF16), o_nsa, w_nsa_proj[l].astype(BF16), u, d)
        xf, xb, ssq = _out_proj(mix, w_out[l].astype(BF16), xf)
        w_up_g = (w_up[l] * g_mlp_norm[l].astype(F32)[:, None]).astype(BF16)
        act = _mlp_up(xb, w_up_g, ssq)
        xf = _matmul_residual(act, w_down[l].astype(BF16), xf, MM_TILE, MM_TILE, 4096, "mlp_down")
    out = _rmsnorm(xf, g_final_norm, F32)
    return out.reshape(batch, seq, d)
```

```python
import functools
import math

import numpy as np
import jax
import jax.numpy as jnp
from jax import lax
from jax.experimental import pallas as pl
from jax.experimental.pallas import tpu as pltpu

F32 = jnp.float32
BF16 = jnp.bfloat16
HIGHEST = lax.Precision.HIGHEST

NORM_EPS = 1e-6
GLA_HEADS = 16
GLA_DK = 64
GLA_DV = 128
GLA_KEY_W = GLA_HEADS * GLA_DK
GLA_VAL_W = GLA_HEADS * GLA_DV
GLA_GATE_RANK = 16
GLA_TAU = 16.0
GLA_CHUNK = 64
NSA_HEADS = 16
NSA_KV_HEADS = 4
NSA_GROUP = NSA_HEADS // NSA_KV_HEADS
NSA_DIM = 128
NSA_Q_W = NSA_HEADS * NSA_DIM
NSA_KV_W = NSA_KV_HEADS * NSA_DIM
CMP_BLOCK = 32
CMP_STRIDE = 16
SEL_BLOCK = 64
SEL_TOPK = 16
SEL_LOCAL = 2
WINDOW = 512
REL_BUCKETS = 32
REL_MAX_DIST = 128
NEG_INF = -1e30
FORCE_SCORE = 1e4
LOG2E = math.log2(math.e)

LANES = 128
VMEM_LIMIT = 56 * 1024 * 1024
VMEM_LIMIT_KTILED = 62 * 1024 * 1024
ATT_TILE = 256
CMP_TILE = 512
MM_TILE = 1024
GLA_PRE_BLOCK = 256
GLA_UNROLL = 16

U_GQ = 0
U_GK = U_GQ + GLA_KEY_W
U_GV = U_GK + GLA_KEY_W
U_GR = U_GV + GLA_VAL_W
U_NQ = U_GR + GLA_VAL_W
U_MG = U_NQ + NSA_Q_W

NT = (((1,), (1,)), ((), ()))
TN = (((0,), (0,)), ((), ()))


def _params(sem, vmem_limit=VMEM_LIMIT):
    return pltpu.CompilerParams(dimension_semantics=sem, vmem_limit_bytes=vmem_limit)


def _with_casts(kernel_fn, n_in, n_out, n_cast):
    def kernel(*refs, **kw):
        ins = refs[:n_in]
        cast_in = refs[n_in:n_in + n_cast]
        outs = refs[n_in + n_cast:n_in + n_cast + n_out]
        cast_out = refs[n_in + n_cast + n_out:n_in + 2 * n_cast + n_out]
        for src, dst in zip(cast_in, cast_out):
            dst[...] = src[0].astype(dst.dtype)
        kernel_fn(*ins, *outs, *refs[n_in + 2 * n_cast + n_out:], **kw)
    return kernel


def _cast_rows_ok(w, steps):
    rows = w.shape[1]
    return rows % steps == 0 and (rows // steps) % 16 == 0


def _cast_specs(weights, layer, steps, step_of):
    in_specs, out_specs, out_shapes = [], [], []
    for w in weights:
        rows, cols = w.shape[1:]
        assert _cast_rows_ok(w, steps)
        rp = rows // steps
        in_specs.append(pl.BlockSpec((1, rp, cols), lambda *g: (layer, step_of(*g), 0)))
        out_specs.append(pl.BlockSpec((rp, cols), lambda *g: (step_of(*g), 0)))
        out_shapes.append(jax.ShapeDtypeStruct((rows, cols), BF16))
    return in_specs, out_specs, out_shapes


def _rmsnorm_kernel(x_ref, g_ref, o_ref):
    x = x_ref[...].astype(F32)
    ms = jnp.mean(x * x, axis=-1, keepdims=True)
    o_ref[...] = (x * lax.rsqrt(ms + NORM_EPS) * g_ref[...]).astype(o_ref.dtype)


def _rmsnorm(x2, g, out_dtype, tm=512):
    t, d = x2.shape
    return pl.pallas_call(
        _rmsnorm_kernel,
        grid=(t // tm,),
        in_specs=[pl.BlockSpec((tm, d), lambda i: (i, 0)),
                  pl.BlockSpec((1, d), lambda i: (0, 0))],
        out_specs=pl.BlockSpec((tm, d), lambda i: (i, 0)),
        out_shape=jax.ShapeDtypeStruct((t, d), out_dtype),
        compiler_params=_params(("parallel",)),
        name="rmsnorm",
    )(x2, g.reshape(1, d).astype(F32))


def _rmsnorm_proj_kernel(x_ref, g_ref, wt_ref, o_ref, p_ref):
    x = x_ref[...].astype(F32)
    ms = jnp.mean(x * x, axis=-1, keepdims=True)
    h = (x * lax.rsqrt(ms + NORM_EPS) * g_ref[...]).astype(o_ref.dtype)
    o_ref[...] = h
    p_ref[...] = lax.dot_general(h, wt_ref[...], NT, preferred_element_type=F32)


def _rmsnorm_proj(x2, g, w_t, tm=512):
    t, d = x2.shape
    n = w_t.shape[0]
    return pl.pallas_call(
        _rmsnorm_proj_kernel,
        grid=(t // tm,),
        in_specs=[pl.BlockSpec((tm, d), lambda i: (i, 0)),
                  pl.BlockSpec((1, d), lambda i: (0, 0)),
                  pl.BlockSpec((n, d), lambda i: (0, 0))],
        out_specs=[pl.BlockSpec((tm, d), lambda i: (i, 0)),
                   pl.BlockSpec((tm, n), lambda i: (i, 0))],
        out_shape=[jax.ShapeDtypeStruct((t, d), BF16), jax.ShapeDtypeStruct((t, n), F32)],
        compiler_params=_params(("parallel",)),
        name="rmsnorm_proj",
    )(x2, g.reshape(1, d).astype(F32), w_t)


W_ROW_ALIGN = 16


def _w_in_prep_kernel(src_ref, w_ref, ga_ref, ng_ref, o_ref, os_ref, *, q_blocks, q_scale):
    del src_ref
    i = pl.program_id(0)
    is_q = jnp.logical_and(i >= q_blocks[0], i < q_blocks[1])
    scale = jnp.where(is_q, q_scale, 1.0).astype(F32)
    o_ref[...] = (w_ref[0] * scale).astype(o_ref.dtype)

    @pl.when(i == 0)
    def _():
        n_ga = ga_ref.shape[1]
        n_ng = ng_ref.shape[1]
        os_ref[0:n_ga] = ga_ref[0].astype(os_ref.dtype)
        os_ref[n_ga:n_ga + n_ng] = ng_ref[0].astype(os_ref.dtype)
        os_ref[n_ga + n_ng:] = jnp.zeros((os_ref.shape[0] - n_ga - n_ng, os_ref.shape[1]),
                                         os_ref.dtype)


def _w_in_prep(w_in, layer, d, tr=512):
    n = w_in.shape[2]
    s_ga = 2 * GLA_KEY_W + GLA_VAL_W
    s_gr = s_ga + GLA_GATE_RANK
    s_nq = s_gr + GLA_VAL_W
    s_kc = s_nq + NSA_Q_W
    s_ng = s_kc + 6 * NSA_KV_W
    s_mg = s_ng + 3 * NSA_HEADS
    assert n == s_mg + 2 * d
    w_t = jnp.swapaxes(w_in, 1, 2)
    pieces = [(0, s_ga, U_GQ), (s_gr, s_nq, U_GR), (s_nq, s_kc, U_NQ),
              (s_mg, n, U_MG), (s_kc, s_ng, U_MG + 2 * d)]
    src_rows = []
    for s0, s1, d0 in pieces:
        assert s0 % W_ROW_ALIGN == 0 and (s1 - s0) % tr == 0 and d0 == len(src_rows) * tr
        src_rows += [r // W_ROW_ALIGN for r in range(s0, s1, tr)]
    n_main = len(src_rows) * tr
    def rows_at(start, count):
        return pl.BlockSpec((pl.Element(1), pl.Element(count), pl.Element(d)),
                            lambda i, src: (layer, start, 0))

    return pl.pallas_call(
        functools.partial(_w_in_prep_kernel,
                          q_blocks=(U_NQ // tr, (U_NQ + NSA_Q_W) // tr),
                          q_scale=NSA_DIM ** -0.5 * LOG2E),
        grid_spec=pltpu.PrefetchScalarGridSpec(
            num_scalar_prefetch=1,
            grid=(n_main // tr,),
            in_specs=[pl.BlockSpec(
                (pl.Element(1), pl.Element(tr), pl.Element(d)),
                lambda i, src: (layer, src[i] * W_ROW_ALIGN, 0)),
                rows_at(s_ga, s_gr - s_ga), rows_at(s_ng, s_mg - s_ng)],
            out_specs=[pl.BlockSpec((tr, d), lambda i, src: (i, 0)),
                       pl.BlockSpec((LANES, d), lambda i, src: (0, 0))]),
        out_shape=[jax.ShapeDtypeStruct((n_main, d), BF16),
                   jax.ShapeDtypeStruct((LANES, d), BF16)],
        compiler_params=_params(("arbitrary",)),
        name="w_in_prep",
    )(jnp.asarray(src_rows, jnp.int32), w_t, w_t, w_t)


def _mm_nt_kernel(a_ref, bt_ref, o_ref):
    o_ref[...] = lax.dot_general(a_ref[...], bt_ref[...], NT,
                                 preferred_element_type=F32).astype(o_ref.dtype)


def _matmul_nt(a, b_t, out_dtype, tm, tn, name):
    m, k = a.shape
    n = b_t.shape[0]
    return pl.pallas_call(
        _mm_nt_kernel,
        grid=(m // tm, n // tn),
        in_specs=[pl.BlockSpec((tm, k), lambda i, j: (i, 0)),
                  pl.BlockSpec((tn, k), lambda i, j: (j, 0))],
        out_specs=pl.BlockSpec((tm, tn), lambda i, j: (i, j)),
        out_shape=jax.ShapeDtypeStruct((m, n), out_dtype),
        compiler_params=_params(("parallel", "arbitrary")),
        name=name,
    )(a, b_t)


def _mix_kernel(og_ref, wg_ref, on_ref, wn_ref, mg_ref, mn_ref, o_ref):
    yg = jnp.dot(og_ref[...], wg_ref[...], preferred_element_type=F32)
    yn = jnp.dot(on_ref[...], wn_ref[...], preferred_element_type=F32)
    o = (jax.nn.sigmoid(mg_ref[...].astype(F32)) * yg
         + jax.nn.sigmoid(mn_ref[...].astype(F32)) * yn)
    o_ref[...] = o.astype(o_ref.dtype)


def _mix(o_gla, w_g, o_nsa, w_n, u, d, tm=MM_TILE, tn=MM_TILE):
    t = o_gla.shape[0]
    mg_blk = U_MG // tn
    mn_blk = (U_MG + d) // tn
    return pl.pallas_call(
        _mix_kernel,
        grid=(t // tm, d // tn),
        in_specs=[pl.BlockSpec((tm, o_gla.shape[1]), lambda i, j: (i, 0)),
                  pl.BlockSpec((w_g.shape[0], tn), lambda i, j: (0, j)),
                  pl.BlockSpec((tm, o_nsa.shape[1]), lambda i, j: (i, 0)),
                  pl.BlockSpec((w_n.shape[0], tn), lambda i, j: (0, j)),
                  pl.BlockSpec((tm, tn), lambda i, j: (i, mg_blk + j)),
                  pl.BlockSpec((tm, tn), lambda i, j: (i, mn_blk + j))],
        out_specs=pl.BlockSpec((tm, tn), lambda i, j: (i, j)),
        out_shape=jax.ShapeDtypeStruct((t, d), BF16),
        compiler_params=_params(("parallel", "arbitrary")),
        name="mix",
    )(o_gla, w_g, o_nsa, w_n, u, u)


def _mm_res_kernel(a_ref, b_ref, r_ref, o_ref, acc_ref):
    kk = pl.program_id(2)

    @pl.when(kk == 0)
    def _():
        acc_ref[...] = r_ref[...]

    acc_ref[...] += jnp.dot(a_ref[...], b_ref[...], preferred_element_type=F32)

    @pl.when(kk == pl.num_programs(2) - 1)
    def _():
        o_ref[...] = acc_ref[...]


def _out_proj_kernel(a_ref, b_ref, r_ref, g_ref, o_ref, ob_ref, ss_ref):
    y = r_ref[...] + jnp.dot(a_ref[...], b_ref[...], preferred_element_type=F32)
    o_ref[...] = y
    ob_ref[...] = (y * g_ref[...]).astype(ob_ref.dtype)
    ss_ref[...] = jnp.broadcast_to(jnp.sum(y * y, axis=-1, keepdims=True), ss_ref.shape)


def _out_proj(a, b, res, gain, tm=MM_TILE, tn=MM_TILE):
    m, k = a.shape
    _, n = b.shape
    return pl.pallas_call(
        _out_proj_kernel,
        grid=(m // tm, n // tn),
        in_specs=[pl.BlockSpec((tm, k), lambda i, j: (i, 0)),
                  pl.BlockSpec((k, tn), lambda i, j: (0, j)),
                  pl.BlockSpec((tm, tn), lambda i, j: (i, j)),
                  pl.BlockSpec((1, tn), lambda i, j: (0, j))],
        out_specs=[pl.BlockSpec((tm, tn), lambda i, j: (i, j)),
                   pl.BlockSpec((tm, tn), lambda i, j: (i, j)),
                   pl.BlockSpec((tm, LANES), lambda i, j: (i, j))],
        out_shape=[jax.ShapeDtypeStruct((m, n), F32), jax.ShapeDtypeStruct((m, n), BF16),
                   jax.ShapeDtypeStruct((m, n // tn * LANES), F32)],
        compiler_params=_params(("parallel", "arbitrary"), VMEM_LIMIT_KTILED),
        name="out_proj",
    )(a, b, res, gain.reshape(1, n).astype(F32))


def _mlp_up_kernel(a_ref, b_ref, ss_ref, o_ref, *, d):
    acc = jnp.dot(a_ref[...], b_ref[...], preferred_element_type=F32)
    ss = ss_ref[...]
    total = ss[:, :LANES]
    for p in range(1, ss.shape[1] // LANES):
        total = total + ss[:, p * LANES:(p + 1) * LANES]
    inv = 1.0 / (total[:, :1] * (1.0 / d) + NORM_EPS)
    r = jnp.maximum(acc, 0.0)
    o_ref[...] = (r * r * inv).astype(o_ref.dtype)


def _mlp_up(xb, w_up, ssq, tm=MM_TILE, tn=MM_TILE):
    m, k = xb.shape
    _, n = w_up.shape
    return pl.pallas_call(
        functools.partial(_mlp_up_kernel, d=k),
        grid=(m // tm, n // tn),
        in_specs=[pl.BlockSpec((tm, k), lambda i, j: (i, 0)),
                  pl.BlockSpec((k, tn), lambda i, j: (0, j)),
                  pl.BlockSpec((tm, ssq.shape[1]), lambda i, j: (i, 0))],
        out_specs=pl.BlockSpec((tm, tn), lambda i, j: (i, j)),
        out_shape=jax.ShapeDtypeStruct((m, n), BF16),
        compiler_params=_params(("parallel", "arbitrary")),
        name="mlp_up",
    )(xb, w_up, ssq)


def _matmul_residual(a, b, res, tm, tn, tk, name):
    m, k = a.shape
    _, n = b.shape
    return pl.pallas_call(
        _mm_res_kernel,
        grid=(m // tm, n // tn, k // tk),
        in_specs=[pl.BlockSpec((tm, tk), lambda i, j, kk: (i, kk)),
                  pl.BlockSpec((tk, tn), lambda i, j, kk: (kk, j)),
                  pl.BlockSpec((tm, tn), lambda i, j, kk: (i, j))],
        out_specs=pl.BlockSpec((tm, tn), lambda i, j, kk: (i, j)),
        out_shape=jax.ShapeDtypeStruct((m, n), F32),
        scratch_shapes=[pltpu.VMEM((tm, tn), F32)],
        compiler_params=_params(("parallel", "arbitrary", "arbitrary"), VMEM_LIMIT_KTILED),
        name=name,
    )(a, b, res)


def _gla_kernel(q_ref, k_ref, v_ref, r_ref, a_ref, wa_ref, ba_ref, g_ref, o_ref,
                bc_ref, st_ref, *, seq):
    c = GLA_CHUNK
    dv = GLA_DV
    shift = int(math.log2(c))

    nb = GLA_PRE_BLOCK
    prow = lax.broadcasted_iota(jnp.int32, (nb, nb), 0)
    pcol = lax.broadcasted_iota(jnp.int32, (nb, nb), 1)
    same_chunk = lax.shift_right_logical(prow, shift) == lax.shift_right_logical(pcol, shift)
    tri = jnp.where(prow >= pcol, jnp.where(same_chunk, 1.0, 0.0), 0.0).astype(BF16)

    def split(x):
        hi = x.astype(BF16)
        return hi, (x - hi.astype(F32)).astype(BF16)

    wa_hi, wa_lo = split(wa_ref[...])
    wa3 = jnp.concatenate([wa_hi, wa_hi, wa_lo], axis=0)

    def log_decay(b, carry):
        r0 = pl.multiple_of(b * nb, nb)
        a_hi, a_lo = split(a_ref[pl.ds(r0, nb), :])
        z = jnp.dot(jnp.concatenate([a_hi, a_lo, a_hi], axis=1), wa3,
                    preferred_element_type=F32) + ba_ref[...]
        bc_ref[pl.ds(r0, nb), :] = (
            jnp.minimum(z, 0.0) - jnp.log(1.0 + jnp.exp(-jnp.abs(z)))) * (1.0 / GLA_TAU)
        return carry

    def cumulate(b, carry):
        r0 = pl.multiple_of(b * nb, nb)
        la = bc_ref[pl.ds(r0, nb), :]
        cs = jnp.dot(tri, jnp.concatenate(split(la), axis=1), preferred_element_type=F32)
        w = la.shape[1]
        bc_ref[pl.ds(r0, nb), :] = cs[:, :w] + cs[:, w:]
        return carry

    lax.fori_loop(0, seq // nb, log_decay, 0, unroll=8)
    lax.fori_loop(0, seq // nb, cumulate, 0, unroll=8)
    st_ref[...] = jnp.zeros_like(st_ref)

    row2 = lax.broadcasted_iota(jnp.int32, (2 * c, c), 0)
    col2 = lax.broadcasted_iota(jnp.int32, (2 * c, c), 1)
    causal2 = (row2 & (c - 1)) >= col2
    lane = lax.broadcasted_iota(jnp.int32, (1, 2 * GLA_DK), 1)
    head_mask = [(lane < GLA_DK).astype(F32), (lane >= GLA_DK).astype(F32)]
    gvec = g_ref[...]

    def body(i, carry):
        r0 = pl.multiple_of(i * c, c)
        bc = bc_ref[pl.ds(r0, c), :]
        last = bc[c - 1:c, :]
        q = q_ref[pl.ds(r0, c), :].astype(F32) * (GLA_DK ** -0.5)
        k = k_ref[pl.ds(r0, c), :].astype(F32)
        qe = q * jnp.exp(bc)
        ke = (k * jnp.exp(-bc)).astype(BF16)
        kd = (k * jnp.exp(last - bc)).astype(BF16)
        v = v_ref[pl.ds(r0, c), :]
        st = st_ref[...]
        q2 = jnp.concatenate([qe * head_mask[0], qe * head_mask[1]], axis=0).astype(BF16)
        att = lax.dot_general(q2, ke, NT, preferred_element_type=F32)
        att = jnp.where(causal2, att, 0.0).astype(BF16)
        o2 = (jnp.dot(att, v, preferred_element_type=F32)
              + lax.dot_general(q2, st.astype(BF16), NT, preferred_element_type=F32))
        outs = []
        for h in range(2):
            oh = o2[h * c:(h + 1) * c, h * dv:(h + 1) * dv]
            ms = jnp.mean(oh * oh, axis=-1, keepdims=True)
            y = oh * lax.rsqrt(ms + NORM_EPS) * gvec[:, h * dv:(h + 1) * dv]
            rr = r_ref[pl.ds(r0, c), h * dv:(h + 1) * dv].astype(F32)
            outs.append(y * (rr * jax.nn.sigmoid(rr)))
        o_ref[pl.ds(r0, c), :] = jnp.concatenate(outs, axis=1).astype(o_ref.dtype)
        kv = lax.dot_general(v, kd, TN, preferred_element_type=F32)
        st_ref[...] = st * jnp.exp(last) + kv
        return carry

    lax.fori_loop(0, seq // c, body, 0, unroll=GLA_UNROLL)


def _gla(u, small, w_alpha_pad, b_alpha, norm_g, batch, seq, casts=(), layer=0):
    t = batch * seq
    pairs = GLA_HEADS // 2
    kw = 2 * GLA_DK
    vw = 2 * GLA_DV
    c_in, c_out, c_shape = _cast_specs(casts, layer, batch * pairs, lambda b, j: b * pairs + j)
    return pl.pallas_call(
        _with_casts(functools.partial(_gla_kernel, seq=seq), 8, 1, len(casts)),
        grid=(batch, pairs),
        in_specs=[pl.BlockSpec((seq, kw), lambda b, j: (b, U_GQ // kw + j)),
                  pl.BlockSpec((seq, kw), lambda b, j: (b, U_GK // kw + j)),
                  pl.BlockSpec((seq, vw), lambda b, j: (b, U_GV // vw + j)),
                  pl.BlockSpec((seq, vw), lambda b, j: (b, U_GR // vw + j)),
                  pl.BlockSpec((seq, LANES), lambda b, j: (b, 0)),
                  pl.BlockSpec((LANES, kw), lambda b, j: (0, j)),
                  pl.BlockSpec((1, kw), lambda b, j: (0, j)),
                  pl.BlockSpec((1, vw), lambda b, j: (0, j))] + c_in,
        out_specs=[pl.BlockSpec((seq, vw), lambda b, j: (b, j))] + c_out,
        out_shape=[jax.ShapeDtypeStruct((t, GLA_VAL_W), BF16)] + c_shape,
        scratch_shapes=[pltpu.VMEM((seq, kw), F32), pltpu.VMEM((vw, kw), F32)],
        compiler_params=_params(("parallel", "arbitrary")),
        name="gla",
    )(u, u, u, u, small, w_alpha_pad, b_alpha, norm_g, *casts)


def _cmp_cols_per_tile(tq):
    return tq // CMP_STRIDE


def _cmp_front_pad(seq, tq):
    return seq // CMP_STRIDE - _cmp_cols_per_tile(tq)


def _compress_kernel(xk_ref, xv_ref, pk_ref, pv_ref, w1k_ref, w2k_ref, w1v_ref, w2v_ref,
                     ok_ref, ov_ref, xf_ref, *, front):
    n = CMP_STRIDE
    d = NSA_DIM

    def one(x_ref, pos_ref, w1_ref, w2_ref, o_ref):
        nb = x_ref.shape[0] // n
        xf_ref[...] = x_ref[...].astype(F32)
        top = jnp.zeros((nb, w1_ref.shape[1]), F32)
        bot = jnp.zeros((nb, w1_ref.shape[1]), F32)
        for j in range(n):
            x = xf_ref[pl.ds(j, nb, stride=n), :]
            xa = (x + pos_ref[j:j + 1, :]).astype(BF16)
            xb = (x + pos_ref[n + j:n + j + 1, :]).astype(BF16)
            top = top + jnp.dot(xa, w1_ref[j * d:(j + 1) * d, :],
                                preferred_element_type=F32)
            bot = bot + jnp.dot(xb, w1_ref[(n + j) * d:(n + j + 1) * d, :],
                                preferred_element_type=F32)
        hid = top + pltpu.roll(bot, nb - 1, axis=0)
        hid = jnp.maximum(hid, 0.0).astype(BF16)
        total = o_ref.shape[2]
        o_ref[0, 0, 0:front] = jnp.zeros((front, d), o_ref.dtype)
        o_ref[0, 0, front:front + nb] = jnp.dot(
            hid, w2_ref[...], preferred_element_type=F32).astype(o_ref.dtype)
        o_ref[0, 0, front + nb:total] = jnp.zeros((total - front - nb, d), o_ref.dtype)

    one(xk_ref, pk_ref, w1k_ref, w2k_ref, ok_ref)
    one(xv_ref, pv_ref, w1v_ref, w2v_ref, ov_ref)


def _compress(u, pos_k, pos_v, w1k, w2k, w1v, w2v, batch, seq, u_kc, tq):
    nb = seq // CMP_STRIDE
    front = _cmp_front_pad(seq, tq)
    kc_blk = u_kc // NSA_DIM
    vc_blk = kc_blk + NSA_KV_HEADS
    full = lambda a: pl.BlockSpec(a.shape, lambda b, g: (0,) * a.ndim)
    out = jax.ShapeDtypeStruct((batch, NSA_KV_HEADS, 2 * nb, NSA_DIM), BF16)
    ospec = pl.BlockSpec((1, 1, 2 * nb, NSA_DIM), lambda b, g: (b, g, 0, 0))
    return pl.pallas_call(
        functools.partial(_compress_kernel, front=front),
        grid=(batch, NSA_KV_HEADS),
        in_specs=[pl.BlockSpec((seq, NSA_DIM), lambda b, g: (b, kc_blk + g)),
                  pl.BlockSpec((seq, NSA_DIM), lambda b, g: (b, vc_blk + g)),
                  full(pos_k), full(pos_v), full(w1k), full(w2k), full(w1v), full(w2v)],
        out_specs=[ospec, ospec],
        out_shape=[out, out],
        scratch_shapes=[pltpu.VMEM((seq, NSA_DIM), F32)],
        compiler_params=_params(("parallel", "arbitrary")),
        name="nsa_compress",
    )(u, u, pos_k, pos_v, w1k, w2k, w1v, w2v)


def _cmp_select_kernel(q_ref, kc_ref, vc_ref, pc_ref, ms_ref, o_ref, sel_ref, score_ref, *,
                       tq, nb, n_s, top_n, front):
    i = pl.program_id(2)
    d = NSA_DIM
    w0 = pl.multiple_of(i * _cmp_cols_per_tile(tq), _cmp_cols_per_tile(tq))
    kc = kc_ref[0, 0, pl.ds(w0, nb), :]
    vc = vc_ref[0, 0, pl.ds(w0, nb), :]
    rows = NSA_GROUP * tq
    q = jnp.concatenate([q_ref[:, r * d:(r + 1) * d] for r in range(NSA_GROUP)], axis=0)
    col = lax.broadcasted_iota(jnp.int32, (rows, nb), 1)
    before_start = jnp.where(col < front - w0, NEG_INF, 0.0)
    s = lax.dot_general(q, kc, NT, preferred_element_type=F32) + before_start
    bias = pc_ref[...].reshape(rows, LANES)
    s = jnp.concatenate([s[:, :nb - LANES], s[:, nb - LANES:] + bias], axis=1)
    m = jnp.max(s, axis=-1, keepdims=True)
    e = jnp.exp2(s - m)
    p = e * (1.0 / jnp.sum(e, axis=-1, keepdims=True))
    p = jnp.where(s > 0.5 * NEG_INF, p, 0.0)
    o = jnp.dot(p.astype(BF16), vc, preferred_element_type=F32)
    psum = p[0:tq]
    for r in range(NSA_GROUP):
        o_ref[:, r * d:(r + 1) * d] = o[r * tq:(r + 1) * tq].astype(o_ref.dtype)
        if r:
            psum = psum + p[r * tq:(r + 1) * tq]
    p_hi = psum.astype(BF16)
    p_lo = (psum - p_hi.astype(F32)).astype(BF16)
    imp2 = lax.dot_general(ms_ref[pl.ds(w0, nb), :], jnp.concatenate([p_hi, p_lo], axis=0),
                           (((0,), (1,)), ((), ())), preferred_element_type=F32)
    imp = imp2[:, :tq] + imp2[:, tq:]
    blk = lax.broadcasted_iota(jnp.int32, (n_s, tq), 0)
    pos = i * tq + lax.broadcasted_iota(jnp.int32, (n_s, tq), 1)
    cur = lax.shift_right_logical(pos, int(math.log2(SEL_BLOCK)))
    forced = jnp.where(blk == 0, 1, jnp.where(blk <= cur, jnp.where(blk > cur - SEL_LOCAL, 1, 0), 0))
    score = jnp.where(forced == 1, FORCE_SCORE, jnp.where(blk <= cur, imp, -FORCE_SCORE))
    score_ref[...] = score
    per_trip = tq // SEL_BLOCK

    def count_ahead(g, rank):
        for r in range(per_trip):
            j = g * per_trip + r
            rj = score_ref[pl.ds(j, 1), :]
            tie = jnp.where(rj == score, jnp.where(blk > j, 1, 0), 0)
            rank = rank + jnp.where(rj > score, 1, tie)
        return rank

    rank = lax.fori_loop(0, i + 1, count_ahead, jnp.zeros((n_s, tq), jnp.int32))
    selb = jnp.where(rank < top_n, 0.0, NEG_INF)
    if n_s < LANES:
        selb = jnp.concatenate([selb, jnp.zeros((LANES - n_s, tq), F32)], axis=0)
    sel_ref[0, 0] = selb.T.astype(sel_ref.dtype)


def _cmp_select(u, kc, vc, pc, ms, batch, seq, tq):
    t = batch * seq
    nb = seq // CMP_STRIDE
    n_s = seq // SEL_BLOCK
    top_n = min(SEL_TOPK, n_s)
    nq = seq // tq
    qw = NSA_GROUP * NSA_DIM
    kv_spec = pl.BlockSpec((1, 1, 2 * nb, NSA_DIM), lambda b, g, i: (b, g, 0, 0))
    return pl.pallas_call(
        functools.partial(_cmp_select_kernel, tq=tq, nb=nb, n_s=n_s, top_n=top_n,
                          front=_cmp_front_pad(seq, tq)),
        grid=(batch, NSA_KV_HEADS, nq),
        in_specs=[pl.BlockSpec((tq, qw), lambda b, g, i: (b * nq + i, U_NQ // qw + g)),
                  kv_spec, kv_spec,
                  pl.BlockSpec((NSA_GROUP, tq, LANES), lambda b, g, i: (g, 0, 0)),
                  pl.BlockSpec((2 * nb, n_s), lambda b, g, i: (0, 0))],
        out_specs=[pl.BlockSpec((tq, qw), lambda b, g, i: (b * nq + i, g)),
                   pl.BlockSpec((1, 1, tq, LANES), lambda b, g, i: (b, g, i, 0))],
        out_shape=[jax.ShapeDtypeStruct((t, NSA_Q_W), BF16),
                   jax.ShapeDtypeStruct((batch, NSA_KV_HEADS, seq, LANES), BF16)],
        scratch_shapes=[pltpu.VMEM((n_s, tq), F32)],
        compiler_params=_params(("parallel", "parallel", "arbitrary")),
        name="nsa_cmp_select",
    )(u, kc, vc, pc, ms)


def _flash_kernel(*refs, mode, tq, tk):
    sel = mode == "sel"
    if sel:
        (q_ref, k_ref, v_ref, sel_ref, pw_ref, o_ref,
         qs_ref, ks_ref, m_ref, l_ref, acc_ref) = refs
    else:
        (q_ref, k_ref, v_ref, pw_ref, oc_ref, os_ref, ng_ref, o_ref, vs_ref) = refs
    i = pl.program_id(2)
    d = NSA_DIM

    if sel:
        @pl.when(i == 0)
        def _():
            seq = k_ref.shape[0]
            ks_ref[:, :d] = k_ref[...]
            krow = lax.broadcasted_iota(jnp.int32, (seq, LANES), 0)
            klane = lax.broadcasted_iota(jnp.int32, (seq, LANES), 1)
            kblk = lax.shift_right_logical(krow, int(math.log2(SEL_BLOCK)))
            ks_ref[:, d:] = jnp.where(kblk == klane, 1.0, 0.0).astype(ks_ref.dtype)

        for r in range(NSA_GROUP):
            qs_ref[r * tq:(r + 1) * tq, :d] = q_ref[:, r * d:(r + 1) * d]
            qs_ref[r * tq:(r + 1) * tq, d:] = sel_ref[0, 0]
        m_ref[...] = jnp.full_like(m_ref, NEG_INF)
        l_ref[...] = jnp.zeros_like(l_ref)
        acc_ref[...] = jnp.zeros_like(acc_ref)

    if not sel:
        group = pl.program_id(1)
        per_group = 3 * NSA_GROUP
        shift = lax.rem(LANES - per_group * group, LANES)
        sig = pltpu.roll(jax.nn.sigmoid(ng_ref[...]), shift, axis=1)

        def gate(h, br):
            c = GLA_GATE_RANK + 3 * h + br
            return jnp.broadcast_to(sig[:, c:c + 1], (tq, d))

        @pl.when(i == 0)
        def _():
            vs_ref[:, :d] = v_ref[...]
            vs_ref[:, d:] = jnp.ones((v_ref.shape[0], d), vs_ref.dtype)

    chains = [(0, NSA_GROUP)] if sel else [(r, r + 1) for r in range(NSA_GROUP)]

    def chain_logits(h0, h1, kt):
        q = qs_ref[h0 * tq:h1 * tq, :] if sel else q_ref[:, h0 * d:h1 * d]
        return lax.dot_general(q, kt, NT, preferred_element_type=F32)

    def online_update(rows, s, vt):
        m_prev = m_ref[rows]
        m_new = jnp.maximum(m_prev, jnp.max(s, axis=-1, keepdims=True))
        alpha = jnp.exp2(m_prev - m_new)
        p = jnp.exp2(s - jnp.concatenate([m_new] * (s.shape[1] // LANES), axis=1))
        l_new = alpha * l_ref[rows] + jnp.sum(p, axis=-1, keepdims=True)
        acc_new = alpha * acc_ref[rows] + jnp.dot(p.astype(BF16), vt,
                                                  preferred_element_type=F32)
        return m_new, l_new, acc_new

    def last_tiles(n):
        width = n * tk
        k0 = pl.multiple_of((i + 1 - n) * tk, tk)
        kt = ks_ref[pl.ds(k0, width), :] if sel else k_ref[pl.ds(k0, width), :]
        vt = v_ref[pl.ds(k0, width), :] if sel else vs_ref[pl.ds(k0, width), :]
        for h0, h1 in chains:
            rows = slice(h0 * tq, h1 * tq)
            s = chain_logits(h0, h1, kt) + pw_ref[0, rows, (3 - n) * tk:]
            if sel:
                _, l, acc = online_update(rows, s, vt)
            else:
                p = jnp.exp2(s - jnp.max(s, axis=-1, keepdims=True))
                acc = jnp.dot(p.astype(BF16), vt, preferred_element_type=F32)
                acc, l = acc[:, :d], acc[:, d:]
            out = acc * (1.0 / l)
            for h in range(h0, h1):
                out_h = out[(h - h0) * tq:(h - h0 + 1) * tq, :]
                if not sel:
                    lanes = slice(h * d, (h + 1) * d)
                    out_h = (gate(h, 2) * out_h
                             + gate(h, 0) * oc_ref[:, lanes].astype(F32)
                             + gate(h, 1) * os_ref[:, lanes].astype(F32))
                o_ref[:, h * d:(h + 1) * d] = out_h.astype(o_ref.dtype)

    if sel:
        n_far = jnp.maximum(i - 1, 0)
        odd = n_far % 2

        def far_pair(j):
            k0 = pl.multiple_of(j * (2 * tk), 2 * tk)
            kt = ks_ref[pl.ds(k0, 2 * tk), :]
            vt = v_ref[pl.ds(k0, 2 * tk), :]
            for h0, h1 in chains:
                rows = slice(h0 * tq, h1 * tq)
                m_ref[rows], l_ref[rows], acc_ref[rows] = online_update(
                    rows, chain_logits(h0, h1, kt), vt)

        n_pairs = n_far // 2

        def far(j, carry):
            far_pair(2 * j)
            far_pair(2 * j + 1)
            return carry
        lax.fori_loop(0, n_pairs // 2, far, 0)
        pl.when(n_pairs % 2 == 1)(lambda: far_pair(n_pairs - 1))
        pl.when(i == 0)(lambda: last_tiles(1))
        pl.when(jnp.logical_and(i >= 1, odd == 0))(lambda: last_tiles(2))
        pl.when(odd == 1)(lambda: last_tiles(3))
    else:
        pl.when(i == 0)(lambda: last_tiles(1))
        pl.when(i == 1)(lambda: last_tiles(2))
        pl.when(i >= 2)(lambda: last_tiles(3))


def _flash(u, k_blk, v_blk, pw, batch, seq, mode, sel=None, merge=None, casts=(), layer=0,
           tq=ATT_TILE, tk=ATT_TILE):
    assert tq == tk and 2 * tk >= REL_MAX_DIST and 3 * tk > WINDOW >= 2 * tk
    t = batch * seq
    nq = seq // tq
    qw = NSA_GROUP * NSA_DIM
    rows = NSA_GROUP * tq
    d = NSA_DIM
    q_spec = pl.BlockSpec((tq, qw), lambda b, g, i: (b * nq + i, U_NQ // qw + g))
    k_spec = pl.BlockSpec((seq, d), lambda b, g, i: (b, k_blk + g))
    v_spec = pl.BlockSpec((seq, d), lambda b, g, i: (b, v_blk + g))
    b_spec = pl.BlockSpec((1, rows, 3 * tk), lambda b, g, i: (g, 0, 0))
    if mode == "sel":
        in_specs = [q_spec, k_spec, v_spec,
                    pl.BlockSpec((1, 1, tq, LANES), lambda b, g, i: (b, g, i, 0)), b_spec]
        args = (u, u, u, sel, pw)
        scratch = ([pltpu.VMEM((rows, 2 * d), BF16), pltpu.VMEM((seq, 2 * d), BF16)]
                   + [pltpu.VMEM((rows, LANES), F32)] * 3)
    else:
        o_spec = pl.BlockSpec((tq, qw), lambda b, g, i: (b * nq + i, g))
        o_cmp, o_sel, gate_logits = merge
        in_specs = [q_spec, k_spec, v_spec, b_spec, o_spec, o_spec,
                    pl.BlockSpec((tq, LANES), lambda b, g, i: (b * nq + i, 0))]
        args = (u, u, u, pw, o_cmp, o_sel, gate_logits)
        scratch = [pltpu.VMEM((seq, 2 * d), BF16)]
    c_in, c_out, c_shape = _cast_specs(
        casts, layer, batch * NSA_KV_HEADS * nq,
        lambda b, g, i: (b * NSA_KV_HEADS + g) * nq + i)
    return pl.pallas_call(
        _with_casts(functools.partial(_flash_kernel, mode=mode, tq=tq, tk=tk),
                    len(in_specs), 1, len(casts)),
        grid=(batch, NSA_KV_HEADS, nq),
        in_specs=in_specs + c_in,
        out_specs=[pl.BlockSpec((tq, qw), lambda b, g, i: (b * nq + i, g))] + c_out,
        out_shape=[jax.ShapeDtypeStruct((t, NSA_Q_W), BF16)] + c_shape,
        scratch_shapes=scratch,
        compiler_params=_params(("parallel", "parallel", "arbitrary")),
        name="nsa_flash_" + mode,
    )(*args, *casts)


def _rel_bucket(dist):
    n = jnp.maximum(dist, 0)
    max_exact = REL_BUCKETS // 2
    nf = jnp.maximum(n, max_exact).astype(F32)
    large = max_exact + (jnp.log(nf / max_exact) / math.log(REL_MAX_DIST / max_exact)
                         * (REL_BUCKETS - max_exact)).astype(jnp.int32)
    large = jnp.minimum(large, REL_BUCKETS - 1)
    return jnp.where(n < max_exact, n, large)


def _bias_by_distance(table, dist):
    onehot = (_rel_bucket(jnp.asarray(dist, jnp.int32))[..., None]
              == jnp.arange(REL_BUCKETS, dtype=jnp.int32)).astype(F32)
    return jnp.dot(onehot, table, precision=HIGHEST)


def _bias_tables(rel_table, seq, tq, tk, tq_cmp):
    table = rel_table.astype(F32) * LOG2E
    heads = table.shape[1]
    far = table[REL_BUCKETS - 1]
    period = 3 * tk + 1
    vec = _bias_by_distance(table, np.arange(2 * tk)) - far
    vec = jnp.concatenate([vec, jnp.full((period - 2 * tk, heads), NEG_INF, F32)], axis=0).T
    skew = jnp.tile(vec, (1, tk))[:, :tk * (period - 1)].reshape(heads, tk, period - 1)
    tiles = skew[:, :, :2 * tq].transpose(0, 2, 1)
    p0 = tiles[:, :tq].reshape(NSA_KV_HEADS, NSA_GROUP * tq, tk)
    p1 = tiles[:, tq:].reshape(NSA_KV_HEADS, NSA_GROUP * tq, tk)
    a2 = (np.arange(NSA_GROUP * tq) % tq)[:, None]
    edge = np.where(2 * tk + a2 - np.arange(tk)[None, :] < WINDOW, 0.0, NEG_INF).astype(np.float32)
    edge = jnp.broadcast_to(jnp.asarray(edge), p0.shape)
    pw_sel = jnp.concatenate([jnp.zeros_like(p0), p1, p0], axis=2)
    pw_win = jnp.concatenate([edge, p1, p0], axis=2)
    front = _cmp_front_pad(seq, tq_cmp)
    nb = seq // CMP_STRIDE
    a = np.arange(tq_cmp)[:, None]
    rel_blk = np.arange(nb - LANES, nb)[None, :] - front
    dc = a - CMP_STRIDE * rel_blk - (CMP_BLOCK - 1)
    assert (a - CMP_STRIDE * (nb - LANES - 1 - front) - (CMP_BLOCK - 1)).min() >= REL_MAX_DIST
    pc = _bias_by_distance(table, np.maximum(dc, 0)) - far
    pc = jnp.where(jnp.asarray(dc >= 0)[..., None], pc, NEG_INF).transpose(2, 0, 1)
    return pw_sel, pw_win, pc


def _cmp_to_sel_matrix(seq, tq):
    nb = seq // CMP_STRIDE
    n_c = nb - 1
    n_s = seq // SEL_BLOCK
    front = _cmp_front_pad(seq, tq)
    m_mat = np.zeros((2 * nb, n_s), np.float32)
    j = np.arange(n_s)
    for m in range(SEL_BLOCK // CMP_STRIDE):
        for n in range(CMP_BLOCK // CMP_STRIDE):
            c = (SEL_BLOCK // CMP_STRIDE) * j + m - n
            ok = (c >= 0) & (c < n_c)
            np.add.at(m_mat, (front + c[ok], j[ok]), 1.0)
    return jnp.asarray(m_mat)


def _nsa(u, small, pos_k, pos_v, w1k, w2k, w1v, w2v, rel_table, batch, seq, u_kc,
         sel_casts=(), win_casts=(), layer=0):
    tq = ATT_TILE
    tq_cmp = min(CMP_TILE, seq)
    kc_blk = u_kc // NSA_DIM
    ksl_blk = kc_blk + 2 * NSA_KV_HEADS
    vsl_blk = kc_blk + 3 * NSA_KV_HEADS
    kw_blk = kc_blk + 4 * NSA_KV_HEADS
    vw_blk = kc_blk + 5 * NSA_KV_HEADS
    pw_sel, pw_win, pc = _bias_tables(rel_table, seq, tq, tq, tq_cmp)
    ms = _cmp_to_sel_matrix(seq, tq_cmp).astype(BF16)
    kc, vc = _compress(u, pos_k.astype(F32), pos_v.astype(F32),
                       w1k.astype(BF16), w2k.astype(BF16),
                       w1v.astype(BF16), w2v.astype(BF16), batch, seq, u_kc, tq_cmp)
    o_cmp, sel = _cmp_select(u, kc, vc, pc, ms, batch, seq, tq_cmp)
    o_sel, *sel_cast = _flash(u, ksl_blk, vsl_blk, pw_sel, batch, seq, "sel", sel=sel,
                              casts=sel_casts, layer=layer)
    o_nsa, *win_cast = _flash(u, kw_blk, vw_blk, pw_win, batch, seq, "win",
                              merge=(o_cmp, o_sel, small), casts=win_casts, layer=layer)
    return o_nsa, sel_cast, win_cast


def kernel(x, g_mix_norm, w_in, w_alpha2, b_alpha, gla_norm_g, cmp_pos_k, cmp_pos_v,
           phi_k_w1, phi_k_w2, phi_v_w1, phi_v_w2, rel_bias_table, w_gla_proj,
           w_nsa_proj, w_out, g_mlp_norm, w_up, w_down, g_final_norm):
    batch, seq, d = x.shape
    t = batch * seq
    depth = w_in.shape[0]
    u_kc = U_MG + 2 * d
    xf = x.reshape(t, d)
    for l in range(depth):
        w_main_t, w_small_t = _w_in_prep(w_in, l, d)
        wa_pad = jnp.concatenate(
            [w_alpha2[l], jnp.zeros((LANES - GLA_GATE_RANK, GLA_KEY_W), w_alpha2.dtype)],
            axis=0).astype(F32)

        h, small = _rmsnorm_proj(xf, g_mix_norm[l], w_small_t)
        u = _matmul_nt(h, w_main_t, BF16, MM_TILE, MM_TILE, "in_proj")
        nq = seq // ATT_TILE
        gla_steps = batch * (GLA_HEADS // 2)
        att_steps = batch * NSA_KV_HEADS * nq
        gla_casts = [w for w in (w_gla_proj, w_nsa_proj, w_out) if _cast_rows_ok(w, gla_steps)]
        sel_casts = [w for w in (w_up,) if _cast_rows_ok(w, att_steps)]
        win_casts = [w for w in (w_down,) if _cast_rows_ok(w, att_steps)]
        o_gla, *gla_cast = _gla(u, small, wa_pad, b_alpha[l].reshape(1, -1).astype(F32),
                                gla_norm_g[l].reshape(1, -1).astype(F32), batch, seq,
                                casts=gla_casts, layer=l)
        o_nsa, sel_cast, win_cast = _nsa(
            u, small, cmp_pos_k[l], cmp_pos_v[l], phi_k_w1[l], phi_k_w2[l], phi_v_w1[l],
            phi_v_w2[l], rel_bias_table, batch, seq, u_kc, sel_casts, win_casts, l)
        done = {id(w): c for w, c in zip(gla_casts + sel_casts + win_casts,
                                         gla_cast + sel_cast + win_cast)}
        bf16 = lambda w: done[id(w)] if id(w) in done else w[l].astype(BF16)
        mix = _mix(o_gla, bf16(w_gla_proj), o_nsa, bf16(w_nsa_proj), u, d)
        xf, xb, ssq = _out_proj(mix, bf16(w_out), xf, g_mlp_norm[l])
        act = _mlp_up(xb, bf16(w_up), ssq)
        xf = _matmul_residual(act, bf16(w_down), xf, MM_TILE, MM_TILE, 4096, "mlp_down")
    out = _rmsnorm(xf, g_final_norm, F32)
    return out.reshape(batch, seq, d)
```

```python
import functools
import math

import numpy as np
import jax
import jax.numpy as jnp
from jax import lax
from jax.experimental import pallas as pl
from jax.experimental.pallas import tpu as pltpu

F32 = jnp.float32
BF16 = jnp.bfloat16
HIGHEST = lax.Precision.HIGHEST

NORM_EPS = 1e-6
GLA_HEADS = 16
GLA_DK = 64
GLA_DV = 128
GLA_KEY_W = GLA_HEADS * GLA_DK
GLA_VAL_W = GLA_HEADS * GLA_DV
GLA_GATE_RANK = 16
GLA_TAU = 16.0
GLA_CHUNK = 64
NSA_HEADS = 16
NSA_KV_HEADS = 4
NSA_GROUP = NSA_HEADS // NSA_KV_HEADS
NSA_DIM = 128
NSA_Q_W = NSA_HEADS * NSA_DIM
NSA_KV_W = NSA_KV_HEADS * NSA_DIM
CMP_BLOCK = 32
CMP_STRIDE = 16
SEL_BLOCK = 64
SEL_TOPK = 16
SEL_LOCAL = 2
WINDOW = 512
REL_BUCKETS = 32
REL_MAX_DIST = 128
NEG_INF = -1e30
FORCE_SCORE = 1e4
LOG2E = math.log2(math.e)

LANES = 128
VMEM_LIMIT = 56 * 1024 * 1024
VMEM_LIMIT_KTILED = 62 * 1024 * 1024
ATT_TILE = 256
CMP_TILE = 512
MM_TILE = 1024
GLA_PRE_BLOCK = 256
GLA_UNROLL = 16

U_GQ = 0
U_GK = U_GQ + GLA_KEY_W
U_GV = U_GK + GLA_KEY_W
U_GR = U_GV + GLA_VAL_W
U_NQ = U_GR + GLA_VAL_W
U_MG = U_NQ + NSA_Q_W

NT = (((1,), (1,)), ((), ()))
TN = (((0,), (0,)), ((), ()))


def _params(sem, vmem_limit=VMEM_LIMIT):
    return pltpu.CompilerParams(dimension_semantics=sem, vmem_limit_bytes=vmem_limit)


def _with_casts(kernel_fn, n_in, n_out, n_cast):
    def kernel(*refs, **kw):
        ins = refs[:n_in]
        cast_in = refs[n_in:n_in + n_cast]
        outs = refs[n_in + n_cast:n_in + n_cast + n_out]
        cast_out = refs[n_in + n_cast + n_out:n_in + 2 * n_cast + n_out]
        for src, dst in zip(cast_in, cast_out):
            dst[...] = src[0].astype(dst.dtype)
        kernel_fn(*ins, *outs, *refs[n_in + 2 * n_cast + n_out:], **kw)
    return kernel


def _cast_rows_ok(w, steps):
    rows = w.shape[1]
    return rows % steps == 0 and (rows // steps) % 16 == 0


def _cast_specs(weights, layer, steps, step_of):
    in_specs, out_specs, out_shapes = [], [], []
    for w in weights:
        rows, cols = w.shape[1:]
        assert _cast_rows_ok(w, steps)
        rp = rows // steps
        in_specs.append(pl.BlockSpec((1, rp, cols), lambda *g: (layer, step_of(*g), 0)))
        out_specs.append(pl.BlockSpec((rp, cols), lambda *g: (step_of(*g), 0)))
        out_shapes.append(jax.ShapeDtypeStruct((rows, cols), BF16))
    return in_specs, out_specs, out_shapes


def _rmsnorm_kernel(x_ref, g_ref, o_ref):
    x = x_ref[...].astype(F32)
    ms = jnp.mean(x * x, axis=-1, keepdims=True)
    o_ref[...] = (x * lax.rsqrt(ms + NORM_EPS) * g_ref[...]).astype(o_ref.dtype)


def _rmsnorm(x2, g, out_dtype, tm=512):
    t, d = x2.shape
    return pl.pallas_call(
        _rmsnorm_kernel,
        grid=(t // tm,),
        in_specs=[pl.BlockSpec((tm, d), lambda i: (i, 0)),
                  pl.BlockSpec((1, d), lambda i: (0, 0))],
        out_specs=pl.BlockSpec((tm, d), lambda i: (i, 0)),
        out_shape=jax.ShapeDtypeStruct((t, d), out_dtype),
        compiler_params=_params(("parallel",)),
        name="rmsnorm",
    )(x2, g.reshape(1, d).astype(F32))


def _rmsnorm_proj_kernel(x_ref, g_ref, wt_ref, o_ref, p_ref):
    x = x_ref[...].astype(F32)
    ms = jnp.mean(x * x, axis=-1, keepdims=True)
    h = (x * lax.rsqrt(ms + NORM_EPS) * g_ref[...]).astype(o_ref.dtype)
    o_ref[...] = h
    p_ref[...] = lax.dot_general(h, wt_ref[...], NT, preferred_element_type=F32)


def _rmsnorm_proj(x2, g, w_t, tm=512):
    t, d = x2.shape
    n = w_t.shape[0]
    return pl.pallas_call(
        _rmsnorm_proj_kernel,
        grid=(t // tm,),
        in_specs=[pl.BlockSpec((tm, d), lambda i: (i, 0)),
                  pl.BlockSpec((1, d), lambda i: (0, 0)),
                  pl.BlockSpec((n, d), lambda i: (0, 0))],
        out_specs=[pl.BlockSpec((tm, d), lambda i: (i, 0)),
                   pl.BlockSpec((tm, n), lambda i: (i, 0))],
        out_shape=[jax.ShapeDtypeStruct((t, d), BF16), jax.ShapeDtypeStruct((t, n), F32)],
        compiler_params=_params(("parallel",)),
        name="rmsnorm_proj",
    )(x2, g.reshape(1, d).astype(F32), w_t)


W_ROW_ALIGN = 16


def _w_in_prep_kernel(src_ref, w_ref, ga_ref, ng_ref, o_ref, os_ref, *, q_blocks, q_scale):
    del src_ref
    i = pl.program_id(0)
    is_q = jnp.logical_and(i >= q_blocks[0], i < q_blocks[1])
    scale = jnp.where(is_q, q_scale, 1.0).astype(F32)
    o_ref[...] = (w_ref[0] * scale).astype(o_ref.dtype)

    @pl.when(i == 0)
    def _():
        n_ga = ga_ref.shape[1]
        n_ng = ng_ref.shape[1]
        os_ref[0:n_ga] = ga_ref[0].astype(os_ref.dtype)
        os_ref[n_ga:n_ga + n_ng] = ng_ref[0].astype(os_ref.dtype)
        os_ref[n_ga + n_ng:] = jnp.zeros((os_ref.shape[0] - n_ga - n_ng, os_ref.shape[1]),
                                         os_ref.dtype)


def _w_in_prep(w_in, layer, d, tr=512):
    n = w_in.shape[2]
    s_ga = 2 * GLA_KEY_W + GLA_VAL_W
    s_gr = s_ga + GLA_GATE_RANK
    s_nq = s_gr + GLA_VAL_W
    s_kc = s_nq + NSA_Q_W
    s_ng = s_kc + 6 * NSA_KV_W
    s_mg = s_ng + 3 * NSA_HEADS
    assert n == s_mg + 2 * d
    w_t = jnp.swapaxes(w_in, 1, 2)
    pieces = [(0, s_ga, U_GQ), (s_gr, s_nq, U_GR), (s_nq, s_kc, U_NQ),
              (s_mg, n, U_MG), (s_kc, s_ng, U_MG + 2 * d)]
    src_rows = []
    for s0, s1, d0 in pieces:
        assert s0 % W_ROW_ALIGN == 0 and (s1 - s0) % tr == 0 and d0 == len(src_rows) * tr
        src_rows += [r // W_ROW_ALIGN for r in range(s0, s1, tr)]
    n_main = len(src_rows) * tr
    def rows_at(start, count):
        return pl.BlockSpec((pl.Element(1), pl.Element(count), pl.Element(d)),
                            lambda i, src: (layer, start, 0))

    return pl.pallas_call(
        functools.partial(_w_in_prep_kernel,
                          q_blocks=(U_NQ // tr, (U_NQ + NSA_Q_W) // tr),
                          q_scale=NSA_DIM ** -0.5 * LOG2E),
        grid_spec=pltpu.PrefetchScalarGridSpec(
            num_scalar_prefetch=1,
            grid=(n_main // tr,),
            in_specs=[pl.BlockSpec(
                (pl.Element(1), pl.Element(tr), pl.Element(d)),
                lambda i, src: (layer, src[i] * W_ROW_ALIGN, 0)),
                rows_at(s_ga, s_gr - s_ga), rows_at(s_ng, s_mg - s_ng)],
            out_specs=[pl.BlockSpec((tr, d), lambda i, src: (i, 0)),
                       pl.BlockSpec((LANES, d), lambda i, src: (0, 0))]),
        out_shape=[jax.ShapeDtypeStruct((n_main, d), BF16),
                   jax.ShapeDtypeStruct((LANES, d), BF16)],
        compiler_params=_params(("arbitrary",)),
        name="w_in_prep",
    )(jnp.asarray(src_rows, jnp.int32), w_t, w_t, w_t)


def _mm_nt_kernel(a_ref, bt_ref, o_ref):
    o_ref[...] = lax.dot_general(a_ref[...], bt_ref[...], NT,
                                 preferred_element_type=F32).astype(o_ref.dtype)


def _matmul_nt(a, b_t, out_dtype, tm, tn, name):
    m, k = a.shape
    n = b_t.shape[0]
    return pl.pallas_call(
        _mm_nt_kernel,
        grid=(m // tm, n // tn),
        in_specs=[pl.BlockSpec((tm, k), lambda i, j: (i, 0)),
                  pl.BlockSpec((tn, k), lambda i, j: (j, 0))],
        out_specs=pl.BlockSpec((tm, tn), lambda i, j: (i, j)),
        out_shape=jax.ShapeDtypeStruct((m, n), out_dtype),
        compiler_params=_params(("parallel", "arbitrary")),
        name=name,
    )(a, b_t)


def _mix_kernel(og_ref, wg_ref, on_ref, wn_ref, mg_ref, mn_ref, o_ref):
    yg = jnp.dot(og_ref[...], wg_ref[...], preferred_element_type=F32)
    yn = jnp.dot(on_ref[...], wn_ref[...], preferred_element_type=F32)
    o = (jax.nn.sigmoid(mg_ref[...].astype(F32)) * yg
         + jax.nn.sigmoid(mn_ref[...].astype(F32)) * yn)
    o_ref[...] = o.astype(o_ref.dtype)


def _mix(o_gla, w_g, o_nsa, w_n, u, d, tm=MM_TILE, tn=MM_TILE):
    t = o_gla.shape[0]
    mg_blk = U_MG // tn
    mn_blk = (U_MG + d) // tn
    return pl.pallas_call(
        _mix_kernel,
        grid=(t // tm, d // tn),
        in_specs=[pl.BlockSpec((tm, o_gla.shape[1]), lambda i, j: (i, 0)),
                  pl.BlockSpec((w_g.shape[0], tn), lambda i, j: (0, j)),
                  pl.BlockSpec((tm, o_nsa.shape[1]), lambda i, j: (i, 0)),
                  pl.BlockSpec((w_n.shape[0], tn), lambda i, j: (0, j)),
                  pl.BlockSpec((tm, tn), lambda i, j: (i, mg_blk + j)),
                  pl.BlockSpec((tm, tn), lambda i, j: (i, mn_blk + j))],
        out_specs=pl.BlockSpec((tm, tn), lambda i, j: (i, j)),
        out_shape=jax.ShapeDtypeStruct((t, d), BF16),
        compiler_params=_params(("parallel", "arbitrary")),
        name="mix",
    )(o_gla, w_g, o_nsa, w_n, u, u)


def _mm_res_kernel(a_ref, b_ref, r_ref, o_ref, acc_ref):
    kk = pl.program_id(2)

    @pl.when(kk == 0)
    def _():
        acc_ref[...] = r_ref[...]

    acc_ref[...] += jnp.dot(a_ref[...], b_ref[...], preferred_element_type=F32)

    @pl.when(kk == pl.num_programs(2) - 1)
    def _():
        o_ref[...] = acc_ref[...]


def _out_proj_kernel(a_ref, b_ref, r_ref, g_ref, o_ref, ob_ref, ss_ref):
    y = r_ref[...] + jnp.dot(a_ref[...], b_ref[...], preferred_element_type=F32)
    o_ref[...] = y
    ob_ref[...] = (y * g_ref[...]).astype(ob_ref.dtype)
    ss_ref[...] = jnp.broadcast_to(jnp.sum(y * y, axis=-1, keepdims=True), ss_ref.shape)


def _out_proj(a, b, res, gain, tm=MM_TILE, tn=MM_TILE):
    m, k = a.shape
    _, n = b.shape
    return pl.pallas_call(
        _out_proj_kernel,
        grid=(m // tm, n // tn),
        in_specs=[pl.BlockSpec((tm, k), lambda i, j: (i, 0)),
                  pl.BlockSpec((k, tn), lambda i, j: (0, j)),
                  pl.BlockSpec((tm, tn), lambda i, j: (i, j)),
                  pl.BlockSpec((1, tn), lambda i, j: (0, j))],
        out_specs=[pl.BlockSpec((tm, tn), lambda i, j: (i, j)),
                   pl.BlockSpec((tm, tn), lambda i, j: (i, j)),
                   pl.BlockSpec((tm, LANES), lambda i, j: (i, j))],
        out_shape=[jax.ShapeDtypeStruct((m, n), F32), jax.ShapeDtypeStruct((m, n), BF16),
                   jax.ShapeDtypeStruct((m, n // tn * LANES), F32)],
        compiler_params=_params(("parallel", "arbitrary"), VMEM_LIMIT_KTILED),
        name="out_proj",
    )(a, b, res, gain.reshape(1, n).astype(F32))


def _mlp_up_kernel(a_ref, b_ref, ss_ref, o_ref, *, d):
    acc = jnp.dot(a_ref[...], b_ref[...], preferred_element_type=F32)
    ss = ss_ref[...]
    total = ss[:, :LANES]
    for p in range(1, ss.shape[1] // LANES):
        total = total + ss[:, p * LANES:(p + 1) * LANES]
    inv = 1.0 / (total[:, :1] * (1.0 / d) + NORM_EPS)
    r = jnp.maximum(acc, 0.0)
    o_ref[...] = (r * r * inv).astype(o_ref.dtype)


def _mlp_up(xb, w_up, ssq, tm=MM_TILE, tn=MM_TILE):
    m, k = xb.shape
    _, n = w_up.shape
    return pl.pallas_call(
        functools.partial(_mlp_up_kernel, d=k),
        grid=(m // tm, n // tn),
        in_specs=[pl.BlockSpec((tm, k), lambda i, j: (i, 0)),
                  pl.BlockSpec((k, tn), lambda i, j: (0, j)),
                  pl.BlockSpec((tm, ssq.shape[1]), lambda i, j: (i, 0))],
        out_specs=pl.BlockSpec((tm, tn), lambda i, j: (i, j)),
        out_shape=jax.ShapeDtypeStruct((m, n), BF16),
        compiler_params=_params(("parallel", "arbitrary")),
        name="mlp_up",
    )(xb, w_up, ssq)


def _matmul_residual(a, b, res, tm, tn, tk, name):
    m, k = a.shape
    _, n = b.shape
    return pl.pallas_call(
        _mm_res_kernel,
        grid=(m // tm, n // tn, k // tk),
        in_specs=[pl.BlockSpec((tm, tk), lambda i, j, kk: (i, kk)),
                  pl.BlockSpec((tk, tn), lambda i, j, kk: (kk, j)),
                  pl.BlockSpec((tm, tn), lambda i, j, kk: (i, j))],
        out_specs=pl.BlockSpec((tm, tn), lambda i, j, kk: (i, j)),
        out_shape=jax.ShapeDtypeStruct((m, n), F32),
        scratch_shapes=[pltpu.VMEM((tm, tn), F32)],
        compiler_params=_params(("parallel", "arbitrary", "arbitrary"), VMEM_LIMIT_KTILED),
        name=name,
    )(a, b, res)


def _gla_kernel(q_ref, k_ref, v_ref, r_ref, a_ref, wa_ref, ba_ref, g_ref, o_ref,
                bc_ref, st_ref, *, seq):
    c = GLA_CHUNK
    dv = GLA_DV
    shift = int(math.log2(c))

    nb = GLA_PRE_BLOCK
    prow = lax.broadcasted_iota(jnp.int32, (nb, nb), 0)
    pcol = lax.broadcasted_iota(jnp.int32, (nb, nb), 1)
    same_chunk = lax.shift_right_logical(prow, shift) == lax.shift_right_logical(pcol, shift)
    tri = jnp.where(prow >= pcol, jnp.where(same_chunk, 1.0, 0.0), 0.0).astype(BF16)

    def split(x):
        hi = x.astype(BF16)
        return hi, (x - hi.astype(F32)).astype(BF16)

    wa_hi, wa_lo = split(wa_ref[...])
    wa3 = jnp.concatenate([wa_hi, wa_hi, wa_lo], axis=0)

    def log_decay(b, carry):
        r0 = pl.multiple_of(b * nb, nb)
        a_hi, a_lo = split(a_ref[pl.ds(r0, nb), :])
        z = jnp.dot(jnp.concatenate([a_hi, a_lo, a_hi], axis=1), wa3,
                    preferred_element_type=F32) + ba_ref[...]
        bc_ref[pl.ds(r0, nb), :] = (
            jnp.minimum(z, 0.0) - jnp.log(1.0 + jnp.exp(-jnp.abs(z)))) * (1.0 / GLA_TAU)
        return carry

    def cumulate(b, carry):
        r0 = pl.multiple_of(b * nb, nb)
        la = bc_ref[pl.ds(r0, nb), :]
        cs = jnp.dot(tri, jnp.concatenate(split(la), axis=1), preferred_element_type=F32)
        w = la.shape[1]
        bc_ref[pl.ds(r0, nb), :] = cs[:, :w] + cs[:, w:]
        return carry

    lax.fori_loop(0, seq // nb, log_decay, 0, unroll=8)
    lax.fori_loop(0, seq // nb, cumulate, 0, unroll=8)
    st_ref[...] = jnp.zeros_like(st_ref)

    row2 = lax.broadcasted_iota(jnp.int32, (2 * c, c), 0)
    col2 = lax.broadcasted_iota(jnp.int32, (2 * c, c), 1)
    causal2 = (row2 & (c - 1)) >= col2
    lane = lax.broadcasted_iota(jnp.int32, (1, 2 * GLA_DK), 1)
    head_mask = [(lane < GLA_DK).astype(F32), (lane >= GLA_DK).astype(F32)]
    gvec = g_ref[...]

    def body(i, carry):
        r0 = pl.multiple_of(i * c, c)
        bc = bc_ref[pl.ds(r0, c), :]
        last = bc[c - 1:c, :]
        q = q_ref[pl.ds(r0, c), :].astype(F32) * (GLA_DK ** -0.5)
        k = k_ref[pl.ds(r0, c), :].astype(F32)
        qe = q * jnp.exp(bc)
        ke = (k * jnp.exp(-bc)).astype(BF16)
        kd = (k * jnp.exp(last - bc)).astype(BF16)
        v = v_ref[pl.ds(r0, c), :]
        st = st_ref[...]
        q2 = jnp.concatenate([qe * head_mask[0], qe * head_mask[1]], axis=0).astype(BF16)
        att = lax.dot_general(q2, ke, NT, preferred_element_type=F32)
        att = jnp.where(causal2, att, 0.0).astype(BF16)
        o2 = (jnp.dot(att, v, preferred_element_type=F32)
              + lax.dot_general(q2, st.astype(BF16), NT, preferred_element_type=F32))
        outs = []
        for h in range(2):
            oh = o2[h * c:(h + 1) * c, h * dv:(h + 1) * dv]
            ms = jnp.mean(oh * oh, axis=-1, keepdims=True)
            y = oh * lax.rsqrt(ms + NORM_EPS) * gvec[:, h * dv:(h + 1) * dv]
            rr = r_ref[pl.ds(r0, c), h * dv:(h + 1) * dv].astype(F32)
            outs.append(y * (rr * jax.nn.sigmoid(rr)))
        o_ref[pl.ds(r0, c), :] = jnp.concatenate(outs, axis=1).astype(o_ref.dtype)
        kv = lax.dot_general(v, kd, TN, preferred_element_type=F32)
        st_ref[...] = st * jnp.exp(last) + kv
        return carry

    lax.fori_loop(0, seq // c, body, 0, unroll=GLA_UNROLL)


def _gla(u, small, w_alpha_pad, b_alpha, norm_g, batch, seq, casts=(), layer=0):
    t = batch * seq
    pairs = GLA_HEADS // 2
    kw = 2 * GLA_DK
    vw = 2 * GLA_DV
    c_in, c_out, c_shape = _cast_specs(casts, layer, batch * pairs, lambda b, j: b * pairs + j)
    return pl.pallas_call(
        _with_casts(functools.partial(_gla_kernel, seq=seq), 8, 1, len(casts)),
        grid=(batch, pairs),
        in_specs=[pl.BlockSpec((seq, kw), lambda b, j: (b, U_GQ // kw + j)),
                  pl.BlockSpec((seq, kw), lambda b, j: (b, U_GK // kw + j)),
                  pl.BlockSpec((seq, vw), lambda b, j: (b, U_GV // vw + j)),
                  pl.BlockSpec((seq, vw), lambda b, j: (b, U_GR // vw + j)),
                  pl.BlockSpec((seq, LANES), lambda b, j: (b, 0)),
                  pl.BlockSpec((LANES, kw), lambda b, j: (0, j)),
                  pl.BlockSpec((1, kw), lambda b, j: (0, j)),
                  pl.BlockSpec((1, vw), lambda b, j: (0, j))] + c_in,
        out_specs=[pl.BlockSpec((seq, vw), lambda b, j: (b, j))] + c_out,
        out_shape=[jax.ShapeDtypeStruct((t, GLA_VAL_W), BF16)] + c_shape,
        scratch_shapes=[pltpu.VMEM((seq, kw), F32), pltpu.VMEM((vw, kw), F32)],
        compiler_params=_params(("parallel", "arbitrary")),
        name="gla",
    )(u, u, u, u, small, w_alpha_pad, b_alpha, norm_g, *casts)


def _cmp_cols_per_tile(tq):
    return tq // CMP_STRIDE


def _cmp_front_pad(seq, tq):
    return seq // CMP_STRIDE - _cmp_cols_per_tile(tq)


def _compress_kernel(xk_ref, xv_ref, pk_ref, pv_ref, w1k_ref, w2k_ref, w1v_ref, w2v_ref,
                     ok_ref, ov_ref, xf_ref, *, front):
    n = CMP_STRIDE
    d = NSA_DIM

    def one(x_ref, pos_ref, w1_ref, w2_ref, o_ref):
        nb = x_ref.shape[0] // n
        xf_ref[...] = x_ref[...].astype(F32)
        top = jnp.zeros((nb, w1_ref.shape[1]), F32)
        bot = jnp.zeros((nb, w1_ref.shape[1]), F32)
        for j in range(n):
            x = xf_ref[pl.ds(j, nb, stride=n), :]
            xa = (x + pos_ref[j:j + 1, :]).astype(BF16)
            xb = (x + pos_ref[n + j:n + j + 1, :]).astype(BF16)
            top = top + jnp.dot(xa, w1_ref[j * d:(j + 1) * d, :],
                                preferred_element_type=F32)
            bot = bot + jnp.dot(xb, w1_ref[(n + j) * d:(n + j + 1) * d, :],
                                preferred_element_type=F32)
        hid = top + pltpu.roll(bot, nb - 1, axis=0)
        hid = jnp.maximum(hid, 0.0).astype(BF16)
        total = o_ref.shape[2]
        o_ref[0, 0, 0:front] = jnp.zeros((front, d), o_ref.dtype)
        o_ref[0, 0, front:front + nb] = jnp.dot(
            hid, w2_ref[...], preferred_element_type=F32).astype(o_ref.dtype)
        o_ref[0, 0, front + nb:total] = jnp.zeros((total - front - nb, d), o_ref.dtype)

    one(xk_ref, pk_ref, w1k_ref, w2k_ref, ok_ref)
    one(xv_ref, pv_ref, w1v_ref, w2v_ref, ov_ref)


def _compress(u, pos_k, pos_v, w1k, w2k, w1v, w2v, batch, seq, u_kc, tq):
    nb = seq // CMP_STRIDE
    front = _cmp_front_pad(seq, tq)
    kc_blk = u_kc // NSA_DIM
    vc_blk = kc_blk + NSA_KV_HEADS
    full = lambda a: pl.BlockSpec(a.shape, lambda b, g: (0,) * a.ndim)
    out = jax.ShapeDtypeStruct((batch, NSA_KV_HEADS, 2 * nb, NSA_DIM), BF16)
    ospec = pl.BlockSpec((1, 1, 2 * nb, NSA_DIM), lambda b, g: (b, g, 0, 0))
    return pl.pallas_call(
        functools.partial(_compress_kernel, front=front),
        grid=(batch, NSA_KV_HEADS),
        in_specs=[pl.BlockSpec((seq, NSA_DIM), lambda b, g: (b, kc_blk + g)),
                  pl.BlockSpec((seq, NSA_DIM), lambda b, g: (b, vc_blk + g)),
                  full(pos_k), full(pos_v), full(w1k), full(w2k), full(w1v), full(w2v)],
        out_specs=[ospec, ospec],
        out_shape=[out, out],
        scratch_shapes=[pltpu.VMEM((seq, NSA_DIM), F32)],
        compiler_params=_params(("parallel", "arbitrary")),
        name="nsa_compress",
    )(u, u, pos_k, pos_v, w1k, w2k, w1v, w2v)


def _cmp_select_kernel(q_ref, kc_ref, vc_ref, pc_ref, ms_ref, o_ref, sel_ref, score_ref, *,
                       tq, nb, n_s, top_n, front):
    i = pl.program_id(2)
    d = NSA_DIM
    w0 = pl.multiple_of(i * _cmp_cols_per_tile(tq), _cmp_cols_per_tile(tq))
    kc = kc_ref[0, 0, pl.ds(w0, nb), :]
    vc = vc_ref[0, 0, pl.ds(w0, nb), :]
    rows = NSA_GROUP * tq
    q = jnp.concatenate([q_ref[:, r * d:(r + 1) * d] for r in range(NSA_GROUP)], axis=0)
    col = lax.broadcasted_iota(jnp.int32, (rows, nb), 1)
    before_start = jnp.where(col < front - w0, NEG_INF, 0.0)
    s = lax.dot_general(q, kc, NT, preferred_element_type=F32) + before_start
    bias = pc_ref[...].reshape(rows, LANES)
    s = jnp.concatenate([s[:, :nb - LANES], s[:, nb - LANES:] + bias], axis=1)
    m = jnp.max(s, axis=-1, keepdims=True)
    e = jnp.exp2(s - m)
    p = e * (1.0 / jnp.sum(e, axis=-1, keepdims=True))
    p = jnp.where(s > 0.5 * NEG_INF, p, 0.0)
    o = jnp.dot(p.astype(BF16), vc, preferred_element_type=F32)
    psum = p[0:tq]
    for r in range(NSA_GROUP):
        o_ref[:, r * d:(r + 1) * d] = o[r * tq:(r + 1) * tq].astype(o_ref.dtype)
        if r:
            psum = psum + p[r * tq:(r + 1) * tq]
    p_hi = psum.astype(BF16)
    p_lo = (psum - p_hi.astype(F32)).astype(BF16)
    imp2 = lax.dot_general(ms_ref[pl.ds(w0, nb), :], jnp.concatenate([p_hi, p_lo], axis=0),
                           (((0,), (1,)), ((), ())), preferred_element_type=F32)
    imp = imp2[:, :tq] + imp2[:, tq:]
    blk = lax.broadcasted_iota(jnp.int32, (n_s, tq), 0)
    pos = i * tq + lax.broadcasted_iota(jnp.int32, (n_s, tq), 1)
    cur = lax.shift_right_logical(pos, int(math.log2(SEL_BLOCK)))
    forced = jnp.where(blk == 0, 1, jnp.where(blk <= cur, jnp.where(blk > cur - SEL_LOCAL, 1, 0), 0))
    score = jnp.where(forced == 1, FORCE_SCORE, jnp.where(blk <= cur, imp, -FORCE_SCORE))
    score_ref[...] = score
    per_trip = tq // SEL_BLOCK

    def count_ahead(g, rank):
        for r in range(per_trip):
            j = g * per_trip + r
            rj = score_ref[pl.ds(j, 1), :]
            tie = jnp.where(rj == score, jnp.where(blk > j, 1, 0), 0)
            rank = rank + jnp.where(rj > score, 1, tie)
        return rank

    rank = lax.fori_loop(0, i + 1, count_ahead, jnp.zeros((n_s, tq), jnp.int32))
    selb = jnp.where(rank < top_n, 0.0, NEG_INF)
    if n_s < LANES:
        selb = jnp.concatenate([selb, jnp.zeros((LANES - n_s, tq), F32)], axis=0)
    sel_ref[0, 0] = selb.T.astype(sel_ref.dtype)


def _cmp_select(u, kc, vc, pc, ms, batch, seq, tq, casts=(), layer=0):
    t = batch * seq
    nb = seq // CMP_STRIDE
    n_s = seq // SEL_BLOCK
    top_n = min(SEL_TOPK, n_s)
    nq = seq // tq
    qw = NSA_GROUP * NSA_DIM
    kv_spec = pl.BlockSpec((1, 1, 2 * nb, NSA_DIM), lambda b, g, i: (b, g, 0, 0))
    c_in, c_out, c_shape = _cast_specs(
        casts, layer, batch * NSA_KV_HEADS * nq,
        lambda b, g, i: (b * NSA_KV_HEADS + g) * nq + i)
    return pl.pallas_call(
        _with_casts(functools.partial(_cmp_select_kernel, tq=tq, nb=nb, n_s=n_s, top_n=top_n,
                                      front=_cmp_front_pad(seq, tq)), 5, 2, len(casts)),
        grid=(batch, NSA_KV_HEADS, nq),
        in_specs=[pl.BlockSpec((tq, qw), lambda b, g, i: (b * nq + i, U_NQ // qw + g)),
                  kv_spec, kv_spec,
                  pl.BlockSpec((NSA_GROUP, tq, LANES), lambda b, g, i: (g, 0, 0)),
                  pl.BlockSpec((2 * nb, n_s), lambda b, g, i: (0, 0))] + c_in,
        out_specs=[pl.BlockSpec((tq, qw), lambda b, g, i: (b * nq + i, g)),
                   pl.BlockSpec((1, 1, tq, LANES), lambda b, g, i: (b, g, i, 0))] + c_out,
        out_shape=[jax.ShapeDtypeStruct((t, NSA_Q_W), BF16),
                   jax.ShapeDtypeStruct((batch, NSA_KV_HEADS, seq, LANES), BF16)] + c_shape,
        scratch_shapes=[pltpu.VMEM((n_s, tq), F32)],
        compiler_params=_params(("parallel", "parallel", "arbitrary")),
        name="nsa_cmp_select",
    )(u, kc, vc, pc, ms, *casts)


def _flash_kernel(*refs, mode, tq, tk):
    sel = mode == "sel"
    if sel:
        (q_ref, k_ref, v_ref, sel_ref, pw_ref, o_ref,
         qs_ref, ks_ref, m_ref, l_ref, acc_ref) = refs
    else:
        (q_ref, k_ref, v_ref, pw_ref, oc_ref, os_ref, ng_ref, o_ref, vs_ref) = refs
    i = pl.program_id(2)
    d = NSA_DIM

    if sel:
        @pl.when(i == 0)
        def _():
            seq = k_ref.shape[0]
            ks_ref[:, :d] = k_ref[...]
            krow = lax.broadcasted_iota(jnp.int32, (seq, LANES), 0)
            klane = lax.broadcasted_iota(jnp.int32, (seq, LANES), 1)
            kblk = lax.shift_right_logical(krow, int(math.log2(SEL_BLOCK)))
            ks_ref[:, d:] = jnp.where(kblk == klane, 1.0, 0.0).astype(ks_ref.dtype)

        for r in range(NSA_GROUP):
            qs_ref[r * tq:(r + 1) * tq, :d] = q_ref[:, r * d:(r + 1) * d]
            qs_ref[r * tq:(r + 1) * tq, d:] = sel_ref[0, 0]
        m_ref[...] = jnp.full_like(m_ref, NEG_INF)
        l_ref[...] = jnp.zeros_like(l_ref)
        acc_ref[...] = jnp.zeros_like(acc_ref)

    if not sel:
        group = pl.program_id(1)
        per_group = 3 * NSA_GROUP
        shift = lax.rem(LANES - per_group * group, LANES)
        sig = pltpu.roll(jax.nn.sigmoid(ng_ref[...]), shift, axis=1)

        def gate(h, br):
            c = GLA_GATE_RANK + 3 * h + br
            return jnp.broadcast_to(sig[:, c:c + 1], (tq, d))

        @pl.when(i == 0)
        def _():
            vs_ref[:, :d] = v_ref[...]
            vs_ref[:, d:] = jnp.ones((v_ref.shape[0], d), vs_ref.dtype)

    chains = [(0, NSA_GROUP)] if sel else [(r, r + 1) for r in range(NSA_GROUP)]

    def chain_logits(h0, h1, kt):
        q = qs_ref[h0 * tq:h1 * tq, :] if sel else q_ref[:, h0 * d:h1 * d]
        return lax.dot_general(q, kt, NT, preferred_element_type=F32)

    def online_update(rows, s, vt):
        m_prev = m_ref[rows]
        m_new = jnp.maximum(m_prev, jnp.max(s, axis=-1, keepdims=True))
        alpha = jnp.exp2(m_prev - m_new)
        p = jnp.exp2(s - jnp.concatenate([m_new] * (s.shape[1] // LANES), axis=1))
        l_new = alpha * l_ref[rows] + jnp.sum(p, axis=-1, keepdims=True)
        acc_new = alpha * acc_ref[rows] + jnp.dot(p.astype(BF16), vt,
                                                  preferred_element_type=F32)
        return m_new, l_new, acc_new

    def last_tiles(n):
        width = n * tk
        k0 = pl.multiple_of((i + 1 - n) * tk, tk)
        kt = ks_ref[pl.ds(k0, width), :] if sel else k_ref[pl.ds(k0, width), :]
        vt = v_ref[pl.ds(k0, width), :] if sel else vs_ref[pl.ds(k0, width), :]
        for h0, h1 in chains:
            rows = slice(h0 * tq, h1 * tq)
            s = chain_logits(h0, h1, kt) + pw_ref[0, rows, (3 - n) * tk:]
            if sel:
                _, l, acc = online_update(rows, s, vt)
            else:
                p = jnp.exp2(s - jnp.max(s, axis=-1, keepdims=True))
                acc = jnp.dot(p.astype(BF16), vt, preferred_element_type=F32)
                acc, l = acc[:, :d], acc[:, d:]
            out = acc * (1.0 / l)
            for h in range(h0, h1):
                out_h = out[(h - h0) * tq:(h - h0 + 1) * tq, :]
                if not sel:
                    lanes = slice(h * d, (h + 1) * d)
                    out_h = (gate(h, 2) * out_h
                             + gate(h, 0) * oc_ref[:, lanes].astype(F32)
                             + gate(h, 1) * os_ref[:, lanes].astype(F32))
                o_ref[:, h * d:(h + 1) * d] = out_h.astype(o_ref.dtype)

    if sel:
        n_far = jnp.maximum(i - 1, 0)
        odd = n_far % 2

        def far_pair(j):
            k0 = pl.multiple_of(j * (2 * tk), 2 * tk)
            kt = ks_ref[pl.ds(k0, 2 * tk), :]
            vt = v_ref[pl.ds(k0, 2 * tk), :]
            for h0, h1 in chains:
                rows = slice(h0 * tq, h1 * tq)
                m_ref[rows], l_ref[rows], acc_ref[rows] = online_update(
                    rows, chain_logits(h0, h1, kt), vt)

        n_pairs = n_far // 2

        def far(j, carry):
            far_pair(2 * j)
            far_pair(2 * j + 1)
            return carry
        lax.fori_loop(0, n_pairs // 2, far, 0)
        pl.when(n_pairs % 2 == 1)(lambda: far_pair(n_pairs - 1))
        pl.when(i == 0)(lambda: last_tiles(1))
        pl.when(jnp.logical_and(i >= 1, odd == 0))(lambda: last_tiles(2))
        pl.when(odd == 1)(lambda: last_tiles(3))
    else:
        pl.when(i == 0)(lambda: last_tiles(1))
        pl.when(i == 1)(lambda: last_tiles(2))
        pl.when(i >= 2)(lambda: last_tiles(3))


def _flash(u, k_blk, v_blk, pw, batch, seq, mode, sel=None, merge=None, casts=(), layer=0,
           tq=ATT_TILE, tk=ATT_TILE):
    assert tq == tk and 2 * tk >= REL_MAX_DIST and 3 * tk > WINDOW >= 2 * tk
    t = batch * seq
    nq = seq // tq
    qw = NSA_GROUP * NSA_DIM
    rows = NSA_GROUP * tq
    d = NSA_DIM
    q_spec = pl.BlockSpec((tq, qw), lambda b, g, i: (b * nq + i, U_NQ // qw + g))
    k_spec = pl.BlockSpec((seq, d), lambda b, g, i: (b, k_blk + g))
    v_spec = pl.BlockSpec((seq, d), lambda b, g, i: (b, v_blk + g))
    b_spec = pl.BlockSpec((1, rows, 3 * tk), lambda b, g, i: (g, 0, 0))
    if mode == "sel":
        in_specs = [q_spec, k_spec, v_spec,
                    pl.BlockSpec((1, 1, tq, LANES), lambda b, g, i: (b, g, i, 0)), b_spec]
        args = (u, u, u, sel, pw)
        scratch = ([pltpu.VMEM((rows, 2 * d), BF16), pltpu.VMEM((seq, 2 * d), BF16)]
                   + [pltpu.VMEM((rows, LANES), F32)] * 3)
    else:
        o_spec = pl.BlockSpec((tq, qw), lambda b, g, i: (b * nq + i, g))
        o_cmp, o_sel, gate_logits = merge
        in_specs = [q_spec, k_spec, v_spec, b_spec, o_spec, o_spec,
                    pl.BlockSpec((tq, LANES), lambda b, g, i: (b * nq + i, 0))]
        args = (u, u, u, pw, o_cmp, o_sel, gate_logits)
        scratch = [pltpu.VMEM((seq, 2 * d), BF16)]
    c_in, c_out, c_shape = _cast_specs(
        casts, layer, batch * NSA_KV_HEADS * nq,
        lambda b, g, i: (b * NSA_KV_HEADS + g) * nq + i)
    return pl.pallas_call(
        _with_casts(functools.partial(_flash_kernel, mode=mode, tq=tq, tk=tk),
                    len(in_specs), 1, len(casts)),
        grid=(batch, NSA_KV_HEADS, nq),
        in_specs=in_specs + c_in,
        out_specs=[pl.BlockSpec((tq, qw), lambda b, g, i: (b * nq + i, g))] + c_out,
        out_shape=[jax.ShapeDtypeStruct((t, NSA_Q_W), BF16)] + c_shape,
        scratch_shapes=scratch,
        compiler_params=_params(("parallel", "parallel", "arbitrary")),
        name="nsa_flash_" + mode,
    )(*args, *casts)


def _rel_bucket(dist):
    n = jnp.maximum(dist, 0)
    max_exact = REL_BUCKETS // 2
    nf = jnp.maximum(n, max_exact).astype(F32)
    large = max_exact + (jnp.log(nf / max_exact) / math.log(REL_MAX_DIST / max_exact)
                         * (REL_BUCKETS - max_exact)).astype(jnp.int32)
    large = jnp.minimum(large, REL_BUCKETS - 1)
    return jnp.where(n < max_exact, n, large)


def _bias_by_distance(table, dist):
    onehot = (_rel_bucket(jnp.asarray(dist, jnp.int32))[..., None]
              == jnp.arange(REL_BUCKETS, dtype=jnp.int32)).astype(F32)
    return jnp.dot(onehot, table, precision=HIGHEST)


def _bias_tables(rel_table, seq, tq, tk, tq_cmp):
    table = rel_table.astype(F32) * LOG2E
    heads = table.shape[1]
    far = table[REL_BUCKETS - 1]
    period = 3 * tk + 1
    vec = _bias_by_distance(table, np.arange(2 * tk)) - far
    vec = jnp.concatenate([vec, jnp.full((period - 2 * tk, heads), NEG_INF, F32)], axis=0).T
    skew = jnp.tile(vec, (1, tk))[:, :tk * (period - 1)].reshape(heads, tk, period - 1)
    tiles = skew[:, :, :2 * tq].transpose(0, 2, 1)
    p0 = tiles[:, :tq].reshape(NSA_KV_HEADS, NSA_GROUP * tq, tk)
    p1 = tiles[:, tq:].reshape(NSA_KV_HEADS, NSA_GROUP * tq, tk)
    a2 = (np.arange(NSA_GROUP * tq) % tq)[:, None]
    edge = np.where(2 * tk + a2 - np.arange(tk)[None, :] < WINDOW, 0.0, NEG_INF).astype(np.float32)
    edge = jnp.broadcast_to(jnp.asarray(edge), p0.shape)
    pw_sel = jnp.concatenate([jnp.zeros_like(p0), p1, p0], axis=2)
    pw_win = jnp.concatenate([edge, p1, p0], axis=2)
    front = _cmp_front_pad(seq, tq_cmp)
    nb = seq // CMP_STRIDE
    a = np.arange(tq_cmp)[:, None]
    rel_blk = np.arange(nb - LANES, nb)[None, :] - front
    dc = a - CMP_STRIDE * rel_blk - (CMP_BLOCK - 1)
    assert (a - CMP_STRIDE * (nb - LANES - 1 - front) - (CMP_BLOCK - 1)).min() >= REL_MAX_DIST
    pc = _bias_by_distance(table, np.maximum(dc, 0)) - far
    pc = jnp.where(jnp.asarray(dc >= 0)[..., None], pc, NEG_INF).transpose(2, 0, 1)
    return pw_sel, pw_win, pc


def _cmp_to_sel_matrix(seq, tq):
    nb = seq // CMP_STRIDE
    n_c = nb - 1
    n_s = seq // SEL_BLOCK
    front = _cmp_front_pad(seq, tq)
    m_mat = np.zeros((2 * nb, n_s), np.float32)
    j = np.arange(n_s)
    for m in range(SEL_BLOCK // CMP_STRIDE):
        for n in range(CMP_BLOCK // CMP_STRIDE):
            c = (SEL_BLOCK // CMP_STRIDE) * j + m - n
            ok = (c >= 0) & (c < n_c)
            np.add.at(m_mat, (front + c[ok], j[ok]), 1.0)
    return jnp.asarray(m_mat)


def _nsa(u, small, pos_k, pos_v, w1k, w2k, w1v, w2v, rel_table, batch, seq, u_kc,
         casts=(), layer=0):
    tq = ATT_TILE
    tq_cmp = min(CMP_TILE, seq)
    kc_blk = u_kc // NSA_DIM
    ksl_blk = kc_blk + 2 * NSA_KV_HEADS
    vsl_blk = kc_blk + 3 * NSA_KV_HEADS
    kw_blk = kc_blk + 4 * NSA_KV_HEADS
    vw_blk = kc_blk + 5 * NSA_KV_HEADS
    pw_sel, pw_win, pc = _bias_tables(rel_table, seq, tq, tq, tq_cmp)
    ms = _cmp_to_sel_matrix(seq, tq_cmp).astype(BF16)
    kc, vc = _compress(u, pos_k.astype(F32), pos_v.astype(F32),
                       w1k.astype(BF16), w2k.astype(BF16),
                       w1v.astype(BF16), w2v.astype(BF16), batch, seq, u_kc, tq_cmp)
    o_cmp, sel, *cast = _cmp_select(u, kc, vc, pc, ms, batch, seq, tq_cmp, casts, layer)
    o_sel, = _flash(u, ksl_blk, vsl_blk, pw_sel, batch, seq, "sel", sel=sel)
    o_nsa, = _flash(u, kw_blk, vw_blk, pw_win, batch, seq, "win", merge=(o_cmp, o_sel, small))
    return o_nsa, cast


def kernel(x, g_mix_norm, w_in, w_alpha2, b_alpha, gla_norm_g, cmp_pos_k, cmp_pos_v,
           phi_k_w1, phi_k_w2, phi_v_w1, phi_v_w2, rel_bias_table, w_gla_proj,
           w_nsa_proj, w_out, g_mlp_norm, w_up, w_down, g_final_norm):
    batch, seq, d = x.shape
    t = batch * seq
    depth = w_in.shape[0]
    u_kc = U_MG + 2 * d
    xf = x.reshape(t, d)
    for l in range(depth):
        w_main_t, w_small_t = _w_in_prep(w_in, l, d)
        wa_pad = jnp.concatenate(
            [w_alpha2[l], jnp.zeros((LANES - GLA_GATE_RANK, GLA_KEY_W), w_alpha2.dtype)],
            axis=0).astype(F32)

        h, small = _rmsnorm_proj(xf, g_mix_norm[l], w_small_t)
        u = _matmul_nt(h, w_main_t, BF16, MM_TILE, MM_TILE, "in_proj")
        gla_steps = batch * (GLA_HEADS // 2)
        cmp_steps = batch * NSA_KV_HEADS * (seq // min(CMP_TILE, seq))
        gla_casts = [w for w in (w_gla_proj, w_nsa_proj, w_out) if _cast_rows_ok(w, gla_steps)]
        nsa_casts = [w for w in (w_up, w_down) if _cast_rows_ok(w, cmp_steps)]
        o_gla, *gla_cast = _gla(u, small, wa_pad, b_alpha[l].reshape(1, -1).astype(F32),
                                gla_norm_g[l].reshape(1, -1).astype(F32), batch, seq,
                                casts=gla_casts, layer=l)
        o_nsa, nsa_cast = _nsa(
            u, small, cmp_pos_k[l], cmp_pos_v[l], phi_k_w1[l], phi_k_w2[l], phi_v_w1[l],
            phi_v_w2[l], rel_bias_table, batch, seq, u_kc, nsa_casts, l)
        done = {id(w): c for w, c in zip(gla_casts + nsa_casts, gla_cast + nsa_cast)}
        bf16 = lambda w: done[id(w)] if id(w) in done else w[l].astype(BF16)
        mix = _mix(o_gla, bf16(w_gla_proj), o_nsa, bf16(w_nsa_proj), u, d)
        xf, xb, ssq = _out_proj(mix, bf16(w_out), xf, g_mlp_norm[l])
        act = _mlp_up(xb, bf16(w_up), ssq)
        xf = _matmul_residual(act, bf16(w_down), xf, MM_TILE, MM_TILE, 4096, "mlp_down")
    out = _rmsnorm(xf, g_final_norm, F32)
    return out.reshape(batch, seq, d)
```
